```python
import math
import jax
import jax.numpy as jnp
from jax import lax
import numpy as np

D_MODEL = 2048
BATCH = 2
SEQ = 4096
DEPTH = 2
DEC_BATCH = 128
DEC_SEQ = 1
PAST_LEN = 2048
PAGE_SIZE = 128

BRANCH = D_MODEL // 4
MIX = 4 * BRANCH
H_DIFF = 4
DQK = BRANCH // (2 * H_DIFF)
DV = 2 * DQK
Q_BLOCK = 128
R_HEAD = 64
H_RWKV = BRANCH // R_HEAD
W_LORA = 64
A_LORA = 64
V_LORA = 32
N_SHIFT = 3 * BRANCH + W_LORA + A_LORA
DECAY_SCALE = 0.6065306597
LNX_EPS = 64e-5
M_HEADDIM = 64
H_MAMBA = BRANCH // M_HEADDIM
M_GROUPS = 2
M_STATE = 128
M_CONV = 4
CONV_CH = BRANCH + 2 * M_GROUPS * M_STATE
SSD_CHUNK = 128
H_X = 4
D_X = BRANCH // H_X
N_MEM = 256
RMS_EPS = 1e-5
IN_SIZES = (BRANCH, BRANCH, BRANCH, BRANCH, N_SHIFT, BRANCH, BRANCH, CONV_CH, H_MAMBA, BRANCH, BRANCH)
N_IN = 8 * BRANCH + N_SHIFT + CONV_CH + H_MAMBA

kernel_name = 'hymba_diff_rwkv7_mamba2_memory_step'


def rmsnorm(x, g, eps=RMS_EPS):
    xf = x.astype(jnp.float32)
    y = xf * lax.rsqrt(jnp.mean(xf * xf, axis=-1, keepdims=True) + eps)
    return (y * g.astype(jnp.float32)).astype(x.dtype)


def split_in(p):
    offs = [int(o) for o in np.cumsum(IN_SIZES)[:-1]]
    return jnp.split(p, offs, axis=-1)


def alibi_slopes():
    return 2.0 ** (-8.0 * jnp.arange(1, H_DIFF + 1, dtype=jnp.float32) / H_DIFF)


def diff_attn(q, k, v, q_pos, k_pos, lam):
    s = jnp.einsum('bqhcd,bkhcd->bhcqk', q, k, preferred_element_type=jnp.float32) * (DQK ** -0.5)
    dist = (q_pos[:, None] - k_pos[None, :]).astype(jnp.float32)
    bias = jnp.where(dist >= 0, -alibi_slopes()[:, None, None] * dist, -jnp.inf)
    pr = jax.nn.softmax(s + bias[None, :, None], axis=-1)
    w = pr[:, :, 0] - lam * pr[:, :, 1]
    return jnp.einsum('bhqk,bkhv->bqhv', w.astype(v.dtype), v, preferred_element_type=jnp.float32)


def diff_attn_prompt(q, k, v, lam):
    b, t = q.shape[:2]
    nb = t // Q_BLOCK
    qb = jnp.moveaxis(q.reshape(b, nb, Q_BLOCK, H_DIFF, 2, DQK), 1, 0)
    k_pos = jnp.arange(t)

    def one_block(args):
        q_blk, i = args
        return diff_attn(q_blk, k, v, i * Q_BLOCK + jnp.arange(Q_BLOCK), k_pos, lam)

    o = lax.map(one_block, (qb, jnp.arange(nb)))
    return jnp.moveaxis(o, 0, 1).reshape(b, t, H_DIFF, DV)


def rwkv7_scan(r, w, k, v, a, bb, s0):
    def step(s, inp):
        r_t, w_t, k_t, v_t, a_t, b_t = inp
        sa = jnp.einsum('bhvk,bhk->bhv', s, a_t)
        s = s * w_t[:, :, None, :] + sa[..., None] * b_t[:, :, None, :] + v_t[..., None] * k_t[:, :, None, :]
        return s, jnp.einsum('bhvk,bhk->bhv', s, r_t)

    seq = tuple(jnp.moveaxis(z, 1, 0) for z in (r, w, k, v, a, bb))
    s, y = lax.scan(step, s0, seq)
    return jnp.moveaxis(y, 0, 1), s


def segsum(a):
    t = a.shape[-1]
    x = jnp.broadcast_to(a[..., :, None], a.shape + (t,))
    x = jnp.where(jnp.tril(jnp.ones((t, t), bool), -1), x, 0.0)
    cs = jnp.cumsum(x, axis=-2)
    return jnp.where(jnp.tril(jnp.ones((t, t), bool)), cs, -jnp.inf)


def ssd_scan(xdt, adt, bh, ch, s0, chunk):
    b, t, h, p = xdt.shape
    n = bh.shape[-1]
    nc = t // chunk
    X = xdt.reshape(b, nc, chunk, h, p)
    Bc = bh.reshape(b, nc, chunk, h, n)
    Cc = ch.reshape(b, nc, chunk, h, n)
    A = jnp.moveaxis(adt.reshape(b, nc, chunk, h), 3, 1)
    a_cs = jnp.cumsum(A, axis=-1)
    L = jnp.exp(segsum(A))
    y_diag = jnp.einsum('bhcls,bcshp->bclhp', jnp.einsum('bclhn,bcshn->bhcls', Cc, Bc) * L, X)
    decay_states = jnp.exp(a_cs[..., -1:] - a_cs)
    states = jnp.einsum('bclhn,bhcl,bclhp->bchpn', Bc, decay_states, X)
    states = jnp.concatenate([s0[:, None], states], axis=1)
    chunk_decay = jnp.exp(segsum(jnp.pad(a_cs[..., -1], ((0, 0), (0, 0), (1, 0)))))
    new_states = jnp.einsum('bhzc,bchpn->bzhpn', chunk_decay, states)
    y_off = jnp.einsum('bclhn,bchpn,bhcl->bclhp', Cc, new_states[:, :-1], jnp.exp(a_cs))
    return (y_diag + y_off).reshape(b, t, h, p), new_states[:, -1]


def memory_kv(mem, g, wk, wv):
    b = mem.shape[0]
    m = rmsnorm(mem, g)
    return (m @ wk).reshape(b, N_MEM, H_X, D_X), (m @ wv).reshape(b, N_MEM, H_X, D_X)


def hybrid_layer(x, l, lp, past_kv, rw_prev, rw_state, conv_buf, ssm_state, mem_k, mem_v, v_first):
    f32 = jnp.float32
    b, t, _ = x.shape
    xdtype = x.dtype
    h = rmsnorm(x, lp['norm_g'])
    p = h @ lp['w_in']
    dq, dk, dvv, dg, rs, rg, mz, mxbc, mdt, xq, xg = split_in(p)

    q = dq.reshape(b, t, H_DIFF, 2, DQK)
    k = dk.reshape(b, t, H_DIFF, 2, DQK)
    v = dvv.reshape(b, t, H_DIFF, DV)
    lam_init = 0.8 - 0.6 * math.exp(-0.3 * l)
    lam = (jnp.exp(jnp.sum(lp['lam_q1'] * lp['lam_k1'])) - jnp.exp(jnp.sum(lp['lam_q2'] * lp['lam_k2'])) + lam_init)
    if past_kv is None:
        o = diff_attn_prompt(q, k, v, lam)
    else:
        k_past, v_past = past_kv
        pos0 = k_past.shape[1]
        o = diff_attn(q, jnp.concatenate([k_past.astype(k.dtype), k], axis=1),
                      jnp.concatenate([v_past.astype(v.dtype), v], axis=1),
                      pos0 + jnp.arange(t), jnp.arange(pos0 + t), lam)
    o = rmsnorm(o, lp['subln_g']) * (1.0 - lam_init)
    out_a = o.reshape(b, t, BRANCH) * jax.nn.silu(dg.astype(f32))

    rs32 = rs.astype(f32)
    prev = jnp.concatenate([rw_prev[:, None].astype(f32), rs32[:, :-1]], axis=1)
    xs = rs32 + (prev - rs32) * lp['shift_mu']
    r, kr, vr, wlo, alo = jnp.split(xs, [BRANCH, 2 * BRANCH, 3 * BRANCH, 3 * BRANCH + W_LORA], axis=-1)
    logw = -DECAY_SCALE * jax.nn.sigmoid(lp['w0'] + jnp.tanh(wlo) @ lp['w_up'])
    a = jax.nn.sigmoid(lp['a0'] + alo @ lp['a_up'])
    if l == 0:
        v_first = vr
    else:
        vr = vr + (v_first - vr) * jax.nn.sigmoid(lp['vres_0'] + (vr @ lp['vres_a']) @ lp['vres_b'])
    hd = lambda z: z.reshape(b, t, H_RWKV, R_HEAD)
    kk = hd(kr * lp['k_k'])
    kk = kk / jnp.maximum(jnp.sqrt(jnp.sum(kk * kk, axis=-1, keepdims=True)), 1e-12)
    kr = kr * (1.0 + (a - 1.0) * lp['k_a'])
    rh, kh, vh, ah = hd(r), hd(kr), hd(vr), hd(a)
    y, rw_state_new = rwkv7_scan(rh, jnp.exp(hd(logw)), kh, vh, -kk, kk * ah, rw_state.astype(f32))
    mu = jnp.mean(y, axis=-1, keepdims=True)
    var = jnp.mean(jnp.square(y - mu), axis=-1, keepdims=True)
    y = ((y - mu) * lax.rsqrt(var + LNX_EPS)).reshape(b, t, BRANCH) * lp['lnx_g'] + lp['lnx_b']
    y = y + (jnp.sum(rh * kh * lp['r_k'], axis=-1, keepdims=True) * vh).reshape(b, t, BRANCH)
    out_b = y * jax.nn.silu(rg.astype(f32))
    new_rw_prev = rs[:, -1]

    buf = jnp.concatenate([conv_buf.astype(mxbc.dtype), mxbc], axis=1)
    xbc = lp['conv_b']
    for j in range(M_CONV):
        xbc = xbc + buf[:, j:j + t].astype(f32) * lp['conv_w'][j]
    xbc = jax.nn.silu(xbc)
    xm, bm, cm = jnp.split(xbc, [BRANCH, BRANCH + M_GROUPS * M_STATE], axis=-1)
    dt = jax.nn.softplus(mdt.astype(f32) + lp['dt_bias'])
    A = -jnp.exp(lp['a_log'])
    xh = xm.reshape(b, t, H_MAMBA, M_HEADDIM)
    rep = H_MAMBA // M_GROUPS
    bh = jnp.repeat(bm.reshape(b, t, M_GROUPS, M_STATE), rep, axis=2)
    chh = jnp.repeat(cm.reshape(b, t, M_GROUPS, M_STATE), rep, axis=2)
    chunk = SSD_CHUNK if t % SSD_CHUNK == 0 else t
    ym, ssm_new = ssd_scan(xh * dt[..., None], A * dt, bh, chh, ssm_state.astype(f32), chunk)
    ym = ym + lp['d_skip'][:, None] * xh
    out_m = rmsnorm(ym.reshape(b, t, BRANCH) * jax.nn.silu(mz.astype(f32)), lp['gnorm_g'])
    new_conv = buf[:, -(M_CONV - 1):]

    qx = xq.reshape(b, t, H_X, D_X)
    s = jnp.einsum('bqhd,bmhd->bhqm', qx, mem_k.astype(qx.dtype), preferred_element_type=f32) * (D_X ** -0.5)
    pm = jax.nn.softmax(s, axis=-1)
    ox = jnp.einsum('bhqm,bmhd->bqhd', pm, mem_v.astype(f32))
    out_x = ox.reshape(b, t, BRANCH) * jax.nn.silu(xg.astype(f32))

    mixed = jnp.concatenate([out_a, out_b, out_m, out_x], axis=-1).astype(xdtype)
    x = x + mixed @ lp['w_out']
    return x, (k, v, new_rw_prev, rw_state_new, new_conv, ssm_new), v_first


def setup_inputs(seed: int = 0) -> dict:
    key = jax.random.key(seed)
    ks = iter(jax.random.split(key, 64))
    f32 = jnp.float32

    def nrm(shape, scale):
        return jax.random.normal(next(ks), shape, f32) * scale

    def gain(shape):
        return 1.0 + nrm(shape, 0.02)

    n_pages = PAST_LEN // PAGE_SIZE
    n_pool = (DEC_BATCH * n_pages * 5) // 4
    perm = jax.random.permutation(next(ks), n_pool)[: DEC_BATCH * n_pages]
    page_table = perm.reshape(DEC_BATCH, n_pages).astype(jnp.int32)

    dt0 = jnp.exp(jax.random.uniform(next(ks), (DEPTH, H_MAMBA), f32) * (math.log(0.1) - math.log(0.001)) + math.log(0.001))
    dt_bias = dt0 + jnp.log(-jnp.expm1(-dt0))
    a_log = jnp.log(jax.random.uniform(next(ks), (DEPTH, H_MAMBA), f32, 1.0, 16.0))

    return {
        'x_prompt': nrm((BATCH, SEQ, D_MODEL), 1.0),
        'x_sample': nrm((DEC_BATCH, DEC_SEQ, D_MODEL), 1.0),
        'cache_diff_k': nrm((DEPTH, n_pool, PAGE_SIZE, H_DIFF, 2, DQK), 1.0),
        'cache_diff_v': nrm((DEPTH, n_pool, PAGE_SIZE, H_DIFF, DV), 1.0),
        'cache_mem_k': nrm((DEPTH, DEC_BATCH, N_MEM, H_X, D_X), 1.0),
        'cache_mem_v': nrm((DEPTH, DEC_BATCH, N_MEM, H_X, D_X), 1.0),
        'state_rwkv_shift': nrm((DEPTH, DEC_BATCH, N_SHIFT), 1.0),
        'state_rwkv': nrm((DEPTH, DEC_BATCH, H_RWKV, R_HEAD, R_HEAD), 0.3),
        'state_conv': nrm((DEPTH, DEC_BATCH, M_CONV - 1, CONV_CH), 1.0),
        'state_ssm': nrm((DEPTH, DEC_BATCH, H_MAMBA, M_HEADDIM, M_STATE), 0.3),
        'page_table': page_table,
        'mem_prompt': nrm((BATCH, N_MEM, D_MODEL), 1.0),
        'norm_g': gain((DEPTH, D_MODEL)),
        'w_in': nrm((DEPTH, D_MODEL, N_IN), D_MODEL ** -0.5),
        'w_out': nrm((DEPTH, MIX, D_MODEL), MIX ** -0.5),
        'lam_q1': nrm((DEPTH, DQK), 0.1),
        'lam_k1': nrm((DEPTH, DQK), 0.1),
        'lam_q2': nrm((DEPTH, DQK), 0.1),
        'lam_k2': nrm((DEPTH, DQK), 0.1),
        'subln_g': gain((DEPTH, DV)),
        'shift_mu': jax.random.uniform(next(ks), (DEPTH, N_SHIFT), f32),
        'w0': nrm((DEPTH, BRANCH), 0.5),
        'w_up': nrm((DEPTH, W_LORA, BRANCH), W_LORA ** -0.5),
        'a0': nrm((DEPTH, BRANCH), 0.5),
        'a_up': nrm((DEPTH, A_LORA, BRANCH), A_LORA ** -0.5),
        'k_k': 0.85 + nrm((DEPTH, BRANCH), 0.1),
        'k_a': 1.0 + nrm((DEPTH, BRANCH), 0.1),
        'r_k': nrm((DEPTH, H_RWKV, R_HEAD), 0.1),
        'lnx_g': gain((DEPTH, BRANCH)),
        'lnx_b': nrm((DEPTH, BRANCH), 0.02),
        'vres_0': nrm((DEPTH - 1, BRANCH), 0.5),
        'vres_a': nrm((DEPTH - 1, BRANCH, V_LORA), BRANCH ** -0.5),
        'vres_b': nrm((DEPTH - 1, V_LORA, BRANCH), V_LORA ** -0.5),
        'conv_w': nrm((DEPTH, M_CONV, CONV_CH), M_CONV ** -0.5),
        'conv_b': nrm((DEPTH, CONV_CH), 0.02),
        'dt_bias': dt_bias,
        'a_log': a_log,
        'd_skip': 1.0 + nrm((DEPTH, H_MAMBA), 0.1),
        'gnorm_g': gain((DEPTH, BRANCH)),
        'mem_norm_g': gain((DEPTH, D_MODEL)),
        'w_mk': nrm((DEPTH, D_MODEL, BRANCH), D_MODEL ** -0.5),
        'w_mv': nrm((DEPTH, D_MODEL, BRANCH), D_MODEL ** -0.5),
        'final_g': gain((D_MODEL,)),
    }


def reference(x_prompt, x_sample, cache_diff_k, cache_diff_v, cache_mem_k, cache_mem_v, state_rwkv_shift,
              state_rwkv, state_conv, state_ssm, page_table, mem_prompt, norm_g, w_in, w_out, lam_q1, lam_k1,
              lam_q2, lam_k2, subln_g, shift_mu, w0, w_up, a0, a_up, k_k, k_a, r_k, lnx_g, lnx_b, vres_0,
              vres_a, vres_b, conv_w, conv_b, dt_bias, a_log, d_skip, gnorm_g, mem_norm_g, w_mk, w_mv, final_g):
    f32 = jnp.float32
    bp = x_prompt.shape[0]
    db = x_sample.shape[0]
    n_pages = page_table.shape[1]
    past_len = n_pages * PAGE_SIZE
    xp, xs = x_prompt, x_sample
    vf_p, vf_s = None, None
    pk, pv, pmk, pmv, prs, prw, pcv, pss = [], [], [], [], [], [], [], []
    sk, sv, srs, srw, scv, sss = [], [], [], [], [], []
    for l in range(DEPTH):
        lp = {
            'norm_g': norm_g[l], 'w_in': w_in[l], 'w_out': w_out[l],
            'lam_q1': lam_q1[l].astype(f32), 'lam_k1': lam_k1[l].astype(f32),
            'lam_q2': lam_q2[l].astype(f32), 'lam_k2': lam_k2[l].astype(f32),
            'subln_g': subln_g[l], 'shift_mu': shift_mu[l].astype(f32),
            'w0': w0[l].astype(f32), 'w_up': w_up[l].astype(f32),
            'a0': a0[l].astype(f32), 'a_up': a_up[l].astype(f32),
            'k_k': k_k[l].astype(f32), 'k_a': k_a[l].astype(f32), 'r_k': r_k[l].astype(f32),
            'lnx_g': lnx_g[l].astype(f32), 'lnx_b': lnx_b[l].astype(f32),
            'conv_w': conv_w[l].astype(f32), 'conv_b': conv_b[l].astype(f32),
            'dt_bias': dt_bias[l].astype(f32), 'a_log': a_log[l].astype(f32),
            'd_skip': d_skip[l].astype(f32), 'gnorm_g': gnorm_g[l],
        }
        if l > 0:
            lp['vres_0'] = vres_0[l - 1].astype(f32)
            lp['vres_a'] = vres_a[l - 1].astype(f32)
            lp['vres_b'] = vres_b[l - 1].astype(f32)

        mk, mv = memory_kv(mem_prompt, mem_norm_g[l], w_mk[l], w_mv[l])
        xp, st, vf_p = hybrid_layer(
            xp, l, lp, None,
            jnp.zeros((bp, N_SHIFT), xp.dtype),
            jnp.zeros((bp, H_RWKV, R_HEAD, R_HEAD), f32),
            jnp.zeros((bp, M_CONV - 1, CONV_CH), xp.dtype),
            jnp.zeros((bp, H_MAMBA, M_HEADDIM, M_STATE), f32),
            mk, mv, vf_p)
        pk.append(st[0]); pv.append(st[1]); pmk.append(mk); pmv.append(mv)
        prs.append(st[2].astype(state_rwkv_shift.dtype)); prw.append(st[3].astype(state_rwkv.dtype))
        pcv.append(st[4].astype(state_conv.dtype)); pss.append(st[5].astype(state_ssm.dtype))

        k_past = cache_diff_k[l][page_table].reshape(db, past_len, H_DIFF, 2, DQK)
        v_past = cache_diff_v[l][page_table].reshape(db, past_len, H_DIFF, DV)
        xs, ss, vf_s = hybrid_layer(
            xs, l, lp, (k_past, v_past), state_rwkv_shift[l], state_rwkv[l], state_conv[l], state_ssm[l],
            cache_mem_k[l], cache_mem_v[l], vf_s)
        sk.append(ss[0]); sv.append(ss[1])
        srs.append(ss[2].astype(state_rwkv_shift.dtype)); srw.append(ss[3].astype(state_rwkv.dtype))
        scv.append(ss[4].astype(state_conv.dtype)); sss.append(ss[5].astype(state_ssm.dtype))

    y_prompt = rmsnorm(xp, final_g)
    y_sample = rmsnorm(xs, final_g)
    return (y_prompt, y_sample,
            jnp.stack(pk), jnp.stack(pv), jnp.stack(pmk), jnp.stack(pmv),
            jnp.stack(prs), jnp.stack(prw), jnp.stack(pcv), jnp.stack(pss),
            jnp.stack(sk), jnp.stack(sv), jnp.stack(srs), jnp.stack(srw), jnp.stack(scv), jnp.stack(sss))
```

```python
import functools
import math

import jax
import jax.numpy as jnp
from jax import lax
from jax.experimental import pallas as pl
from jax.experimental.pallas import tpu as pltpu

F32 = jnp.float32
BF16 = jnp.bfloat16
HIGHEST = lax.Precision.HIGHEST

D_MODEL = 2048
BRANCH = 512
H_DIFF = 4
DQK = 64
DV = 128
R_HEAD = 64
H_RWKV = 8
N_LORA = 128
N_SHIFT = 3 * BRANCH + N_LORA
DECAY_SCALE = 0.6065306597
LNX_EPS = 64e-5
M_HEADDIM = 64
H_MAMBA = 8
M_GROUPS = 2
M_STATE = 128
M_CONV = 4
SSD_CHUNK = 128
H_X = 4
D_X = 128
N_MEM = 256
RMS_EPS = 1e-5
PAGE_SIZE = 128

C_DQ, C_DK, C_DV, C_DG, C_RR, C_RK, C_RV, C_RG, C_MZ, C_XQ, C_XG, C_MX, C_MBC = range(13)
C_LORA128 = 13 * BRANCH // 128
C_DT128 = C_LORA128 + 1
N_PROJ = 13 * BRANCH + 2 * 128

VMEM_LIMIT = 56 * 1024 * 1024


def _cparams(sem):
    return pltpu.CompilerParams(dimension_semantics=sem, vmem_limit_bytes=VMEM_LIMIT)


def _silu(x):
    return x * jax.nn.sigmoid(x)


def _dot_t(a, b):
    return lax.dot_general(a, b, (((1,), (1,)), ((), ())), preferred_element_type=F32)


def _rms_matmul_kernel(x_ref, g_ref, w_ref, o_ref, h_ref):
    @pl.when(pl.program_id(1) == 0)
    def _():
        x = x_ref[...]
        ms = jnp.mean(x * x, axis=-1, keepdims=True)
        h_ref[...] = (x * lax.rsqrt(ms + RMS_EPS) * g_ref[...]).astype(BF16)

    o_ref[...] = jnp.dot(h_ref[...], w_ref[...], preferred_element_type=F32)


def _rms_matmul(x, g, w, tm, tn):
    m, k = x.shape
    n = w.shape[1]
    return pl.pallas_call(
        _rms_matmul_kernel,
        grid=(m // tm, n // tn),
        in_specs=[pl.BlockSpec((tm, k), lambda i, j: (i, 0)),
                  pl.BlockSpec((1, k), lambda i, j: (0, 0)),
                  pl.BlockSpec((k, tn), lambda i, j: (0, j))],
        out_specs=pl.BlockSpec((tm, tn), lambda i, j: (i, j)),
        out_shape=jax.ShapeDtypeStruct((m, n), F32),
        scratch_shapes=[pltpu.VMEM((tm, k), BF16)],
        compiler_params=_cparams(("parallel", "arbitrary")),
        name="rms_matmul",
    )(x, g.reshape(1, k), w)


def _outproj_kernel(a_ref, b_ref, m_ref, c_ref, w_ref, x_ref, fg_ref, o_ref, *, final):
    acc = x_ref[...]
    for g, r in enumerate((a_ref, b_ref, m_ref, c_ref)):
        acc = acc + jnp.dot(r[...], w_ref[g * BRANCH:(g + 1) * BRANCH, :], preferred_element_type=F32)
    if final:
        ms = jnp.mean(acc * acc, axis=-1, keepdims=True)
        acc = acc * lax.rsqrt(ms + RMS_EPS) * fg_ref[...]
    o_ref[...] = acc


def _outproj(parts, w, x, fg, tm, final):
    m = x.shape[0]
    part_spec = pl.BlockSpec((tm, BRANCH), lambda i: (i, 0))
    return pl.pallas_call(
        functools.partial(_outproj_kernel, final=final),
        grid=(m // tm,),
        in_specs=[part_spec] * 4 + [pl.BlockSpec((D_MODEL, D_MODEL), lambda i: (0, 0)),
                                    pl.BlockSpec((tm, D_MODEL), lambda i: (i, 0)),
                                    pl.BlockSpec((1, D_MODEL), lambda i: (0, 0))],
        out_specs=pl.BlockSpec((tm, D_MODEL), lambda i: (i, 0)),
        out_shape=jax.ShapeDtypeStruct((m, D_MODEL), F32),
        compiler_params=_cparams(("parallel",)),
        name="outproj",
    )(*parts, w, x, fg.reshape(1, D_MODEL))


def _alibi_slope(h):
    return 2.0 ** (-8.0 * (h + 1) / H_DIFF)


def _lambda(lq1, lk1, lq2, lk2, lam_init):
    s1 = jnp.sum(lq1[...] * lk1[...], axis=-1, keepdims=True)
    s2 = jnp.sum(lq2[...] * lk2[...], axis=-1, keepdims=True)
    return jnp.exp(s1) - jnp.exp(s2) + lam_init


def _subln_gate(o, subg, gate, lam_init):
    ms = jnp.mean(o * o, axis=-1, keepdims=True)
    o = (o * lax.rsqrt(ms + RMS_EPS) * subg) * (1.0 - lam_init)
    return o * _silu(gate)


def _diff_prompt_kernel(lq1, lk1, lq2, lk2, subg_ref, q_ref, k_ref, v_ref, g_ref, o_ref,
                        m_ref, l_ref, acc_ref, *, tq, lam_init):
    qi = pl.program_id(1)
    ki = pl.program_id(2)

    @pl.when(ki == 0)
    def _():
        m_ref[...] = jnp.full(m_ref.shape, -1e30, F32)
        l_ref[...] = jnp.zeros(l_ref.shape, F32)
        acc_ref[...] = jnp.zeros(acc_ref.shape, F32)

    def block(masked):
        dist = (lax.broadcasted_iota(jnp.int32, (tq, tq), 0)
                - lax.broadcasted_iota(jnp.int32, (tq, tq), 1)).astype(F32) + ((qi - ki) * tq).astype(F32)
        q = (q_ref[...] * (DQK ** -0.5)).astype(BF16)
        k = k_ref[...].astype(BF16)
        v = v_ref[...].astype(BF16)
        for h in range(H_DIFF):
            bias = (-_alibi_slope(h)) * dist
            if masked:
                bias = jnp.where(dist >= 0, bias, -jnp.inf)
            vh = v[:, h * DV:(h + 1) * DV]
            for c in range(2):
                lo = h * 2 * DQK + c * DQK
                j = 2 * h + c
                s = _dot_t(q[:, lo:lo + DQK], k[:, lo:lo + DQK]) + bias
                m_prev = m_ref[j]
                m_new = jnp.maximum(m_prev, jnp.max(s, axis=-1, keepdims=True))
                alpha = jnp.exp(m_prev - m_new)
                p = jnp.exp(s - m_new)
                l_ref[j] = alpha * l_ref[j] + jnp.sum(p, axis=-1, keepdims=True)
                acc_ref[j] = alpha * acc_ref[j] + jnp.dot(p.astype(BF16), vh, preferred_element_type=F32)
                m_ref[j] = m_new

    @pl.when(ki < qi)
    def _():
        block(False)

    @pl.when(ki == qi)
    def _():
        block(True)
        lam = _lambda(lq1, lk1, lq2, lk2, lam_init)
        for h in range(H_DIFF):
            o = acc_ref[2 * h] / l_ref[2 * h] - lam * (acc_ref[2 * h + 1] / l_ref[2 * h + 1])
            sl = slice(h * DV, (h + 1) * DV)
            o_ref[:, sl] = _subln_gate(o, subg_ref[...], g_ref[:, sl], lam_init).astype(o_ref.dtype)


def _diff_prompt(p, lam_params, subg, nb, t, lam_init):
    tq = min(512, t)
    nq = t // tq
    vec = pl.BlockSpec((1, DQK), lambda b, i, j: (0, 0))

    def qmap(c):
        return lambda b, i, j: (b * nq + i, c)

    def kmap(c):
        return lambda b, i, j: (b * nq + jnp.minimum(i, j), c)

    return pl.pallas_call(
        functools.partial(_diff_prompt_kernel, tq=tq, lam_init=lam_init),
        grid=(nb, nq, nq),
        in_specs=[vec, vec, vec, vec, pl.BlockSpec((1, DV), lambda b, i, j: (0, 0)),
                  pl.BlockSpec((tq, BRANCH), qmap(C_DQ)),
                  pl.BlockSpec((tq, BRANCH), kmap(C_DK)),
                  pl.BlockSpec((tq, BRANCH), kmap(C_DV)),
                  pl.BlockSpec((tq, BRANCH), qmap(C_DG))],
        out_specs=pl.BlockSpec((tq, BRANCH), lambda b, i, j: (b * nq + i, 0)),
        out_shape=jax.ShapeDtypeStruct((nb * t, BRANCH), BF16),
        scratch_shapes=[pltpu.VMEM((2 * H_DIFF, tq, 1), F32), pltpu.VMEM((2 * H_DIFF, tq, 1), F32),
                        pltpu.VMEM((2 * H_DIFF, tq, DV), F32)],
        compiler_params=_cparams(("parallel", "parallel", "arbitrary")),
        name="diff_attn_prompt",
    )(*lam_params, subg.reshape(1, DV), p, p, p, p)


def _diff_decode_kernel(pt_ref, lq1, lk1, lq2, lk2, subg_ref, q_ref, kc_ref, vc_ref, g_ref, kp_ref, vp_ref,
                        o_ref, qm_ref, m_ref, l_ref, acc_ref, *, n_pages, lam_init):
    j = pl.program_id(1)
    nrow = 2 * H_DIFF
    row = lax.broadcasted_iota(jnp.int32, (nrow, BRANCH), 0)
    lane = lax.broadcasted_iota(jnp.int32, (nrow, BRANCH), 1)

    @pl.when(j == 0)
    def _():
        q = q_ref[...] * (DQK ** -0.5)
        qm_ref[...] = jnp.where(lane // DQK == row, q, 0.0).astype(BF16)
        m_ref[...] = jnp.full(m_ref.shape, -1e30, F32)
        l_ref[...] = jnp.zeros(l_ref.shape, F32)
        acc_ref[...] = jnp.zeros(acc_ref.shape, F32)

    hrow = lax.broadcasted_iota(jnp.int32, (nrow, 1), 0) // 2
    slope = jnp.zeros((nrow, 1), F32)
    for h in range(H_DIFF):
        slope = jnp.where(hrow == h, _alibi_slope(h), slope)

    s = _dot_t(qm_ref[...], kp_ref[...].astype(BF16))
    kpos = j * PAGE_SIZE + lax.broadcasted_iota(jnp.int32, (nrow, PAGE_SIZE), 1)
    dist = (n_pages * PAGE_SIZE - kpos).astype(F32)
    s = s + (-slope) * dist
    m_prev = m_ref[...]
    m_new = jnp.maximum(m_prev, jnp.max(s, axis=-1, keepdims=True))
    alpha = jnp.exp(m_prev - m_new)
    p = jnp.exp(s - m_new)
    l_ref[...] = alpha * l_ref[...] + jnp.sum(p, axis=-1, keepdims=True)
    acc_ref[...] = alpha * acc_ref[...] + jnp.dot(p.astype(BF16), vp_ref[...].astype(BF16),
                                                  preferred_element_type=F32)
    m_ref[...] = m_new

    @pl.when(j == n_pages - 1)
    def _():
        kc = kc_ref[...].astype(BF16).astype(F32)
        vc = vc_ref[...].astype(BF16).astype(F32)
        sc = jnp.sum(qm_ref[...].astype(F32) * kc, axis=-1, keepdims=True)
        m_prev = m_ref[...]
        m_new = jnp.maximum(m_prev, sc)
        alpha = jnp.exp(m_prev - m_new)
        pc = jnp.exp(sc - m_new)
        l_fin = alpha * l_ref[...] + pc
        acc = alpha * acc_ref[...] + pc.astype(BF16).astype(F32) * vc
        lam = _lambda(lq1, lk1, lq2, lk2, lam_init)
        sign = jnp.where(lax.broadcasted_iota(jnp.int32, (nrow, 1), 0) % 2 == 0, 1.0, -lam)
        contrib = jnp.where(lane // DV == row // 2, (acc / l_fin) * sign, 0.0)
        o = jnp.sum(contrib, axis=0, keepdims=True)
        g = g_ref[...]
        outs = [_subln_gate(o[:, h * DV:(h + 1) * DV], subg_ref[...], g[:, h * DV:(h + 1) * DV], lam_init)
                for h in range(H_DIFF)]
        o_ref[...] = jnp.concatenate(outs, axis=-1).astype(o_ref.dtype)


def _diff_decode(p3, layer, cache_k, cache_v, page_table, lam_params, subg, lam_init):
    nb, n_pages = page_table.shape
    vec = pl.BlockSpec((1, DQK), lambda b, j, pt: (0, 0))

    def row(c):
        return pl.BlockSpec((None, 1, BRANCH), lambda b, j, pt: (b, 0, c))

    page = pl.BlockSpec((None, None, PAGE_SIZE, BRANCH), lambda b, j, pt: (layer, pt[b * n_pages + j], 0, 0))
    nrow = 2 * H_DIFF
    return pl.pallas_call(
        functools.partial(_diff_decode_kernel, n_pages=n_pages, lam_init=lam_init),
        grid_spec=pltpu.PrefetchScalarGridSpec(
            num_scalar_prefetch=1,
            grid=(nb, n_pages),
            in_specs=[vec, vec, vec, vec, pl.BlockSpec((1, DV), lambda b, j, pt: (0, 0)),
                      row(C_DQ), row(C_DK), row(C_DV), row(C_DG), page, page],
            out_specs=pl.BlockSpec((None, 1, BRANCH), lambda b, j, pt: (b, 0, 0)),
            scratch_shapes=[pltpu.VMEM((nrow, BRANCH), BF16), pltpu.VMEM((nrow, 1), F32),
                            pltpu.VMEM((nrow, 1), F32), pltpu.VMEM((nrow, BRANCH), F32)]),
        out_shape=jax.ShapeDtypeStruct((nb, 1, BRANCH), BF16),
        compiler_params=_cparams(("parallel", "arbitrary")),
        name="diff_attn_decode",
    )(page_table.reshape(-1), *lam_params, subg.reshape(1, DV), p3, p3, p3, p3, cache_k, cache_v)


def _seg_sum(x, bd_ref):
    return jnp.dot(x, bd_ref[...], preferred_element_type=F32, precision=HIGHEST)


def _rwkv_pre_kernel(*refs, shift, has_vres, tm):
    it = iter(refs)
    cur = [next(it) for _ in range(4)]
    prev = None if shift else [next(it) for _ in range(4)]
    mu = [next(it) for _ in range(4)]
    w0, wup, a0, aup, kk_ref, ka_ref, rk_ref, bd_ref = [next(it) for _ in range(8)]
    if has_vres:
        vf_ref, vr0, vra, vrb = [next(it) for _ in range(4)]
    outs = [next(it) for _ in range(7)]
    carry = [next(it) for _ in range(4)] if shift else None

    xs = []
    for n in range(4):
        c = cur[n][...]
        if shift:
            @pl.when(pl.program_id(1) == 0)
            def _():
                carry[n][...] = jnp.zeros(carry[n].shape, F32)
            rolled = pltpu.roll(c, 1, 0)
            rows = lax.broadcasted_iota(jnp.int32, c.shape, 0)
            pv = jnp.where(rows == 0, carry[n][...], rolled)
            carry[n][...] = c[tm - 1:tm, :]
        else:
            pv = prev[n][...]
        xs.append(c + (pv - c) * mu[n][...])
    xr, xk, xv, xl = xs

    hw = jnp.dot(jnp.tanh(xl).astype(BF16), wup[...], preferred_element_type=F32)
    w = jnp.exp(-DECAY_SCALE * jax.nn.sigmoid(w0[...] + hw))
    a = jax.nn.sigmoid(a0[...] + jnp.dot(xl.astype(BF16), aup[...], preferred_element_type=F32))
    if has_vres:
        t1 = jnp.dot(xv.astype(BF16), vra[...], preferred_element_type=F32)
        t2 = jnp.dot(t1.astype(BF16), vrb[...], preferred_element_type=F32)
        xv = xv + (vf_ref[...] - xv) * jax.nn.sigmoid(vr0[...] + t2)
    kk = xk * kk_ref[...]
    kk = kk / jnp.maximum(jnp.sqrt(_seg_sum(kk * kk, bd_ref)), 1e-12)
    k2 = xk * (1.0 + (a - 1.0) * ka_ref[...])
    bonus = _seg_sum(xr * k2 * rk_ref[...], bd_ref) * xv
    for o, val in zip(outs, (xr, w, k2, xv, -kk, kk * a, bonus)):
        o[...] = val


def _rwkv_pre(p, prev, lw, vfirst, nb, t, tm):
    shift = prev is None
    has_vres = vfirst is not None
    nt = t // tm
    if shift:
        grid = (nb, nt)
        rmap = lambda c: (lambda b, i: (b * nt + i, c))
        cmap = lambda b, i: (0, 0)
        sem = ("parallel", "arbitrary")
    else:
        grid = (nt,)
        rmap = lambda c: (lambda i: (i, c))
        cmap = lambda i: (0, 0)
        sem = ("parallel",)
    wide = lambda c: pl.BlockSpec((tm, BRANCH), rmap(c))
    vec = lambda n: pl.BlockSpec((1, n), cmap)
    args, specs = [], []
    args += [p, p, p, p]
    specs += [wide(C_RR), wide(C_RK), wide(C_RV), pl.BlockSpec((tm, N_LORA), rmap(C_LORA128))]
    if not shift:
        args += [prev, prev, prev, prev]
        specs += [wide(0), wide(1), wide(2), pl.BlockSpec((tm, N_LORA), rmap(3 * BRANCH // N_LORA))]
    args += lw["mu"]
    specs += [vec(BRANCH)] * 3 + [vec(N_LORA)]
    args += [lw["w0"], lw["w_up"], lw["a0"], lw["a_up"], lw["k_k"], lw["k_a"], lw["r_k"], lw["bd"]]
    specs += [vec(BRANCH), pl.BlockSpec((N_LORA, BRANCH), cmap), vec(BRANCH), pl.BlockSpec((N_LORA, BRANCH), cmap),
              vec(BRANCH), vec(BRANCH), vec(BRANCH), pl.BlockSpec((BRANCH, BRANCH), cmap)]
    if has_vres:
        args += [vfirst, lw["vres_0"], lw["vres_a"], lw["vres_b"]]
        specs += [pl.BlockSpec((tm, BRANCH), rmap(0)), vec(BRANCH),
                  pl.BlockSpec(lw["vres_a"].shape, cmap), pl.BlockSpec(lw["vres_b"].shape, cmap)]
    out_spec = pl.BlockSpec((tm, BRANCH), rmap(0))
    scratch = [pltpu.VMEM((1, BRANCH), F32)] * 3 + [pltpu.VMEM((1, N_LORA), F32)] if shift else []
    return pl.pallas_call(
        functools.partial(_rwkv_pre_kernel, shift=shift, has_vres=has_vres, tm=tm),
        grid=grid, in_specs=specs, out_specs=[out_spec] * 7,
        out_shape=[jax.ShapeDtypeStruct((nb * t, BRANCH), F32)] * 7,
        scratch_shapes=scratch,
        compiler_params=_cparams(sem),
        name="rwkv_pre",
    )(*args)


RWKV_SUB = 8


def _rwkv_scan_kernel(r_ref, w_ref, k_ref, v_ref, a_ref, b_ref, y_ref, s_ref, *, tc):
    @pl.when(pl.program_id(1) == 0)
    def _():
        s_ref[...] = jnp.zeros(s_ref.shape, F32)

    eye = (lax.broadcasted_iota(jnp.int32, (R_HEAD, R_HEAD), 0)
           == lax.broadcasted_iota(jnp.int32, (R_HEAD, R_HEAD), 1)).astype(F32)

    def sub_block(i, carry):
        base = pl.multiple_of(i * RWKV_SUB, RWKV_SUB)
        for hp in range(H_RWKV // 2):
            pair = slice(hp * 2 * R_HEAD, (hp + 1) * 2 * R_HEAD)
            tiles = [ref[pl.ds(base, RWKV_SUB), pair] for ref in (r_ref, w_ref, k_ref, v_ref, a_ref, b_ref)]
            states = [s_ref[2 * hp + e] for e in range(2)]
            yrows = []
            for tt in range(RWKV_SUB):
                ys = []
                for e in range(2):
                    r, w, k, v, a, b = [z[tt:tt + 1, e * R_HEAD:(e + 1) * R_HEAD] for z in tiles]
                    st = states[e]
                    sa = jnp.sum(st * a, axis=-1, keepdims=True)
                    vcol = jnp.sum(eye * v, axis=-1, keepdims=True)
                    st = st * w + sa * b + vcol * k
                    states[e] = st
                    ycol = jnp.sum(st * r, axis=-1, keepdims=True)
                    ys.append(jnp.sum(eye * ycol, axis=0, keepdims=True))
                yrows.append(jnp.concatenate(ys, axis=-1))
            for e in range(2):
                s_ref[2 * hp + e] = states[e]
            y_ref[pl.ds(base, RWKV_SUB), pair] = jnp.concatenate(yrows, axis=0)
        return carry

    lax.fori_loop(0, tc // RWKV_SUB, sub_block, 0)


def _rwkv_scan(seqs, nb, t):
    tc = min(256, t)
    blk = pl.BlockSpec((None, tc, BRANCH), lambda b, c: (b, c, 0))
    return pl.pallas_call(
        functools.partial(_rwkv_scan_kernel, tc=tc),
        grid=(nb, t // tc),
        in_specs=[blk] * 6,
        out_specs=[blk, pl.BlockSpec((H_RWKV, R_HEAD, R_HEAD), lambda b, c: (b, 0, 0))],
        out_shape=[jax.ShapeDtypeStruct((nb, t, BRANCH), F32),
                   jax.ShapeDtypeStruct((nb * H_RWKV, R_HEAD, R_HEAD), F32)],
        compiler_params=_cparams(("parallel", "arbitrary")),
        name="rwkv_scan",
    )(*seqs)


def _rwkv_step_kernel(r_ref, w_ref, k_ref, a_ref, b_ref, vt_ref, s_ref, yt_ref, so_ref, *, bb):
    for j in range(bb):
        for h in range(H_RWKV):
            sl = slice(h * R_HEAD, (h + 1) * R_HEAD)
            row = lambda ref: ref[j:j + 1, sl]
            st = s_ref[j, sl, :]
            sa = jnp.sum(st * row(a_ref), axis=-1, keepdims=True)
            st = st * row(w_ref) + sa * row(b_ref) + vt_ref[sl, j:j + 1] * row(k_ref)
            so_ref[j, sl, :] = st
            yt_ref[sl, j:j + 1] = jnp.sum(st * row(r_ref), axis=-1, keepdims=True)


def _rwkv_step(r, w, k, na, b, vt, state, bb=8):
    n = r.shape[0]
    rowb = pl.BlockSpec((bb, BRANCH), lambda i: (i, 0))
    colb = pl.BlockSpec((None, BRANCH, bb), lambda i: (i, 0, 0))
    stb = pl.BlockSpec((bb, BRANCH, R_HEAD), lambda i: (i, 0, 0))
    return pl.pallas_call(
        functools.partial(_rwkv_step_kernel, bb=bb),
        grid=(n // bb,),
        in_specs=[rowb] * 5 + [colb, stb],
        out_specs=[colb, stb],
        out_shape=[jax.ShapeDtypeStruct((n // bb, BRANCH, bb), F32),
                   jax.ShapeDtypeStruct((n, BRANCH, R_HEAD), F32)],
        compiler_params=_cparams(("parallel",)),
        name="rwkv_step",
    )(r, w, k, na, b, vt, state)


def _rwkv_post_kernel(y_ref, bonus_ref, g_ref, lg_ref, lb_ref, bd_ref, o_ref):
    y = y_ref[...]
    mu = _seg_sum(y, bd_ref) * (1.0 / R_HEAD)
    d = y - mu
    var = _seg_sum(d * d, bd_ref) * (1.0 / R_HEAD)
    o = d * lax.rsqrt(var + LNX_EPS) * lg_ref[...] + lb_ref[...] + bonus_ref[...]
    o_ref[...] = (o * _silu(g_ref[...])).astype(o_ref.dtype)


def _rwkv_post(y, bonus, p, lw, tm):
    m = y.shape[0]
    blk = pl.BlockSpec((tm, BRANCH), lambda i: (i, 0))
    vec = pl.BlockSpec((1, BRANCH), lambda i: (0, 0))
    return pl.pallas_call(
        _rwkv_post_kernel,
        grid=(m // tm,),
        in_specs=[blk, blk, pl.BlockSpec((tm, BRANCH), lambda i: (i, C_RG)), vec, vec,
                  pl.BlockSpec((BRANCH, BRANCH), lambda i: (0, 0))],
        out_specs=blk,
        out_shape=jax.ShapeDtypeStruct((m, BRANCH), BF16),
        compiler_params=_cparams(("parallel",)),
        name="rwkv_post",
    )(y, bonus, p, lw["lnx_g"], lw["lnx_b"], lw["bd"])


def _softplus(x):
    return jnp.maximum(x, 0.0) + jnp.log1p(jnp.exp(-jnp.abs(x)))


def _neg_exp_alog(alog_ref):
    lanes = lax.broadcasted_iota(jnp.int32, alog_ref.shape, 1)
    return jnp.where(lanes < H_MAMBA, -jnp.exp(alog_ref[...]), 0.0)


def _gated_norm(ym, z, g):
    u = ym * _silu(z)
    ms = jnp.mean(u * u, axis=-1, keepdims=True)
    return u * lax.rsqrt(ms + RMS_EPS) * g


def _mamba_prompt_kernel(mx_ref, mbc_ref, mdt_ref, mz_ref, cwx_ref, cwb_ref, cbx_ref, cbb_ref, dtb_ref, alog_ref,
                         dskip_ref, gn_ref, o_ref, s_ref, extx_ref, extb_ref, ym_ref):
    L = SSD_CHUNK
    pad = 8

    @pl.when(pl.program_id(1) == 0)
    def _():
        extx_ref[0:pad, :] = jnp.zeros((pad, BRANCH), F32)
        extb_ref[0:pad, :] = jnp.zeros((pad, BRANCH), F32)
        s_ref[...] = jnp.zeros(s_ref.shape, F32)

    def conv(ext_ref, cur_ref, cw_ref, cb_ref):
        ext_ref[pad:pad + L, :] = cur_ref[...]
        acc = cb_ref[...]
        for j in range(M_CONV):
            lo = pad - (M_CONV - 1) + j
            acc = acc + ext_ref[lo:lo + L, :] * cw_ref[j:j + 1, :]
        ext_ref[0:pad, :] = ext_ref[L:L + pad, :]
        return _silu(acc)

    x = conv(extx_ref, mx_ref, cwx_ref, cbx_ref)
    bc = conv(extb_ref, mbc_ref, cwb_ref, cbb_ref)
    dt = _softplus(mdt_ref[...] + dtb_ref[...])
    adt = dt * _neg_exp_alog(alog_ref)
    rowi = lax.broadcasted_iota(jnp.int32, (L, L), 0)
    coli = lax.broadcasted_iota(jnp.int32, (L, L), 1)
    lower = rowi >= coli
    acs = jnp.dot(lower.astype(F32), adt, preferred_element_type=F32, precision=HIGHEST)
    acs_t = acs.T
    last = acs[L - 1:L, :]
    dstate = jnp.exp(last - acs)
    eacs = jnp.exp(acs)
    elast = jnp.exp(last)
    bcb = bc.astype(BF16)
    grp = []
    for g in range(M_GROUPS):
        bg = bcb[:, g * M_STATE:(g + 1) * M_STATE]
        cg = bcb[:, (M_GROUPS + g) * M_STATE:(M_GROUPS + g + 1) * M_STATE]
        grp.append((bg, cg, _dot_t(cg, bg)))
    for h in range(H_MAMBA):
        bg, cg, cb = grp[h // (H_MAMBA // M_GROUPS)]
        sl = slice(h * M_HEADDIM, (h + 1) * M_HEADDIM)
        decay = jnp.where(lower, jnp.exp(acs[:, h:h + 1] - acs_t[h:h + 1, :]), 0.0)
        xh = x[:, sl]
        xdt = xh * dt[:, h:h + 1]
        yd = jnp.dot((cb * decay).astype(BF16), xdt.astype(BF16), preferred_element_type=F32)
        st = s_ref[h]
        yo = _dot_t(cg, st.astype(BF16)) * eacs[:, h:h + 1]
        upd = lax.dot_general((xdt * dstate[:, h:h + 1]).astype(BF16), bg, (((0,), (0,)), ((), ())),
                              preferred_element_type=F32)
        s_ref[h] = st * elast[:, h:h + 1] + upd
        ym_ref[:, sl] = yd + yo + dskip_ref[:, sl] * xh
    o_ref[...] = _gated_norm(ym_ref[...], mz_ref[...], gn_ref[...]).astype(o_ref.dtype)


def _mamba_prompt(p, lw, nb, t):
    L = SSD_CHUNK
    nt = t // L
    rmap = lambda c: (lambda b, i: (b * nt + i, c))
    cmap = lambda b, i: (0, 0)
    wide = lambda c: pl.BlockSpec((L, BRANCH), rmap(c))
    vec = lambda n: pl.BlockSpec((1, n), cmap)
    cw = pl.BlockSpec((M_CONV, BRANCH), cmap)
    return pl.pallas_call(
        _mamba_prompt_kernel,
        grid=(nb, nt),
        in_specs=[wide(C_MX), wide(C_MBC), pl.BlockSpec((L, 128), rmap(C_DT128)), wide(C_MZ),
                  cw, cw, vec(BRANCH), vec(BRANCH), vec(128), vec(128), vec(BRANCH), vec(BRANCH)],
        out_specs=[pl.BlockSpec((L, BRANCH), rmap(0)),
                   pl.BlockSpec((None, H_MAMBA, M_HEADDIM, M_STATE), lambda b, i: (b, 0, 0, 0))],
        out_shape=[jax.ShapeDtypeStruct((nb * t, BRANCH), BF16),
                   jax.ShapeDtypeStruct((nb, H_MAMBA, M_HEADDIM, M_STATE), F32)],
        scratch_shapes=[pltpu.VMEM((L + 8, BRANCH), F32), pltpu.VMEM((L + 8, BRANCH), F32),
                        pltpu.VMEM((L, BRANCH), F32)],
        compiler_params=_cparams(("parallel", "arbitrary")),
        name="mamba_prompt",
    )(p, p, p, p, lw["cw_x"], lw["cw_bc"], lw["cb_x"], lw["cb_bc"], lw["dt_bias"], lw["a_log"],
      lw["d_skip"], lw["gnorm_g"])


def _mamba_dec_pre_kernel(mx_ref, mbc_ref, mdt_ref, sx_ref, sb_ref, cwx_ref, cwb_ref, cbx_ref, cbb_ref,
                          dtb_ref, alog_ref, x_ref, bc_ref, xdt_ref, da_ref):
    def conv(st_ref, cur_ref, cw_ref, cb_ref):
        acc = cb_ref[...]
        for j in range(M_CONV - 1):
            acc = acc + st_ref[j] * cw_ref[j:j + 1, :]
        acc = acc + cur_ref[...] * cw_ref[M_CONV - 1:M_CONV, :]
        return _silu(acc)

    x = conv(sx_ref, mx_ref, cwx_ref, cbx_ref)
    x_ref[...] = x
    bc_ref[...] = conv(sb_ref, mbc_ref, cwb_ref, cbb_ref)
    dt = _softplus(mdt_ref[...] + dtb_ref[...])
    da = jnp.exp(dt * _neg_exp_alog(alog_ref))
    n = x.shape[0]
    for h in range(H_MAMBA):
        sl = slice(h * M_HEADDIM, (h + 1) * M_HEADDIM)
        xdt_ref[:, sl] = x[:, sl] * dt[:, h:h + 1]
        da_ref[:, sl] = jnp.broadcast_to(da[:, h:h + 1], (n, M_HEADDIM))


def _mamba_dec_pre(p, conv_x, conv_bc, lw):
    n = p.shape[0]
    cmap = lambda i: (0, 0)
    wide = lambda c: pl.BlockSpec((n, BRANCH), lambda i: (0, c))
    vec = lambda m: pl.BlockSpec((1, m), cmap)
    cw = pl.BlockSpec((M_CONV, BRANCH), cmap)
    st = pl.BlockSpec((M_CONV - 1, n, BRANCH), lambda i: (0, 0, 0))
    out = pl.BlockSpec((n, BRANCH), cmap)
    return pl.pallas_call(
        _mamba_dec_pre_kernel,
        grid=(1,),
        in_specs=[wide(C_MX), wide(C_MBC), pl.BlockSpec((n, 128), lambda i: (0, C_DT128)), st, st,
                  cw, cw, vec(BRANCH), vec(BRANCH), vec(128), vec(128)],
        out_specs=[out] * 4,
        out_shape=[jax.ShapeDtypeStruct((n, BRANCH), F32)] * 4,
        compiler_params=_cparams(("arbitrary",)),
        name="mamba_dec_pre",
    )(p, p, p, conv_x, conv_bc, lw["cw_x"], lw["cw_bc"], lw["cb_x"], lw["cb_bc"], lw["dt_bias"], lw["a_log"])


def _mamba_step_kernel(bc_ref, xdtt_ref, dat_ref, s_ref, yt_ref, so_ref, *, bb):
    for j in range(bb):
        for h in range(H_MAMBA):
            g = h // (H_MAMBA // M_GROUPS)
            sl = slice(h * M_HEADDIM, (h + 1) * M_HEADDIM)
            bg = bc_ref[j:j + 1, g * M_STATE:(g + 1) * M_STATE]
            cg = bc_ref[j:j + 1, (M_GROUPS + g) * M_STATE:(M_GROUPS + g + 1) * M_STATE]
            st = s_ref[j, sl, :] * dat_ref[sl, j:j + 1] + xdtt_ref[sl, j:j + 1] * bg
            so_ref[j, sl, :] = st
            yt_ref[sl, j:j + 1] = jnp.sum(st * cg, axis=-1, keepdims=True)


def _mamba_step(bc, xdtt, dat, state, bb=8):
    n = bc.shape[0]
    rowb = pl.BlockSpec((bb, BRANCH), lambda i: (i, 0))
    colb = pl.BlockSpec((None, BRANCH, bb), lambda i: (i, 0, 0))
    stb = pl.BlockSpec((bb, BRANCH, M_STATE), lambda i: (i, 0, 0))
    return pl.pallas_call(
        functools.partial(_mamba_step_kernel, bb=bb),
        grid=(n // bb,),
        in_specs=[rowb, colb, colb, stb],
        out_specs=[colb, stb],
        out_shape=[jax.ShapeDtypeStruct((n // bb, BRANCH, bb), F32),
                   jax.ShapeDtypeStruct((n, BRANCH, M_STATE), F32)],
        compiler_params=_cparams(("parallel",)),
        name="mamba_step",
    )(bc, xdtt, dat, state)


def _mamba_dec_post_kernel(y_ref, x_ref, z_ref, dskip_ref, gn_ref, o_ref):
    ym = y_ref[...] + dskip_ref[...] * x_ref[...]
    o_ref[...] = _gated_norm(ym, z_ref[...], gn_ref[...]).astype(o_ref.dtype)


def _mamba_dec_post(y, x, p, lw):
    n = y.shape[0]
    blk = pl.BlockSpec((n, BRANCH), lambda i: (0, 0))
    vec = pl.BlockSpec((1, BRANCH), lambda i: (0, 0))
    return pl.pallas_call(
        _mamba_dec_post_kernel,
        grid=(1,),
        in_specs=[blk, blk, pl.BlockSpec((n, BRANCH), lambda i: (0, C_MZ)), vec, vec],
        out_specs=blk,
        out_shape=jax.ShapeDtypeStruct((n, BRANCH), BF16),
        compiler_params=_cparams(("arbitrary",)),
        name="mamba_dec_post",
    )(y, x, p, lw["d_skip"], lw["gnorm_g"])


def _xattn_prompt_kernel(q_ref, g_ref, mk_ref, mv_ref, o_ref):
    q = q_ref[...].astype(BF16)
    mk = mk_ref[...].astype(BF16)
    mv = mv_ref[...].astype(BF16)
    for h in range(H_X):
        sl = slice(h * D_X, (h + 1) * D_X)
        s = _dot_t(q[:, sl], mk[:, sl]) * (D_X ** -0.5)
        e = jnp.exp(s - jnp.max(s, axis=-1, keepdims=True))
        o = jnp.dot(e.astype(BF16), mv[:, sl], preferred_element_type=F32) / jnp.sum(e, axis=-1, keepdims=True)
        o_ref[:, sl] = (o * _silu(g_ref[:, sl])).astype(o_ref.dtype)


def _xattn_prompt(p, mkv, nb, t):
    tq = min(512, t)
    nq = t // tq
    return pl.pallas_call(
        _xattn_prompt_kernel,
        grid=(nb, nq),
        in_specs=[pl.BlockSpec((tq, BRANCH), lambda b, i: (b * nq + i, C_XQ)),
                  pl.BlockSpec((tq, BRANCH), lambda b, i: (b * nq + i, C_XG)),
                  pl.BlockSpec((N_MEM, BRANCH), lambda b, i: (b, 0)),
                  pl.BlockSpec((N_MEM, BRANCH), lambda b, i: (b, 1))],
        out_specs=pl.BlockSpec((tq, BRANCH), lambda b, i: (b * nq + i, 0)),
        out_shape=jax.ShapeDtypeStruct((nb * t, BRANCH), BF16),
        compiler_params=_cparams(("parallel", "parallel")),
        name="xattn_prompt",
    )(p, p, mkv, mkv)


def _xattn_decode_kernel(q_ref, g_ref, mk_ref, mv_ref, o_ref):
    nrow = 8
    row = lax.broadcasted_iota(jnp.int32, (nrow, BRANCH), 0)
    lane = lax.broadcasted_iota(jnp.int32, (nrow, BRANCH), 1)
    sel = lane // D_X == row
    qm = jnp.where(sel, q_ref[...], 0.0).astype(BF16)
    s = _dot_t(qm, mk_ref[...].astype(BF16)) * (D_X ** -0.5)
    e = jnp.exp(s - jnp.max(s, axis=-1, keepdims=True))
    r = jnp.dot(e.astype(BF16), mv_ref[...].astype(BF16), preferred_element_type=F32)
    r = r / jnp.sum(e, axis=-1, keepdims=True)
    o = jnp.sum(jnp.where(sel, r, 0.0), axis=0, keepdims=True)
    o_ref[...] = (o * _silu(g_ref[...])).astype(o_ref.dtype)


def _xattn_decode(p3, layer, mem_k, mem_v):
    n = p3.shape[0]
    mem = pl.BlockSpec((None, None, N_MEM, BRANCH), lambda b: (layer, b, 0, 0))
    return pl.pallas_call(
        _xattn_decode_kernel,
        grid=(n,),
        in_specs=[pl.BlockSpec((None, 1, BRANCH), lambda b: (b, 0, C_XQ)),
                  pl.BlockSpec((None, 1, BRANCH), lambda b: (b, 0, C_XG)), mem, mem],
        out_specs=pl.BlockSpec((None, 1, BRANCH), lambda b: (b, 0, 0)),
        out_shape=jax.ShapeDtypeStruct((n, 1, BRANCH), BF16),
        compiler_params=_cparams(("parallel",)),
        name="xattn_decode",
    )(p3, p3, mem_k, mem_v)


def _layer_weights(l, prm):
    w = prm["w_in"][l]
    o_rs, o_rg, o_mz, o_mxbc, o_dt, o_xq, o_xg = 2048, 3712, 4224, 4736, 5760, 5768, 6280
    w_perm = jnp.concatenate([
        w[:, :o_rs + 3 * BRANCH], w[:, o_rg:o_mz], w[:, o_mz:o_mxbc], w[:, o_xq:o_xg], w[:, o_xg:],
        w[:, o_mxbc:o_dt], w[:, o_rs + 3 * BRANCH:o_rg], w[:, o_dt:o_xq],
        jnp.zeros((D_MODEL, 128 - H_MAMBA), w.dtype)], axis=1).astype(BF16)
    row = lambda v: v.reshape(1, -1).astype(F32)
    mu = prm["shift_mu"][l].astype(F32)
    seg = jnp.arange(BRANCH) // R_HEAD
    zl = jnp.zeros((N_LORA // 2, BRANCH), F32)
    cw = prm["conv_w"][l].astype(F32)
    cb = prm["conv_b"][l].astype(F32)
    pad8 = lambda v: jnp.pad(v.astype(F32), (0, 128 - H_MAMBA)).reshape(1, 128)
    lw = {
        "w_in": w_perm, "norm_g": prm["norm_g"][l], "w_out": prm["w_out"][l].astype(BF16),
        "lam": [row(prm[n][l]) for n in ("lam_q1", "lam_k1", "lam_q2", "lam_k2")],
        "subln_g": prm["subln_g"][l].astype(F32),
        "mu": [row(mu[:BRANCH]), row(mu[BRANCH:2 * BRANCH]), row(mu[2 * BRANCH:3 * BRANCH]), row(mu[3 * BRANCH:])],
        "w0": row(prm["w0"][l]), "a0": row(prm["a0"][l]),
        "w_up": jnp.concatenate([prm["w_up"][l].astype(F32), zl], axis=0).astype(BF16),
        "a_up": jnp.concatenate([zl, prm["a_up"][l].astype(F32)], axis=0).astype(BF16),
        "k_k": row(prm["k_k"][l]), "k_a": row(prm["k_a"][l]), "r_k": row(prm["r_k"][l]),
        "lnx_g": row(prm["lnx_g"][l]), "lnx_b": row(prm["lnx_b"][l]),
        "bd": (seg[:, None] == seg[None, :]).astype(F32),
        "cw_x": cw[:, :BRANCH], "cw_bc": cw[:, BRANCH:], "cb_x": row(cb[:BRANCH]), "cb_bc": row(cb[BRANCH:]),
        "dt_bias": pad8(prm["dt_bias"][l]), "a_log": pad8(prm["a_log"][l]),
        "d_skip": row(jnp.repeat(prm["d_skip"][l].astype(F32), M_HEADDIM)),
        "gnorm_g": row(prm["gnorm_g"][l]),
        "w_mkv": jnp.concatenate([prm["w_mk"][l], prm["w_mv"][l]], axis=1).astype(BF16),
        "mem_norm_g": prm["mem_norm_g"][l],
    }
    if l > 0:
        lw["vres_0"] = row(prm["vres_0"][l - 1])
        lw["vres_a"] = prm["vres_a"][l - 1].astype(BF16)
        lw["vres_b"] = prm["vres_b"][l - 1].astype(BF16)
    return lw


def _lam_init(l):
    return 0.8 - 0.6 * math.exp(-0.3 * l)


def _prompt_layer(x, l, lw, mem, vfirst, nb, t, fg, final, tm_proj, tm_row):
    p = _rms_matmul(x, lw["norm_g"], lw["w_in"], tm_proj, 1152)
    mkv = _rms_matmul(mem, lw["mem_norm_g"], lw["w_mkv"], mem.shape[0], 2 * BRANCH)
    out_a = _diff_prompt(p, lw["lam"], lw["subln_g"], nb, t, _lam_init(l))
    r, w, k, v, na, b, bonus = _rwkv_pre(p, None, lw, vfirst, nb, t, tm_row)
    seqs = [z.reshape(nb, t, BRANCH) for z in (r, w, k, v, na, b)]
    y, rw_state = _rwkv_scan(seqs, nb, t)
    out_b = _rwkv_post(y.reshape(nb * t, BRANCH), bonus, p, lw, tm_row)
    out_m, ssm_state = _mamba_prompt(p, lw, nb, t)
    out_x = _xattn_prompt(p, mkv, nb, t)
    x = _outproj((out_a, out_b, out_m, out_x), lw["w_out"], x, fg, tm_row, final)
    return x, p, mkv, rw_state, ssm_state, (v if vfirst is None else vfirst)


def _decode_layer(x, l, lw, caches, vfirst, fg, final):
    cache_k, cache_v, mem_k, mem_v, page_table, rw_prev, rw_state, conv_state, ssm_state = caches
    n = x.shape[0]
    bb = 8
    p = _rms_matmul(x, lw["norm_g"], lw["w_in"], n, 1152)
    p3 = p.reshape(n, 1, N_PROJ)
    out_a = _diff_decode(p3, l, cache_k, cache_v, page_table, lw["lam"], lw["subln_g"], _lam_init(l)).reshape(n, BRANCH)

    r, w, k, v, na, b, bonus = _rwkv_pre(p, rw_prev, lw, vfirst, 1, n, n)
    to_cols = lambda z: z.reshape(n // bb, bb, BRANCH).transpose(0, 2, 1)
    from_cols = lambda z: z.transpose(0, 2, 1).reshape(n, BRANCH)
    yt, rw_new = _rwkv_step(r, w, k, na, b, to_cols(v), rw_state.reshape(n, BRANCH, R_HEAD), bb)
    out_b = _rwkv_post(from_cols(yt), bonus, p, lw, n)

    cs = conv_state.transpose(1, 0, 2)
    xm, bc, xdt, da = _mamba_dec_pre(p, cs[:, :, :BRANCH], cs[:, :, BRANCH:], lw)
    ymt, ssm_new = _mamba_step(bc, to_cols(xdt), to_cols(da), ssm_state.reshape(n, BRANCH, M_STATE), bb)
    out_m = _mamba_dec_post(from_cols(ymt), xm, p, lw)

    out_x = _xattn_decode(p3, l, mem_k, mem_v).reshape(n, BRANCH)
    x = _outproj((out_a, out_b, out_m, out_x), lw["w_out"], x, fg, n, final)
    return x, p, rw_new, ssm_new, (v if vfirst is None else vfirst)


def _cols(p, c, width=BRANCH):
    return p[..., c * width:(c + 1) * width]


def kernel(x_prompt, x_sample, cache_diff_k, cache_diff_v, cache_mem_k, cache_mem_v, state_rwkv_shift, state_rwkv, state_conv, state_ssm, page_table, mem_prompt, norm_g, w_in, w_out, lam_q1, lam_k1, lam_q2, lam_k2, subln_g, shift_mu, w0, w_up, a0, a_up, k_k, k_a, r_k, lnx_g, lnx_b, vres_0, vres_a, vres_b, conv_w, conv_b, dt_bias, a_log, d_skip, gnorm_g, mem_norm_g, w_mk, w_mv, final_g):
    prm = dict(norm_g=norm_g, w_in=w_in, w_out=w_out, lam_q1=lam_q1, lam_k1=lam_k1, lam_q2=lam_q2, lam_k2=lam_k2,
               subln_g=subln_g, shift_mu=shift_mu, w0=w0, w_up=w_up, a0=a0, a_up=a_up, k_k=k_k, k_a=k_a, r_k=r_k,
               lnx_g=lnx_g, lnx_b=lnx_b, vres_0=vres_0, vres_a=vres_a, vres_b=vres_b, conv_w=conv_w, conv_b=conv_b,
               dt_bias=dt_bias, a_log=a_log, d_skip=d_skip, gnorm_g=gnorm_g, mem_norm_g=mem_norm_g,
               w_mk=w_mk, w_mv=w_mv)
    depth = w_in.shape[0]
    nb, t, _ = x_prompt.shape
    n = x_sample.shape[0]
    n_pool = cache_diff_k.shape[1]
    ck = cache_diff_k.reshape(depth, n_pool, PAGE_SIZE, BRANCH)
    cv = cache_diff_v.reshape(depth, n_pool, PAGE_SIZE, BRANCH)
    cmk = cache_mem_k.reshape(depth, n, N_MEM, BRANCH)
    cmv = cache_mem_v.reshape(depth, n, N_MEM, BRANCH)
    mem = mem_prompt.reshape(nb * N_MEM, D_MODEL)

    xp = x_prompt.reshape(nb * t, D_MODEL)
    xs = x_sample.reshape(n, D_MODEL)
    vf_p = vf_s = None
    outs = [[] for _ in range(14)]
    for l in range(depth):
        lw = _layer_weights(l, prm)
        final = l == depth - 1
        xp, p, mkv, rw_p, ssm_p, vf_p = _prompt_layer(xp, l, lw, mem, vf_p, nb, t, final_g, final,
                                                      min(1024, nb * t), min(512, t))
        p3 = p.reshape(nb, t, N_PROJ)
        last = p3[:, t - 1]
        outs[0].append(_cols(p3, C_DK).reshape(nb, t, H_DIFF, 2, DQK))
        outs[1].append(_cols(p3, C_DV).reshape(nb, t, H_DIFF, DV))
        outs[2].append(mkv[:, :BRANCH].reshape(nb, N_MEM, H_X, D_X))
        outs[3].append(mkv[:, BRANCH:].reshape(nb, N_MEM, H_X, D_X))
        outs[4].append(jnp.concatenate([_cols(last, C_RR), _cols(last, C_RK), _cols(last, C_RV),
                                        _cols(last, C_LORA128, 128)], axis=-1))
        outs[5].append(rw_p.reshape(nb, H_RWKV, R_HEAD, R_HEAD))
        tail = p3[:, t - (M_CONV - 1):]
        outs[6].append(jnp.concatenate([_cols(tail, C_MX), _cols(tail, C_MBC)], axis=-1))
        outs[7].append(ssm_p)

        caches = (ck, cv, cmk, cmv, page_table, state_rwkv_shift[l], state_rwkv[l], state_conv[l], state_ssm[l])
        xs, ps, rw_s, ssm_s, vf_s = _decode_layer(xs, l, lw, caches, vf_s, final_g, final)
        outs[8].append(_cols(ps, C_DK).reshape(n, 1, H_DIFF, 2, DQK))
        outs[9].append(_cols(ps, C_DV).reshape(n, 1, H_DIFF, DV))
        outs[10].append(jnp.concatenate([_cols(ps, C_RR), _cols(ps, C_RK), _cols(ps, C_RV),
                                         _cols(ps, C_LORA128, 128)], axis=-1))
        outs[11].append(rw_s.reshape(n, H_RWKV, R_HEAD, R_HEAD))
        outs[12].append(jnp.concatenate(
            [state_conv[l][:, 1:], jnp.concatenate([_cols(ps, C_MX), _cols(ps, C_MBC)], axis=-1)[:, None]], axis=1))
        outs[13].append(ssm_s.reshape(n, H_MAMBA, M_HEADDIM, M_STATE))

    return (xp.reshape(nb, t, D_MODEL), xs.reshape(n, 1, D_MODEL)) + tuple(jnp.stack(o) for o in outs)
```

```python
import functools
import math

import jax
import jax.numpy as jnp
from jax import lax
from jax.experimental import pallas as pl
from jax.experimental.pallas import tpu as pltpu

F32 = jnp.float32
BF16 = jnp.bfloat16
HIGHEST = lax.Precision.HIGHEST

D_MODEL = 2048
BRANCH = 512
H_DIFF = 4
DQK = 64
DV = 128
R_HEAD = 64
H_RWKV = 8
N_LORA = 128
N_SHIFT = 3 * BRANCH + N_LORA
DECAY_SCALE = 0.6065306597
LNX_EPS = 64e-5
M_HEADDIM = 64
H_MAMBA = 8
M_GROUPS = 2
M_STATE = 128
M_CONV = 4
SSD_CHUNK = 128
H_X = 4
D_X = 128
N_MEM = 256
RMS_EPS = 1e-5
PAGE_SIZE = 128

C_DQ, C_DK, C_DV, C_DG, C_RR, C_RK, C_RV, C_RG, C_MZ, C_XQ, C_XG, C_MX, C_MBC = range(13)
C_LORA128 = 13 * BRANCH // 128
C_DT128 = C_LORA128 + 1
N_PROJ = 13 * BRANCH + 2 * 128

VMEM_LIMIT = 56 * 1024 * 1024


def _cparams(sem):
    return pltpu.CompilerParams(dimension_semantics=sem, vmem_limit_bytes=VMEM_LIMIT)


def _silu(x):
    return x * jax.nn.sigmoid(x)


def _dot_t(a, b):
    return lax.dot_general(a, b, (((1,), (1,)), ((), ())), preferred_element_type=F32)


def _rms_matmul_kernel(x_ref, g_ref, w_ref, o_ref, h_ref):
    @pl.when(pl.program_id(1) == 0)
    def _():
        x = x_ref[...]
        ms = jnp.mean(x * x, axis=-1, keepdims=True)
        h_ref[...] = (x * lax.rsqrt(ms + RMS_EPS) * g_ref[...]).astype(BF16)

    o_ref[...] = jnp.dot(h_ref[...], w_ref[...], preferred_element_type=F32)


def _rms_matmul(x, g, w, tm, tn):
    m, k = x.shape
    n = w.shape[1]
    return pl.pallas_call(
        _rms_matmul_kernel,
        grid=(m // tm, n // tn),
        in_specs=[pl.BlockSpec((tm, k), lambda i, j: (i, 0)),
                  pl.BlockSpec((1, k), lambda i, j: (0, 0)),
                  pl.BlockSpec((k, tn), lambda i, j: (0, j))],
        out_specs=pl.BlockSpec((tm, tn), lambda i, j: (i, j)),
        out_shape=jax.ShapeDtypeStruct((m, n), F32),
        scratch_shapes=[pltpu.VMEM((tm, k), BF16)],
        compiler_params=_cparams(("parallel", "arbitrary")),
        name="rms_matmul",
    )(x, g.reshape(1, k), w)


def _outproj_kernel(a_ref, b_ref, m_ref, c_ref, w_ref, x_ref, fg_ref, o_ref, *, final):
    acc = x_ref[...]
    for g, r in enumerate((a_ref, b_ref, m_ref, c_ref)):
        acc = acc + jnp.dot(r[...], w_ref[g * BRANCH:(g + 1) * BRANCH, :], preferred_element_type=F32)
    if final:
        ms = jnp.mean(acc * acc, axis=-1, keepdims=True)
        acc = acc * lax.rsqrt(ms + RMS_EPS) * fg_ref[...]
    o_ref[...] = acc


def _outproj(parts, w, x, fg, tm, final):
    m = x.shape[0]
    part_spec = pl.BlockSpec((tm, BRANCH), lambda i: (i, 0))
    return pl.pallas_call(
        functools.partial(_outproj_kernel, final=final),
        grid=(m // tm,),
        in_specs=[part_spec] * 4 + [pl.BlockSpec((D_MODEL, D_MODEL), lambda i: (0, 0)),
                                    pl.BlockSpec((tm, D_MODEL), lambda i: (i, 0)),
                                    pl.BlockSpec((1, D_MODEL), lambda i: (0, 0))],
        out_specs=pl.BlockSpec((tm, D_MODEL), lambda i: (i, 0)),
        out_shape=jax.ShapeDtypeStruct((m, D_MODEL), F32),
        compiler_params=_cparams(("parallel",)),
        name="outproj",
    )(*parts, w, x, fg.reshape(1, D_MODEL))


def _alibi_slope(h):
    return 2.0 ** (-8.0 * (h + 1) / H_DIFF)


def _lambda(lq1, lk1, lq2, lk2, lam_init):
    s1 = jnp.sum(lq1[...] * lk1[...], axis=-1, keepdims=True)
    s2 = jnp.sum(lq2[...] * lk2[...], axis=-1, keepdims=True)
    return jnp.exp(s1) - jnp.exp(s2) + lam_init


def _subln_gate(o, subg, gate, lam_init):
    ms = jnp.mean(o * o, axis=-1, keepdims=True)
    o = (o * lax.rsqrt(ms + RMS_EPS) * subg) * (1.0 - lam_init)
    return o * _silu(gate)


def _diff_prompt_kernel(lq1, lk1, lq2, lk2, subg_ref, q_ref, k_ref, v_ref, g_ref, o_ref,
                        m_ref, l_ref, acc_ref, *, tq, lam_init):
    qi = pl.program_id(1)
    ki = pl.program_id(2)

    @pl.when(ki == 0)
    def _():
        m_ref[...] = jnp.full(m_ref.shape, -1e30, F32)
        l_ref[...] = jnp.zeros(l_ref.shape, F32)
        acc_ref[...] = jnp.zeros(acc_ref.shape, F32)

    def block(masked):
        dist = (lax.broadcasted_iota(jnp.int32, (tq, tq), 0)
                - lax.broadcasted_iota(jnp.int32, (tq, tq), 1)).astype(F32) + ((qi - ki) * tq).astype(F32)
        q = (q_ref[...] * (DQK ** -0.5)).astype(BF16)
        k = k_ref[...].astype(BF16)
        v = v_ref[...].astype(BF16)
        for h in range(H_DIFF):
            bias = (-_alibi_slope(h)) * dist
            if masked:
                bias = jnp.where(dist >= 0, bias, -jnp.inf)
            vh = v[:, h * DV:(h + 1) * DV]
            for c in range(2):
                lo = h * 2 * DQK + c * DQK
                j = 2 * h + c
                s = _dot_t(q[:, lo:lo + DQK], k[:, lo:lo + DQK]) + bias
                m_prev = m_ref[j]
                m_new = jnp.maximum(m_prev, jnp.max(s, axis=-1, keepdims=True))
                alpha = jnp.exp(m_prev - m_new)
                p = jnp.exp(s - m_new)
                l_ref[j] = alpha * l_ref[j] + jnp.sum(p, axis=-1, keepdims=True)
                acc_ref[j] = alpha * acc_ref[j] + jnp.dot(p.astype(BF16), vh, preferred_element_type=F32)
                m_ref[j] = m_new

    @pl.when(ki < qi)
    def _():
        block(False)

    @pl.when(ki == qi)
    def _():
        block(True)
        lam = _lambda(lq1, lk1, lq2, lk2, lam_init)
        for h in range(H_DIFF):
            o = acc_ref[2 * h] / l_ref[2 * h] - lam * (acc_ref[2 * h + 1] / l_ref[2 * h + 1])
            sl = slice(h * DV, (h + 1) * DV)
            o_ref[:, sl] = _subln_gate(o, subg_ref[...], g_ref[:, sl], lam_init).astype(o_ref.dtype)


def _diff_prompt(p, lam_params, subg, nb, t, lam_init):
    tq = min(512, t)
    nq = t // tq
    vec = pl.BlockSpec((1, DQK), lambda b, i, j: (0, 0))

    def qmap(c):
        return lambda b, i, j: (b * nq + i, c)

    def kmap(c):
        return lambda b, i, j: (b * nq + jnp.minimum(i, j), c)

    return pl.pallas_call(
        functools.partial(_diff_prompt_kernel, tq=tq, lam_init=lam_init),
        grid=(nb, nq, nq),
        in_specs=[vec, vec, vec, vec, pl.BlockSpec((1, DV), lambda b, i, j: (0, 0)),
                  pl.BlockSpec((tq, BRANCH), qmap(C_DQ)),
                  pl.BlockSpec((tq, BRANCH), kmap(C_DK)),
                  pl.BlockSpec((tq, BRANCH), kmap(C_DV)),
                  pl.BlockSpec((tq, BRANCH), qmap(C_DG))],
        out_specs=pl.BlockSpec((tq, BRANCH), lambda b, i, j: (b * nq + i, 0)),
        out_shape=jax.ShapeDtypeStruct((nb * t, BRANCH), BF16),
        scratch_shapes=[pltpu.VMEM((2 * H_DIFF, tq, 1), F32), pltpu.VMEM((2 * H_DIFF, tq, 1), F32),
                        pltpu.VMEM((2 * H_DIFF, tq, DV), F32)],
        compiler_params=_cparams(("parallel", "parallel", "arbitrary")),
        name="diff_attn_prompt",
    )(*lam_params, subg.reshape(1, DV), p, p, p, p)


def _diff_decode_kernel(pt_ref, lq1, lk1, lq2, lk2, subg_ref, q_ref, kc_ref, vc_ref, g_ref, *rest,
                        n_pages, lam_init):
    kp = rest[:n_pages]
    vp = rest[n_pages:2 * n_pages]
    o_ref = rest[2 * n_pages]
    nrow = 2 * H_DIFF
    row = lax.broadcasted_iota(jnp.int32, (nrow, BRANCH), 0)
    lane = lax.broadcasted_iota(jnp.int32, (nrow, BRANCH), 1)
    q = q_ref[...] * (DQK ** -0.5)
    qm = jnp.where(lane // DQK == row, q, 0.0).astype(BF16)

    hrow = lax.broadcasted_iota(jnp.int32, (nrow, 1), 0) // 2
    slope = jnp.zeros((nrow, 1), F32)
    for h in range(H_DIFF):
        slope = jnp.where(hrow == h, _alibi_slope(h), slope)

    past = n_pages * PAGE_SIZE
    s = jnp.concatenate([_dot_t(qm, kp[j][...].astype(BF16)) for j in range(n_pages)], axis=-1)
    dist = (past - lax.broadcasted_iota(jnp.int32, (nrow, past), 1)).astype(F32)
    s = s + (-slope) * dist
    kc = kc_ref[...].astype(BF16).astype(F32)
    vc = vc_ref[...].astype(BF16).astype(F32)
    sc = jnp.sum(qm.astype(F32) * kc, axis=-1, keepdims=True)
    m = jnp.maximum(jnp.max(s, axis=-1, keepdims=True), sc)
    p = jnp.exp(s - m)
    pc = jnp.exp(sc - m)
    l_fin = jnp.sum(p, axis=-1, keepdims=True) + pc
    pb = p.astype(BF16)
    acc = pc.astype(BF16).astype(F32) * vc
    for j in range(n_pages):
        acc = acc + jnp.dot(pb[:, j * PAGE_SIZE:(j + 1) * PAGE_SIZE], vp[j][...].astype(BF16),
                            preferred_element_type=F32)
    lam = _lambda(lq1, lk1, lq2, lk2, lam_init)
    sign = jnp.where(lax.broadcasted_iota(jnp.int32, (nrow, 1), 0) % 2 == 0, 1.0, -lam)
    contrib = jnp.where(lane // DV == row // 2, (acc / l_fin) * sign, 0.0)
    o = jnp.sum(contrib, axis=0, keepdims=True)
    g = g_ref[...]
    outs = [_subln_gate(o[:, h * DV:(h + 1) * DV], subg_ref[...], g[:, h * DV:(h + 1) * DV], lam_init)
            for h in range(H_DIFF)]
    o_ref[...] = jnp.concatenate(outs, axis=-1).astype(o_ref.dtype)


def _diff_decode(p3, layer, cache_k, cache_v, page_table, lam_params, subg, lam_init):
    nb, n_pages = page_table.shape
    vec = pl.BlockSpec((1, DQK), lambda b, pt: (0, 0))

    def row(c):
        return pl.BlockSpec((None, 1, BRANCH), lambda b, pt: (b, 0, c))

    def page(j):
        return pl.BlockSpec((None, None, PAGE_SIZE, BRANCH), lambda b, pt: (layer, pt[b * n_pages + j], 0, 0))

    pages = [page(j) for j in range(n_pages)]
    return pl.pallas_call(
        functools.partial(_diff_decode_kernel, n_pages=n_pages, lam_init=lam_init),
        grid_spec=pltpu.PrefetchScalarGridSpec(
            num_scalar_prefetch=1,
            grid=(nb,),
            in_specs=[vec, vec, vec, vec, pl.BlockSpec((1, DV), lambda b, pt: (0, 0)),
                      row(C_DQ), row(C_DK), row(C_DV), row(C_DG)] + pages + pages,
            out_specs=pl.BlockSpec((None, 1, BRANCH), lambda b, pt: (b, 0, 0))),
        out_shape=jax.ShapeDtypeStruct((nb, 1, BRANCH), BF16),
        compiler_params=_cparams(("parallel",)),
        name="diff_attn_decode",
    )(page_table.reshape(-1), *lam_params, subg.reshape(1, DV), p3, p3, p3, p3,
      *([cache_k] * n_pages), *([cache_v] * n_pages))


def _seg_sum(x, bd_ref):
    return jnp.dot(x, bd_ref[...], preferred_element_type=F32, precision=HIGHEST)


def _rwkv_pre_kernel(*refs, shift, has_vres, tm):
    it = iter(refs)
    cur = [next(it) for _ in range(4)]
    prev = None if shift else [next(it) for _ in range(4)]
    mu = [next(it) for _ in range(4)]
    w0, wup, a0, aup, kk_ref, ka_ref, rk_ref, bd_ref = [next(it) for _ in range(8)]
    if has_vres:
        vf_ref, vr0, vra, vrb = [next(it) for _ in range(4)]
    outs = [next(it) for _ in range(7)]
    carry = [next(it) for _ in range(4)] if shift else None

    xs = []
    for n in range(4):
        c = cur[n][...]
        if shift:
            @pl.when(pl.program_id(1) == 0)
            def _():
                carry[n][...] = jnp.zeros(carry[n].shape, F32)
            rolled = pltpu.roll(c, 1, 0)
            rows = lax.broadcasted_iota(jnp.int32, c.shape, 0)
            pv = jnp.where(rows == 0, carry[n][...], rolled)
            carry[n][...] = c[tm - 1:tm, :]
        else:
            pv = prev[n][...]
        xs.append(c + (pv - c) * mu[n][...])
    xr, xk, xv, xl = xs

    hw = jnp.dot(jnp.tanh(xl).astype(BF16), wup[...], preferred_element_type=F32)
    w = -DECAY_SCALE * jax.nn.sigmoid(w0[...] + hw)
    if not shift:
        w = jnp.exp(w)
    a = jax.nn.sigmoid(a0[...] + jnp.dot(xl.astype(BF16), aup[...], preferred_element_type=F32))
    if has_vres:
        t1 = jnp.dot(xv.astype(BF16), vra[...], preferred_element_type=F32)
        t2 = jnp.dot(t1.astype(BF16), vrb[...], preferred_element_type=F32)
        xv = xv + (vf_ref[...] - xv) * jax.nn.sigmoid(vr0[...] + t2)
    kk = xk * kk_ref[...]
    kk = kk / jnp.maximum(jnp.sqrt(_seg_sum(kk * kk, bd_ref)), 1e-12)
    k2 = xk * (1.0 + (a - 1.0) * ka_ref[...])
    bonus = _seg_sum(xr * k2 * rk_ref[...], bd_ref) * xv
    for o, val in zip(outs, (xr, w, k2, xv, -kk, kk * a, bonus)):
        o[...] = val


def _rwkv_pre(p, prev, lw, vfirst, nb, t, tm):
    shift = prev is None
    has_vres = vfirst is not None
    nt = t // tm
    if shift:
        grid = (nb, nt)
        rmap = lambda c: (lambda b, i: (b * nt + i, c))
        cmap = lambda b, i: (0, 0)
        sem = ("parallel", "arbitrary")
    else:
        grid = (nt,)
        rmap = lambda c: (lambda i: (i, c))
        cmap = lambda i: (0, 0)
        sem = ("parallel",)
    wide = lambda c: pl.BlockSpec((tm, BRANCH), rmap(c))
    vec = lambda n: pl.BlockSpec((1, n), cmap)
    args, specs = [], []
    args += [p, p, p, p]
    specs += [wide(C_RR), wide(C_RK), wide(C_RV), pl.BlockSpec((tm, N_LORA), rmap(C_LORA128))]
    if not shift:
        args += [prev, prev, prev, prev]
        specs += [wide(0), wide(1), wide(2), pl.BlockSpec((tm, N_LORA), rmap(3 * BRANCH // N_LORA))]
    args += lw["mu"]
    specs += [vec(BRANCH)] * 3 + [vec(N_LORA)]
    args += [lw["w0"], lw["w_up"], lw["a0"], lw["a_up"], lw["k_k"], lw["k_a"], lw["r_k"], lw["bd"]]
    specs += [vec(BRANCH), pl.BlockSpec((N_LORA, BRANCH), cmap), vec(BRANCH), pl.BlockSpec((N_LORA, BRANCH), cmap),
              vec(BRANCH), vec(BRANCH), vec(BRANCH), pl.BlockSpec((BRANCH, BRANCH), cmap)]
    if has_vres:
        args += [vfirst, lw["vres_0"], lw["vres_a"], lw["vres_b"]]
        specs += [pl.BlockSpec((tm, BRANCH), rmap(0)), vec(BRANCH),
                  pl.BlockSpec(lw["vres_a"].shape, cmap), pl.BlockSpec(lw["vres_b"].shape, cmap)]
    out_spec = pl.BlockSpec((tm, BRANCH), rmap(0))
    scratch = [pltpu.VMEM((1, BRANCH), F32)] * 3 + [pltpu.VMEM((1, N_LORA), F32)] if shift else []
    return pl.pallas_call(
        functools.partial(_rwkv_pre_kernel, shift=shift, has_vres=has_vres, tm=tm),
        grid=grid, in_specs=specs, out_specs=[out_spec] * 7,
        out_shape=[jax.ShapeDtypeStruct((nb * t, BRANCH), F32)] * 7,
        scratch_shapes=scratch,
        compiler_params=_cparams(sem),
        name="rwkv_pre",
    )(*args)


RWKV_CHUNK = 64

_NN = ((1,), (0,))
_NT = ((1,), (1,))
_TN = ((0,), (0,))


def _split(x):
    hi = x.astype(BF16)
    return hi, (x - hi.astype(F32)).astype(BF16)


def _mm3(a, b, dims):
    dg = lambda x, y: lax.dot_general(x, y, (dims, ((), ())), preferred_element_type=F32)
    return dg(a[0], b[0]) + (dg(a[0], b[1]) + dg(a[1], b[0]))


def _rwkv_scan_kernel(r_ref, lw_ref, k_ref, v_ref, a_ref, b_ref, y_ref, s_ref):
    C = RWKV_CHUNK

    @pl.when(pl.program_id(1) == 0)
    def _():
        s_ref[...] = jnp.zeros(s_ref.shape, F32)

    rowi = lax.broadcasted_iota(jnp.int32, (C, C), 0)
    coli = lax.broadcasted_iota(jnp.int32, (C, C), 1)
    incl = rowi >= coli
    strict = rowi > coli

    lw = lw_ref[...]
    tri = incl.astype(BF16)
    l1 = lw.astype(BF16)
    rem = lw - l1.astype(F32)
    l2 = rem.astype(BF16)
    l3 = (rem - l2.astype(F32)).astype(BF16)
    cs = (jnp.dot(tri, l1, preferred_element_type=F32) + jnp.dot(tri, l2, preferred_element_type=F32)
          + jnp.dot(tri, l3, preferred_element_type=F32))
    p_in = jnp.exp(cs)
    p_inv = jnp.exp(-cs)
    rt = r_ref[...] * p_in
    at = a_ref[...] * jnp.exp(cs - lw)
    bt = b_ref[...] * p_inv
    kt = k_ref[...] * p_inv
    v = v_ref[...]
    p_last = p_in[C - 1:C, :]

    heads = range(H_RWKV)
    sls = [slice(h * R_HEAD, (h + 1) * R_HEAD) for h in heads]
    ops = [[_split(z[:, sl]) for z in (at, rt, bt, kt, v)] for sl in sls]
    s0 = [s_ref[h] for h in heads]
    s0_s = [_split(z) for z in s0]
    n_ab = [jnp.where(strict, _mm3(o[0], o[2], _NT), 0.0) for o in ops]
    a_ak = [jnp.where(strict, _mm3(o[0], o[3], _NT), 0.0) for o in ops]
    a_rb = [_split(jnp.where(incl, _mm3(o[1], o[2], _NT), 0.0)) for o in ops]
    a_rk = [_split(jnp.where(incl, _mm3(o[1], o[3], _NT), 0.0)) for o in ops]
    rhs = [_mm3(ops[h][0], s0_s[h], _NT) + _mm3(_split(a_ak[h]), ops[h][4], _NN) for h in heads]
    m_s = [_split(z) for z in n_ab]
    u = [rhs[h] + _mm3(m_s[h], _split(rhs[h]), _NN) for h in heads]
    for _ in range(int(math.log2(C)) - 1):
        m_s = [_split(_mm3(z, z, _NN)) for z in m_s]
        u = [u[h] + _mm3(m_s[h], _split(u[h]), _NN) for h in heads]
    u_s = [_split(z) for z in u]
    for h in heads:
        y_ref[:, sls[h]] = (_mm3(ops[h][1], s0_s[h], _NT) + _mm3(a_rb[h], u_s[h], _NN)
                            + _mm3(a_rk[h], ops[h][4], _NN))
        s_ref[h] = (s0[h] + _mm3(u_s[h], ops[h][2], _TN) + _mm3(ops[h][4], ops[h][3], _TN)) * p_last[:, sls[h]]


def _rwkv_scan(seqs, nb, t):
    tc = RWKV_CHUNK
    blk = pl.BlockSpec((None, tc, BRANCH), lambda b, c: (b, c, 0))
    return pl.pallas_call(
        _rwkv_scan_kernel,
        grid=(nb, t // tc),
        in_specs=[blk] * 6,
        out_specs=[blk, pl.BlockSpec((H_RWKV, R_HEAD, R_HEAD), lambda b, c: (b, 0, 0))],
        out_shape=[jax.ShapeDtypeStruct((nb, t, BRANCH), F32),
                   jax.ShapeDtypeStruct((nb * H_RWKV, R_HEAD, R_HEAD), F32)],
        compiler_params=_cparams(("parallel", "arbitrary")),
        name="rwkv_scan",
    )(*seqs)


def _rwkv_step_kernel(r_ref, w_ref, k_ref, a_ref, b_ref, vt_ref, s_ref, yt_ref, so_ref, *, bb):
    units = [(j, slice(h * R_HEAD, (h + 1) * R_HEAD)) for j in range(bb) for h in range(H_RWKV)]
    row = lambda ref, j, sl: ref[j:j + 1, sl]
    st = [s_ref[j, sl, :] for j, sl in units]
    sa = [jnp.sum(z * row(a_ref, j, sl), axis=-1, keepdims=True) for z, (j, sl) in zip(st, units)]
    st = [z * row(w_ref, j, sl) + x * row(b_ref, j, sl) + vt_ref[sl, j:j + 1] * row(k_ref, j, sl)
          for z, x, (j, sl) in zip(st, sa, units)]
    for z, (j, sl) in zip(st, units):
        so_ref[j, sl, :] = z
    ys = [jnp.sum(z * row(r_ref, j, sl), axis=-1, keepdims=True) for z, (j, sl) in zip(st, units)]
    for y, (j, sl) in zip(ys, units):
        yt_ref[sl, j:j + 1] = y


def _rwkv_step(r, w, k, na, b, vt, state, bb=8):
    n = r.shape[0]
    rowb = pl.BlockSpec((bb, BRANCH), lambda i: (i, 0))
    colb = pl.BlockSpec((None, BRANCH, bb), lambda i: (i, 0, 0))
    stb = pl.BlockSpec((bb, BRANCH, R_HEAD), lambda i: (i, 0, 0))
    return pl.pallas_call(
        functools.partial(_rwkv_step_kernel, bb=bb),
        grid=(n // bb,),
        in_specs=[rowb] * 5 + [colb, stb],
        out_specs=[colb, stb],
        out_shape=[jax.ShapeDtypeStruct((n // bb, BRANCH, bb), F32),
                   jax.ShapeDtypeStruct((n, BRANCH, R_HEAD), F32)],
        compiler_params=_cparams(("parallel",)),
        name="rwkv_step",
    )(r, w, k, na, b, vt, state)


def _rwkv_post_kernel(y_ref, bonus_ref, g_ref, lg_ref, lb_ref, bd_ref, o_ref):
    y = y_ref[...]
    mu = _seg_sum(y, bd_ref) * (1.0 / R_HEAD)
    d = y - mu
    var = _seg_sum(d * d, bd_ref) * (1.0 / R_HEAD)
    o = d * lax.rsqrt(var + LNX_EPS) * lg_ref[...] + lb_ref[...] + bonus_ref[...]
    o_ref[...] = (o * _silu(g_ref[...])).astype(o_ref.dtype)


def _rwkv_post(y, bonus, p, lw, tm):
    m = y.shape[0]
    blk = pl.BlockSpec((tm, BRANCH), lambda i: (i, 0))
    vec = pl.BlockSpec((1, BRANCH), lambda i: (0, 0))
    return pl.pallas_call(
        _rwkv_post_kernel,
        grid=(m // tm,),
        in_specs=[blk, blk, pl.BlockSpec((tm, BRANCH), lambda i: (i, C_RG)), vec, vec,
                  pl.BlockSpec((BRANCH, BRANCH), lambda i: (0, 0))],
        out_specs=blk,
        out_shape=jax.ShapeDtypeStruct((m, BRANCH), BF16),
        compiler_params=_cparams(("parallel",)),
        name="rwkv_post",
    )(y, bonus, p, lw["lnx_g"], lw["lnx_b"], lw["bd"])


def _softplus(x):
    return jnp.maximum(x, 0.0) + jnp.log1p(jnp.exp(-jnp.abs(x)))


def _neg_exp_alog(alog_ref):
    lanes = lax.broadcasted_iota(jnp.int32, alog_ref.shape, 1)
    return jnp.where(lanes < H_MAMBA, -jnp.exp(alog_ref[...]), 0.0)


def _gated_norm(ym, z, g):
    u = ym * _silu(z)
    ms = jnp.mean(u * u, axis=-1, keepdims=True)
    return u * lax.rsqrt(ms + RMS_EPS) * g


def _mamba_prompt_kernel(mx_ref, mbc_ref, mdt_ref, mz_ref, cwx_ref, cwb_ref, cbx_ref, cbb_ref, dtb_ref, alog_ref,
                         dskip_ref, gn_ref, o_ref, s_ref, extx_ref, extb_ref, ym_ref):
    L = SSD_CHUNK
    pad = 8

    @pl.when(pl.program_id(1) == 0)
    def _():
        extx_ref[0:pad, :] = jnp.zeros((pad, BRANCH), F32)
        extb_ref[0:pad, :] = jnp.zeros((pad, BRANCH), F32)
        s_ref[...] = jnp.zeros(s_ref.shape, F32)

    def conv(ext_ref, cur_ref, cw_ref, cb_ref):
        ext_ref[pad:pad + L, :] = cur_ref[...]
        acc = cb_ref[...]
        for j in range(M_CONV):
            lo = pad - (M_CONV - 1) + j
            acc = acc + ext_ref[lo:lo + L, :] * cw_ref[j:j + 1, :]
        ext_ref[0:pad, :] = ext_ref[L:L + pad, :]
        return _silu(acc)

    x = conv(extx_ref, mx_ref, cwx_ref, cbx_ref)
    bc = conv(extb_ref, mbc_ref, cwb_ref, cbb_ref)
    dt = _softplus(mdt_ref[...] + dtb_ref[...])
    adt = dt * _neg_exp_alog(alog_ref)
    rowi = lax.broadcasted_iota(jnp.int32, (L, L), 0)
    coli = lax.broadcasted_iota(jnp.int32, (L, L), 1)
    lower = rowi >= coli
    acs = jnp.dot(lower.astype(F32), adt, preferred_element_type=F32, precision=HIGHEST)
    acs_t = acs.T
    last = acs[L - 1:L, :]
    dstate = jnp.exp(last - acs)
    eacs = jnp.exp(acs)
    elast = jnp.exp(last)
    bcb = bc.astype(BF16)
    grp = []
    for g in range(M_GROUPS):
        bg = bcb[:, g * M_STATE:(g + 1) * M_STATE]
        cg = bcb[:, (M_GROUPS + g) * M_STATE:(M_GROUPS + g + 1) * M_STATE]
        grp.append((bg, cg, _dot_t(cg, bg)))
    for h in range(H_MAMBA):
        bg, cg, cb = grp[h // (H_MAMBA // M_GROUPS)]
        sl = slice(h * M_HEADDIM, (h + 1) * M_HEADDIM)
        decay = jnp.where(lower, jnp.exp(acs[:, h:h + 1] - acs_t[h:h + 1, :]), 0.0)
        xh = x[:, sl]
        xdt = xh * dt[:, h:h + 1]
        yd = jnp.dot((cb * decay).astype(BF16), xdt.astype(BF16), preferred_element_type=F32)
        st = s_ref[h]
        yo = _dot_t(cg, st.astype(BF16)) * eacs[:, h:h + 1]
        upd = lax.dot_general((xdt * dstate[:, h:h + 1]).astype(BF16), bg, (((0,), (0,)), ((), ())),
                              preferred_element_type=F32)
        s_ref[h] = st * elast[:, h:h + 1] + upd
        ym_ref[:, sl] = yd + yo + dskip_ref[:, sl] * xh
    o_ref[...] = _gated_norm(ym_ref[...], mz_ref[...], gn_ref[...]).astype(o_ref.dtype)


def _mamba_prompt(p, lw, nb, t):
    L = SSD_CHUNK
    nt = t // L
    rmap = lambda c: (lambda b, i: (b * nt + i, c))
    cmap = lambda b, i: (0, 0)
    wide = lambda c: pl.BlockSpec((L, BRANCH), rmap(c))
    vec = lambda n: pl.BlockSpec((1, n), cmap)
    cw = pl.BlockSpec((M_CONV, BRANCH), cmap)
    return pl.pallas_call(
        _mamba_prompt_kernel,
        grid=(nb, nt),
        in_specs=[wide(C_MX), wide(C_MBC), pl.BlockSpec((L, 128), rmap(C_DT128)), wide(C_MZ),
                  cw, cw, vec(BRANCH), vec(BRANCH), vec(128), vec(128), vec(BRANCH), vec(BRANCH)],
        out_specs=[pl.BlockSpec((L, BRANCH), rmap(0)),
                   pl.BlockSpec((None, H_MAMBA, M_HEADDIM, M_STATE), lambda b, i: (b, 0, 0, 0))],
        out_shape=[jax.ShapeDtypeStruct((nb * t, BRANCH), BF16),
                   jax.ShapeDtypeStruct((nb, H_MAMBA, M_HEADDIM, M_STATE), F32)],
        scratch_shapes=[pltpu.VMEM((L + 8, BRANCH), F32), pltpu.VMEM((L + 8, BRANCH), F32),
                        pltpu.VMEM((L, BRANCH), F32)],
        compiler_params=_cparams(("parallel", "arbitrary")),
        name="mamba_prompt",
    )(p, p, p, p, lw["cw_x"], lw["cw_bc"], lw["cb_x"], lw["cb_bc"], lw["dt_bias"], lw["a_log"],
      lw["d_skip"], lw["gnorm_g"])


def _mamba_dec_pre_kernel(mx_ref, mbc_ref, mdt_ref, sx_ref, sb_ref, cwx_ref, cwb_ref, cbx_ref, cbb_ref,
                          dtb_ref, alog_ref, x_ref, bc_ref, xdt_ref, da_ref):
    def conv(st_ref, cur_ref, cw_ref, cb_ref):
        acc = cb_ref[...]
        for j in range(M_CONV - 1):
            acc = acc + st_ref[j] * cw_ref[j:j + 1, :]
        acc = acc + cur_ref[...] * cw_ref[M_CONV - 1:M_CONV, :]
        return _silu(acc)

    x = conv(sx_ref, mx_ref, cwx_ref, cbx_ref)
    x_ref[...] = x
    bc_ref[...] = conv(sb_ref, mbc_ref, cwb_ref, cbb_ref)
    dt = _softplus(mdt_ref[...] + dtb_ref[...])
    da = jnp.exp(dt * _neg_exp_alog(alog_ref))
    n = x.shape[0]
    for h in range(H_MAMBA):
        sl = slice(h * M_HEADDIM, (h + 1) * M_HEADDIM)
        xdt_ref[:, sl] = x[:, sl] * dt[:, h:h + 1]
        da_ref[:, sl] = jnp.broadcast_to(da[:, h:h + 1], (n, M_HEADDIM))


def _mamba_dec_pre(p, conv_x, conv_bc, lw):
    n = p.shape[0]
    cmap = lambda i: (0, 0)
    wide = lambda c: pl.BlockSpec((n, BRANCH), lambda i: (0, c))
    vec = lambda m: pl.BlockSpec((1, m), cmap)
    cw = pl.BlockSpec((M_CONV, BRANCH), cmap)
    st = pl.BlockSpec((M_CONV - 1, n, BRANCH), lambda i: (0, 0, 0))
    out = pl.BlockSpec((n, BRANCH), cmap)
    return pl.pallas_call(
        _mamba_dec_pre_kernel,
        grid=(1,),
        in_specs=[wide(C_MX), wide(C_MBC), pl.BlockSpec((n, 128), lambda i: (0, C_DT128)), st, st,
                  cw, cw, vec(BRANCH), vec(BRANCH), vec(128), vec(128)],
        out_specs=[out] * 4,
        out_shape=[jax.ShapeDtypeStruct((n, BRANCH), F32)] * 4,
        compiler_params=_cparams(("arbitrary",)),
        name="mamba_dec_pre",
    )(p, p, p, conv_x, conv_bc, lw["cw_x"], lw["cw_bc"], lw["cb_x"], lw["cb_bc"], lw["dt_bias"], lw["a_log"])


def _mamba_step_kernel(bc_ref, xdtt_ref, dat_ref, s_ref, yt_ref, so_ref, *, bb):
    units = [(j, h // (H_MAMBA // M_GROUPS), slice(h * M_HEADDIM, (h + 1) * M_HEADDIM))
             for j in range(bb) for h in range(H_MAMBA)]
    st = []
    for j, g, sl in units:
        bg = bc_ref[j:j + 1, g * M_STATE:(g + 1) * M_STATE]
        z = s_ref[j, sl, :] * dat_ref[sl, j:j + 1] + xdtt_ref[sl, j:j + 1] * bg
        so_ref[j, sl, :] = z
        st.append(z)
    ys = [jnp.sum(z * bc_ref[j:j + 1, (M_GROUPS + g) * M_STATE:(M_GROUPS + g + 1) * M_STATE], axis=-1, keepdims=True)
          for z, (j, g, sl) in zip(st, units)]
    for y, (j, g, sl) in zip(ys, units):
        yt_ref[sl, j:j + 1] = y


def _mamba_step(bc, xdtt, dat, state, bb=8):
    n = bc.shape[0]
    rowb = pl.BlockSpec((bb, BRANCH), lambda i: (i, 0))
    colb = pl.BlockSpec((None, BRANCH, bb), lambda i: (i, 0, 0))
    stb = pl.BlockSpec((bb, BRANCH, M_STATE), lambda i: (i, 0, 0))
    return pl.pallas_call(
        functools.partial(_mamba_step_kernel, bb=bb),
        grid=(n // bb,),
        in_specs=[rowb, colb, colb, stb],
        out_specs=[colb, stb],
        out_shape=[jax.ShapeDtypeStruct((n // bb, BRANCH, bb), F32),
                   jax.ShapeDtypeStruct((n, BRANCH, M_STATE), F32)],
        compiler_params=_cparams(("parallel",)),
        name="mamba_step",
    )(bc, xdtt, dat, state)


def _mamba_dec_post_kernel(y_ref, x_ref, z_ref, dskip_ref, gn_ref, o_ref):
    ym = y_ref[...] + dskip_ref[...] * x_ref[...]
    o_ref[...] = _gated_norm(ym, z_ref[...], gn_ref[...]).astype(o_ref.dtype)


def _mamba_dec_post(y, x, p, lw):
    n = y.shape[0]
    blk = pl.BlockSpec((n, BRANCH), lambda i: (0, 0))
    vec = pl.BlockSpec((1, BRANCH), lambda i: (0, 0))
    return pl.pallas_call(
        _mamba_dec_post_kernel,
        grid=(1,),
        in_specs=[blk, blk, pl.BlockSpec((n, BRANCH), lambda i: (0, C_MZ)), vec, vec],
        out_specs=blk,
        out_shape=jax.ShapeDtypeStruct((n, BRANCH), BF16),
        compiler_params=_cparams(("arbitrary",)),
        name="mamba_dec_post",
    )(y, x, p, lw["d_skip"], lw["gnorm_g"])


def _xattn_prompt_kernel(q_ref, g_ref, mk_ref, mv_ref, o_ref):
    q = q_ref[...].astype(BF16)
    mk = mk_ref[...].astype(BF16)
    mv = mv_ref[...].astype(BF16)
    for h in range(H_X):
        sl = slice(h * D_X, (h + 1) * D_X)
        s = _dot_t(q[:, sl], mk[:, sl]) * (D_X ** -0.5)
        e = jnp.exp(s - jnp.max(s, axis=-1, keepdims=True))
        o = jnp.dot(e.astype(BF16), mv[:, sl], preferred_element_type=F32) / jnp.sum(e, axis=-1, keepdims=True)
        o_ref[:, sl] = (o * _silu(g_ref[:, sl])).astype(o_ref.dtype)


def _xattn_prompt(p, mkv, nb, t):
    tq = min(512, t)
    nq = t // tq
    return pl.pallas_call(
        _xattn_prompt_kernel,
        grid=(nb, nq),
        in_specs=[pl.BlockSpec((tq, BRANCH), lambda b, i: (b * nq + i, C_XQ)),
                  pl.BlockSpec((tq, BRANCH), lambda b, i: (b * nq + i, C_XG)),
                  pl.BlockSpec((N_MEM, BRANCH), lambda b, i: (b, 0)),
                  pl.BlockSpec((N_MEM, BRANCH), lambda b, i: (b, 1))],
        out_specs=pl.BlockSpec((tq, BRANCH), lambda b, i: (b * nq + i, 0)),
        out_shape=jax.ShapeDtypeStruct((nb * t, BRANCH), BF16),
        compiler_params=_cparams(("parallel", "parallel")),
        name="xattn_prompt",
    )(p, p, mkv, mkv)


def _xattn_decode_kernel(q_ref, g_ref, mk_ref, mv_ref, o_ref):
    nrow = 8
    row = lax.broadcasted_iota(jnp.int32, (nrow, BRANCH), 0)
    lane = lax.broadcasted_iota(jnp.int32, (nrow, BRANCH), 1)
    sel = lane // D_X == row
    qm = jnp.where(sel, q_ref[...], 0.0).astype(BF16)
    s = _dot_t(qm, mk_ref[...].astype(BF16)) * (D_X ** -0.5)
    e = jnp.exp(s - jnp.max(s, axis=-1, keepdims=True))
    r = jnp.dot(e.astype(BF16), mv_ref[...].astype(BF16), preferred_element_type=F32)
    r = r / jnp.sum(e, axis=-1, keepdims=True)
    o = jnp.sum(jnp.where(sel, r, 0.0), axis=0, keepdims=True)
    o_ref[...] = (o * _silu(g_ref[...])).astype(o_ref.dtype)


def _xattn_decode(p3, layer, mem_k, mem_v):
    n = p3.shape[0]
    mem = pl.BlockSpec((None, None, N_MEM, BRANCH), lambda b: (layer, b, 0, 0))
    return pl.pallas_call(
        _xattn_decode_kernel,
        grid=(n,),
        in_specs=[pl.BlockSpec((None, 1, BRANCH), lambda b: (b, 0, C_XQ)),
                  pl.BlockSpec((None, 1, BRANCH), lambda b: (b, 0, C_XG)), mem, mem],
        out_specs=pl.BlockSpec((None, 1, BRANCH), lambda b: (b, 0, 0)),
        out_shape=jax.ShapeDtypeStruct((n, 1, BRANCH), BF16),
        compiler_params=_cparams(("parallel",)),
        name="xattn_decode",
    )(p3, p3, mem_k, mem_v)


def _layer_weights(l, prm):
    w = prm["w_in"][l]
    o_rs, o_rg, o_mz, o_mxbc, o_dt, o_xq, o_xg = 2048, 3712, 4224, 4736, 5760, 5768, 6280
    w_perm = jnp.concatenate([
        w[:, :o_rs + 3 * BRANCH], w[:, o_rg:o_mz], w[:, o_mz:o_mxbc], w[:, o_xq:o_xg], w[:, o_xg:],
        w[:, o_mxbc:o_dt], w[:, o_rs + 3 * BRANCH:o_rg], w[:, o_dt:o_xq],
        jnp.zeros((D_MODEL, 128 - H_MAMBA), w.dtype)], axis=1).astype(BF16)
    row = lambda v: v.reshape(1, -1).astype(F32)
    mu = prm["shift_mu"][l].astype(F32)
    seg = jnp.arange(BRANCH) // R_HEAD
    zl = jnp.zeros((N_LORA // 2, BRANCH), F32)
    cw = prm["conv_w"][l].astype(F32)
    cb = prm["conv_b"][l].astype(F32)
    pad8 = lambda v: jnp.pad(v.astype(F32), (0, 128 - H_MAMBA)).reshape(1, 128)
    lw = {
        "w_in": w_perm, "norm_g": prm["norm_g"][l], "w_out": prm["w_out"][l].astype(BF16),
        "lam": [row(prm[n][l]) for n in ("lam_q1", "lam_k1", "lam_q2", "lam_k2")],
        "subln_g": prm["subln_g"][l].astype(F32),
        "mu": [row(mu[:BRANCH]), row(mu[BRANCH:2 * BRANCH]), row(mu[2 * BRANCH:3 * BRANCH]), row(mu[3 * BRANCH:])],
        "w0": row(prm["w0"][l]), "a0": row(prm["a0"][l]),
        "w_up": jnp.concatenate([prm["w_up"][l].astype(F32), zl], axis=0).astype(BF16),
        "a_up": jnp.concatenate([zl, prm["a_up"][l].astype(F32)], axis=0).astype(BF16),
        "k_k": row(prm["k_k"][l]), "k_a": row(prm["k_a"][l]), "r_k": row(prm["r_k"][l]),
        "lnx_g": row(prm["lnx_g"][l]), "lnx_b": row(prm["lnx_b"][l]),
        "bd": (seg[:, None] == seg[None, :]).astype(F32),
        "cw_x": cw[:, :BRANCH], "cw_bc": cw[:, BRANCH:], "cb_x": row(cb[:BRANCH]), "cb_bc": row(cb[BRANCH:]),
        "dt_bias": pad8(prm["dt_bias"][l]), "a_log": pad8(prm["a_log"][l]),
        "d_skip": row(jnp.repeat(prm["d_skip"][l].astype(F32), M_HEADDIM)),
        "gnorm_g": row(prm["gnorm_g"][l]),
        "w_mkv": jnp.concatenate([prm["w_mk"][l], prm["w_mv"][l]], axis=1).astype(BF16),
        "mem_norm_g": prm["mem_norm_g"][l],
    }
    if l > 0:
        lw["vres_0"] = row(prm["vres_0"][l - 1])
        lw["vres_a"] = prm["vres_a"][l - 1].astype(BF16)
        lw["vres_b"] = prm["vres_b"][l - 1].astype(BF16)
    return lw


def _lam_init(l):
    return 0.8 - 0.6 * math.exp(-0.3 * l)


def _prompt_layer(x, l, lw, mem, vfirst, nb, t, fg, final, tm_proj, tm_row):
    p = _rms_matmul(x, lw["norm_g"], lw["w_in"], tm_proj, 1152)
    mkv = _rms_matmul(mem, lw["mem_norm_g"], lw["w_mkv"], mem.shape[0], 2 * BRANCH)
    out_a = _diff_prompt(p, lw["lam"], lw["subln_g"], nb, t, _lam_init(l))
    r, w, k, v, na, b, bonus = _rwkv_pre(p, None, lw, vfirst, nb, t, tm_row)
    seqs = [z.reshape(nb, t, BRANCH) for z in (r, w, k, v, na, b)]
    y, rw_state = _rwkv_scan(seqs, nb, t)
    out_b = _rwkv_post(y.reshape(nb * t, BRANCH), bonus, p, lw, tm_row)
    out_m, ssm_state = _mamba_prompt(p, lw, nb, t)
    out_x = _xattn_prompt(p, mkv, nb, t)
    x = _outproj((out_a, out_b, out_m, out_x), lw["w_out"], x, fg, tm_row, final)
    return x, p, mkv, rw_state, ssm_state, (v if vfirst is None else vfirst)


def _decode_layer(x, l, lw, caches, vfirst, fg, final):
    cache_k, cache_v, mem_k, mem_v, page_table, rw_prev, rw_state, conv_state, ssm_state = caches
    n = x.shape[0]
    bb = 8
    p = _rms_matmul(x, lw["norm_g"], lw["w_in"], n, 1152)
    p3 = p.reshape(n, 1, N_PROJ)
    out_a = _diff_decode(p3, l, cache_k, cache_v, page_table, lw["lam"], lw["subln_g"], _lam_init(l)).reshape(n, BRANCH)

    r, w, k, v, na, b, bonus = _rwkv_pre(p, rw_prev, lw, vfirst, 1, n, n)
    to_cols = lambda z: z.reshape(n // bb, bb, BRANCH).transpose(0, 2, 1)
    from_cols = lambda z: z.transpose(0, 2, 1).reshape(n, BRANCH)
    yt, rw_new = _rwkv_step(r, w, k, na, b, to_cols(v), rw_state.reshape(n, BRANCH, R_HEAD), bb)
    out_b = _rwkv_post(from_cols(yt), bonus, p, lw, n)

    cs = conv_state.transpose(1, 0, 2)
    xm, bc, xdt, da = _mamba_dec_pre(p, cs[:, :, :BRANCH], cs[:, :, BRANCH:], lw)
    ymt, ssm_new = _mamba_step(bc, to_cols(xdt), to_cols(da), ssm_state.reshape(n, BRANCH, M_STATE), bb)
    out_m = _mamba_dec_post(from_cols(ymt), xm, p, lw)

    out_x = _xattn_decode(p3, l, mem_k, mem_v).reshape(n, BRANCH)
    x = _outproj((out_a, out_b, out_m, out_x), lw["w_out"], x, fg, n, final)
    return x, p, rw_new, ssm_new, (v if vfirst is None else vfirst)


def _cols(p, c, width=BRANCH):
    return p[..., c * width:(c + 1) * width]


def kernel(x_prompt, x_sample, cache_diff_k, cache_diff_v, cache_mem_k, cache_mem_v, state_rwkv_shift, state_rwkv, state_conv, state_ssm, page_table, mem_prompt, norm_g, w_in, w_out, lam_q1, lam_k1, lam_q2, lam_k2, subln_g, shift_mu, w0, w_up, a0, a_up, k_k, k_a, r_k, lnx_g, lnx_b, vres_0, vres_a, vres_b, conv_w, conv_b, dt_bias, a_log, d_skip, gnorm_g, mem_norm_g, w_mk, w_mv, final_g):
    prm = dict(norm_g=norm_g, w_in=w_in, w_out=w_out, lam_q1=lam_q1, lam_k1=lam_k1, lam_q2=lam_q2, lam_k2=lam_k2,
               subln_g=subln_g, shift_mu=shift_mu, w0=w0, w_up=w_up, a0=a0, a_up=a_up, k_k=k_k, k_a=k_a, r_k=r_k,
               lnx_g=lnx_g, lnx_b=lnx_b, vres_0=vres_0, vres_a=vres_a, vres_b=vres_b, conv_w=conv_w, conv_b=conv_b,
               dt_bias=dt_bias, a_log=a_log, d_skip=d_skip, gnorm_g=gnorm_g, mem_norm_g=mem_norm_g,
               w_mk=w_mk, w_mv=w_mv)
    depth = w_in.shape[0]
    nb, t, _ = x_prompt.shape
    n = x_sample.shape[0]
    n_pool = cache_diff_k.shape[1]
    ck = cache_diff_k.reshape(depth, n_pool, PAGE_SIZE, BRANCH)
    cv = cache_diff_v.reshape(depth, n_pool, PAGE_SIZE, BRANCH)
    cmk = cache_mem_k.reshape(depth, n, N_MEM, BRANCH)
    cmv = cache_mem_v.reshape(depth, n, N_MEM, BRANCH)
    mem = mem_prompt.reshape(nb * N_MEM, D_MODEL)

    xp = x_prompt.reshape(nb * t, D_MODEL)
    xs = x_sample.reshape(n, D_MODEL)
    vf_p = vf_s = None
    outs = [[] for _ in range(14)]
    for l in range(depth):
        lw = _layer_weights(l, prm)
        final = l == depth - 1
        xp, p, mkv, rw_p, ssm_p, vf_p = _prompt_layer(xp, l, lw, mem, vf_p, nb, t, final_g, final,
                                                      min(1024, nb * t), min(512, t))
        p3 = p.reshape(nb, t, N_PROJ)
        last = p3[:, t - 1]
        outs[0].append(_cols(p3, C_DK).reshape(nb, t, H_DIFF, 2, DQK))
        outs[1].append(_cols(p3, C_DV).reshape(nb, t, H_DIFF, DV))
        outs[2].append(mkv[:, :BRANCH].reshape(nb, N_MEM, H_X, D_X))
        outs[3].append(mkv[:, BRANCH:].reshape(nb, N_MEM, H_X, D_X))
        outs[4].append(jnp.concatenate([_cols(last, C_RR), _cols(last, C_RK), _cols(last, C_RV),
                                        _cols(last, C_LORA128, 128)], axis=-1))
        outs[5].append(rw_p.reshape(nb, H_RWKV, R_HEAD, R_HEAD))
        tail = p3[:, t - (M_CONV - 1):]
        outs[6].append(jnp.concatenate([_cols(tail, C_MX), _cols(tail, C_MBC)], axis=-1))
        outs[7].append(ssm_p)

        caches = (ck, cv, cmk, cmv, page_table, state_rwkv_shift[l], state_rwkv[l], state_conv[l], state_ssm[l])
        xs, ps, rw_s, ssm_s, vf_s = _decode_layer(xs, l, lw, caches, vf_s, final_g, final)
        outs[8].append(_cols(ps, C_DK).reshape(n, 1, H_DIFF, 2, DQK))
        outs[9].append(_cols(ps, C_DV).reshape(n, 1, H_DIFF, DV))
        outs[10].append(jnp.concatenate([_cols(ps, C_RR), _cols(ps, C_RK), _cols(ps, C_RV),
                                         _cols(ps, C_LORA128, 128)], axis=-1))
        outs[11].append(rw_s.reshape(n, H_RWKV, R_HEAD, R_HEAD))
        outs[12].append(jnp.concatenate(
            [state_conv[l][:, 1:], jnp.concatenate([_cols(ps, C_MX), _cols(ps, C_MBC)], axis=-1)[:, None]], axis=1))
        outs[13].append(ssm_s.reshape(n, H_MAMBA, M_HEADDIM, M_STATE))

    return (xp.reshape(nb, t, D_MODEL), xs.reshape(n, 1, D_MODEL)) + tuple(jnp.stack(o) for o in outs)
```

```python
import functools
import math

import jax
import jax.numpy as jnp
from jax import lax
from jax.experimental import pallas as pl
from jax.experimental.pallas import tpu as pltpu

F32 = jnp.float32
BF16 = jnp.bfloat16
HIGHEST = lax.Precision.HIGHEST

D_MODEL = 2048
BRANCH = 512
H_DIFF = 4
DQK = 64
DV = 128
R_HEAD = 64
H_RWKV = 8
N_LORA = 128
N_SHIFT = 3 * BRANCH + N_LORA
DECAY_SCALE = 0.6065306597
LNX_EPS = 64e-5
M_HEADDIM = 64
H_MAMBA = 8
M_GROUPS = 2
M_STATE = 128
M_CONV = 4
SSD_CHUNK = 128
H_X = 4
D_X = 128
N_MEM = 256
RMS_EPS = 1e-5
PAGE_SIZE = 128

C_DQ, C_DK, C_DV, C_DG, C_RR, C_RK, C_RV, C_RG, C_MZ, C_XQ, C_XG, C_MX, C_MBC = range(13)
C_LORA128 = 13 * BRANCH // 128
C_DT128 = C_LORA128 + 1
N_PROJ = 13 * BRANCH + 2 * 128

VMEM_LIMIT = 56 * 1024 * 1024


def _cparams(sem):
    return pltpu.CompilerParams(dimension_semantics=sem, vmem_limit_bytes=VMEM_LIMIT)


def _silu(x):
    return x * jax.nn.sigmoid(x)


def _dot_t(a, b):
    return lax.dot_general(a, b, (((1,), (1,)), ((), ())), preferred_element_type=F32)


def _rms_matmul_kernel(x_ref, g_ref, w_ref, o_ref, h_ref):
    @pl.when(pl.program_id(1) == 0)
    def _():
        x = x_ref[...]
        ms = jnp.mean(x * x, axis=-1, keepdims=True)
        h_ref[...] = (x * lax.rsqrt(ms + RMS_EPS) * g_ref[...]).astype(BF16)

    o_ref[...] = jnp.dot(h_ref[...], w_ref[...], preferred_element_type=F32)


def _rms_matmul(x, g, w, tm, tn):
    m, k = x.shape
    n = w.shape[1]
    return pl.pallas_call(
        _rms_matmul_kernel,
        grid=(m // tm, n // tn),
        in_specs=[pl.BlockSpec((tm, k), lambda i, j: (i, 0)),
                  pl.BlockSpec((1, k), lambda i, j: (0, 0)),
                  pl.BlockSpec((k, tn), lambda i, j: (0, j))],
        out_specs=pl.BlockSpec((tm, tn), lambda i, j: (i, j)),
        out_shape=jax.ShapeDtypeStruct((m, n), F32),
        scratch_shapes=[pltpu.VMEM((tm, k), BF16)],
        compiler_params=_cparams(("parallel", "arbitrary")),
        name="rms_matmul",
    )(x, g.reshape(1, k), w)


def _outproj_kernel(a_ref, b_ref, m_ref, c_ref, w_ref, x_ref, fg_ref, o_ref, *, final):
    acc = x_ref[...]
    for g, r in enumerate((a_ref, b_ref, m_ref, c_ref)):
        acc = acc + jnp.dot(r[...], w_ref[g * BRANCH:(g + 1) * BRANCH, :], preferred_element_type=F32)
    if final:
        ms = jnp.mean(acc * acc, axis=-1, keepdims=True)
        acc = acc * lax.rsqrt(ms + RMS_EPS) * fg_ref[...]
    o_ref[...] = acc


def _outproj(parts, w, x, fg, tm, final):
    m = x.shape[0]
    part_spec = pl.BlockSpec((tm, BRANCH), lambda i: (i, 0))
    return pl.pallas_call(
        functools.partial(_outproj_kernel, final=final),
        grid=(m // tm,),
        in_specs=[part_spec] * 4 + [pl.BlockSpec((D_MODEL, D_MODEL), lambda i: (0, 0)),
                                    pl.BlockSpec((tm, D_MODEL), lambda i: (i, 0)),
                                    pl.BlockSpec((1, D_MODEL), lambda i: (0, 0))],
        out_specs=pl.BlockSpec((tm, D_MODEL), lambda i: (i, 0)),
        out_shape=jax.ShapeDtypeStruct((m, D_MODEL), F32),
        compiler_params=_cparams(("parallel",)),
        name="outproj",
    )(*parts, w, x, fg.reshape(1, D_MODEL))


def _alibi_slope(h):
    return 2.0 ** (-8.0 * (h + 1) / H_DIFF)


def _lambda(lq1, lk1, lq2, lk2, lam_init):
    s1 = jnp.sum(lq1[...] * lk1[...], axis=-1, keepdims=True)
    s2 = jnp.sum(lq2[...] * lk2[...], axis=-1, keepdims=True)
    return jnp.exp(s1) - jnp.exp(s2) + lam_init


def _subln_gate(o, subg, gate, lam_init):
    ms = jnp.mean(o * o, axis=-1, keepdims=True)
    o = (o * lax.rsqrt(ms + RMS_EPS) * subg) * (1.0 - lam_init)
    return o * _silu(gate)


def _diff_prompt_kernel(lq1, lk1, lq2, lk2, subg_ref, q_ref, k_ref, v_ref, g_ref, o_ref,
                        m_ref, l_ref, acc_ref, *, tq, lam_init):
    qi = pl.program_id(1)
    ki = pl.program_id(2)

    @pl.when(ki == 0)
    def _():
        m_ref[...] = jnp.full(m_ref.shape, -1e30, F32)
        l_ref[...] = jnp.zeros(l_ref.shape, F32)
        acc_ref[...] = jnp.zeros(acc_ref.shape, F32)

    def block(masked):
        dist = (lax.broadcasted_iota(jnp.int32, (tq, tq), 0)
                - lax.broadcasted_iota(jnp.int32, (tq, tq), 1)).astype(F32) + ((qi - ki) * tq).astype(F32)
        q = (q_ref[...] * (DQK ** -0.5)).astype(BF16)
        k = k_ref[...].astype(BF16)
        v = v_ref[...].astype(BF16)
        for h in range(H_DIFF):
            bias = (-_alibi_slope(h)) * dist
            if masked:
                bias = jnp.where(dist >= 0, bias, -jnp.inf)
            vh = v[:, h * DV:(h + 1) * DV]
            for c in range(2):
                lo = h * 2 * DQK + c * DQK
                j = 2 * h + c
                s = _dot_t(q[:, lo:lo + DQK], k[:, lo:lo + DQK]) + bias
                m_prev = m_ref[j]
                m_new = jnp.maximum(m_prev, jnp.max(s, axis=-1, keepdims=True))
                alpha = jnp.exp(m_prev - m_new)
                p = jnp.exp(s - m_new)
                l_ref[j] = alpha * l_ref[j] + jnp.sum(p, axis=-1, keepdims=True)
                acc_ref[j] = alpha * acc_ref[j] + jnp.dot(p.astype(BF16), vh, preferred_element_type=F32)
                m_ref[j] = m_new

    @pl.when(ki < qi)
    def _():
        block(False)

    @pl.when(ki == qi)
    def _():
        block(True)
        lam = _lambda(lq1, lk1, lq2, lk2, lam_init)
        for h in range(H_DIFF):
            o = acc_ref[2 * h] / l_ref[2 * h] - lam * (acc_ref[2 * h + 1] / l_ref[2 * h + 1])
            sl = slice(h * DV, (h + 1) * DV)
            o_ref[:, sl] = _subln_gate(o, subg_ref[...], g_ref[:, sl], lam_init).astype(o_ref.dtype)


def _diff_prompt(p, lam_params, subg, nb, t, lam_init):
    tq = min(512, t)
    nq = t // tq
    vec = pl.BlockSpec((1, DQK), lambda b, i, j: (0, 0))

    def qmap(c):
        return lambda b, i, j: (b * nq + i, c)

    def kmap(c):
        return lambda b, i, j: (b * nq + jnp.minimum(i, j), c)

    return pl.pallas_call(
        functools.partial(_diff_prompt_kernel, tq=tq, lam_init=lam_init),
        grid=(nb, nq, nq),
        in_specs=[vec, vec, vec, vec, pl.BlockSpec((1, DV), lambda b, i, j: (0, 0)),
                  pl.BlockSpec((tq, BRANCH), qmap(C_DQ)),
                  pl.BlockSpec((tq, BRANCH), kmap(C_DK)),
                  pl.BlockSpec((tq, BRANCH), kmap(C_DV)),
                  pl.BlockSpec((tq, BRANCH), qmap(C_DG))],
        out_specs=pl.BlockSpec((tq, BRANCH), lambda b, i, j: (b * nq + i, 0)),
        out_shape=jax.ShapeDtypeStruct((nb * t, BRANCH), BF16),
        scratch_shapes=[pltpu.VMEM((2 * H_DIFF, tq, 1), F32), pltpu.VMEM((2 * H_DIFF, tq, 1), F32),
                        pltpu.VMEM((2 * H_DIFF, tq, DV), F32)],
        compiler_params=_cparams(("parallel", "parallel", "arbitrary")),
        name="diff_attn_prompt",
    )(*lam_params, subg.reshape(1, DV), p, p, p, p)


def _diff_decode_kernel(pt_ref, lq1, lk1, lq2, lk2, subg_ref, q_ref, kc_ref, vc_ref, g_ref, *rest,
                        n_pages, lam_init):
    kp = rest[:n_pages]
    vp = rest[n_pages:2 * n_pages]
    o_ref = rest[2 * n_pages]
    nrow = 2 * H_DIFF
    row = lax.broadcasted_iota(jnp.int32, (nrow, BRANCH), 0)
    lane = lax.broadcasted_iota(jnp.int32, (nrow, BRANCH), 1)
    q = q_ref[...] * (DQK ** -0.5)
    qm = jnp.where(lane // DQK == row, q, 0.0).astype(BF16)

    rowc = lax.broadcasted_iota(jnp.int32, (nrow, 1), 0)
    slope = jnp.zeros((nrow, 1), F32)
    for h in range(H_DIFF):
        slope = jnp.where(rowc // 2 == h, _alibi_slope(h), slope)

    past = n_pages * PAGE_SIZE
    s = jnp.concatenate([jnp.dot(qm, kp[j][...].astype(BF16), preferred_element_type=F32)
                         for j in range(n_pages)], axis=-1)
    dist = (past - lax.broadcasted_iota(jnp.int32, (nrow, past), 1)).astype(F32)
    s = s + (-slope) * dist
    kc = kc_ref[...].astype(BF16).astype(F32)
    vc = vc_ref[...].astype(BF16).astype(F32)
    sc = jnp.sum(qm.astype(F32) * kc, axis=-1, keepdims=True)
    m = jnp.maximum(jnp.max(s, axis=-1, keepdims=True), sc)
    p = jnp.exp(s - m)
    pc = jnp.exp(sc - m)
    l_fin = jnp.sum(p, axis=-1, keepdims=True) + pc
    pb = p.astype(BF16)
    pcb = pc.astype(BF16).astype(F32)
    lam = _lambda(lq1, lk1, lq2, lk2, lam_init)
    sign = jnp.where(rowc % 2 == 0, 1.0, -lam) / l_fin
    g = g_ref[...]
    outs = []
    for h in range(H_DIFF):
        hs = slice(h * DV, (h + 1) * DV)
        acc = pcb * vc[:, hs]
        for j in range(n_pages):
            vh = vp[j][pl.ds(h, PAGE_SIZE, stride=H_DIFF), :].astype(BF16)
            acc = acc + jnp.dot(pb[:, j * PAGE_SIZE:(j + 1) * PAGE_SIZE], vh, preferred_element_type=F32)
        o = jnp.sum(jnp.where(rowc // 2 == h, acc * sign, 0.0), axis=0, keepdims=True)
        outs.append(_subln_gate(o, subg_ref[...], g[:, hs], lam_init))
    o_ref[...] = jnp.concatenate(outs, axis=-1).astype(o_ref.dtype)


def _diff_decode(p3, layer, cache_k, cache_v, page_table, lam_params, subg, lam_init):
    nb, n_pages = page_table.shape
    vec = pl.BlockSpec((1, DQK), lambda b, pt: (0, 0))

    def row(c):
        return pl.BlockSpec((None, 1, BRANCH), lambda b, pt: (b, 0, c))

    def page(j, shape):
        return pl.BlockSpec((None, None) + shape, lambda b, pt: (layer, pt[b * n_pages + j], 0, 0))

    kpages = [page(j, (BRANCH, PAGE_SIZE)) for j in range(n_pages)]
    vpages = [page(j, (PAGE_SIZE * H_DIFF, DV)) for j in range(n_pages)]
    return pl.pallas_call(
        functools.partial(_diff_decode_kernel, n_pages=n_pages, lam_init=lam_init),
        grid_spec=pltpu.PrefetchScalarGridSpec(
            num_scalar_prefetch=1,
            grid=(nb,),
            in_specs=[vec, vec, vec, vec, pl.BlockSpec((1, DV), lambda b, pt: (0, 0)),
                      row(C_DQ), row(C_DK), row(C_DV), row(C_DG)] + kpages + vpages,
            out_specs=pl.BlockSpec((None, 1, BRANCH), lambda b, pt: (b, 0, 0))),
        out_shape=jax.ShapeDtypeStruct((nb, 1, BRANCH), BF16),
        compiler_params=_cparams(("parallel",)),
        name="diff_attn_decode",
    )(page_table.reshape(-1), *lam_params, subg.reshape(1, DV), p3, p3, p3, p3,
      *([cache_k] * n_pages), *([cache_v] * n_pages))


def _seg_sum(x, bd_ref):
    return jnp.dot(x, bd_ref[...], preferred_element_type=F32, precision=HIGHEST)


def _rwkv_pre_kernel(*refs, shift, has_vres, tm):
    it = iter(refs)
    cur = [next(it) for _ in range(4)]
    prev = None if shift else [next(it) for _ in range(4)]
    mu = [next(it) for _ in range(4)]
    w0, wup, a0, aup, kk_ref, ka_ref, rk_ref, bd_ref = [next(it) for _ in range(8)]
    if has_vres:
        vf_ref, vr0, vra, vrb = [next(it) for _ in range(4)]
    outs = [next(it) for _ in range(7)]
    carry = [next(it) for _ in range(4)] if shift else None

    xs = []
    for n in range(4):
        c = cur[n][...]
        if shift:
            @pl.when(pl.program_id(1) == 0)
            def _():
                carry[n][...] = jnp.zeros(carry[n].shape, F32)
            rolled = pltpu.roll(c, 1, 0)
            rows = lax.broadcasted_iota(jnp.int32, c.shape, 0)
            pv = jnp.where(rows == 0, carry[n][...], rolled)
            carry[n][...] = c[tm - 1:tm, :]
        else:
            pv = prev[n][...]
        xs.append(c + (pv - c) * mu[n][...])
    xr, xk, xv, xl = xs

    hw = jnp.dot(jnp.tanh(xl).astype(BF16), wup[...], preferred_element_type=F32)
    w = -DECAY_SCALE * jax.nn.sigmoid(w0[...] + hw)
    if not shift:
        w = jnp.exp(w)
    a = jax.nn.sigmoid(a0[...] + jnp.dot(xl.astype(BF16), aup[...], preferred_element_type=F32))
    if has_vres:
        t1 = jnp.dot(xv.astype(BF16), vra[...], preferred_element_type=F32)
        t2 = jnp.dot(t1.astype(BF16), vrb[...], preferred_element_type=F32)
        xv = xv + (vf_ref[...] - xv) * jax.nn.sigmoid(vr0[...] + t2)
    kk = xk * kk_ref[...]
    kk = kk / jnp.maximum(jnp.sqrt(_seg_sum(kk * kk, bd_ref)), 1e-12)
    k2 = xk * (1.0 + (a - 1.0) * ka_ref[...])
    bonus = _seg_sum(xr * k2 * rk_ref[...], bd_ref) * xv
    for o, val in zip(outs, (xr, w, k2, xv, -kk, kk * a, bonus)):
        o[...] = val


def _rwkv_pre(p, prev, lw, vfirst, nb, t, tm):
    shift = prev is None
    has_vres = vfirst is not None
    nt = t // tm
    if shift:
        grid = (nb, nt)
        rmap = lambda c: (lambda b, i: (b * nt + i, c))
        cmap = lambda b, i: (0, 0)
        sem = ("parallel", "arbitrary")
    else:
        grid = (nt,)
        rmap = lambda c: (lambda i: (i, c))
        cmap = lambda i: (0, 0)
        sem = ("parallel",)
    wide = lambda c: pl.BlockSpec((tm, BRANCH), rmap(c))
    vec = lambda n: pl.BlockSpec((1, n), cmap)
    args, specs = [], []
    args += [p, p, p, p]
    specs += [wide(C_RR), wide(C_RK), wide(C_RV), pl.BlockSpec((tm, N_LORA), rmap(C_LORA128))]
    if not shift:
        args += [prev, prev, prev, prev]
        specs += [wide(0), wide(1), wide(2), pl.BlockSpec((tm, N_LORA), rmap(3 * BRANCH // N_LORA))]
    args += lw["mu"]
    specs += [vec(BRANCH)] * 3 + [vec(N_LORA)]
    args += [lw["w0"], lw["w_up"], lw["a0"], lw["a_up"], lw["k_k"], lw["k_a"], lw["r_k"], lw["bd"]]
    specs += [vec(BRANCH), pl.BlockSpec((N_LORA, BRANCH), cmap), vec(BRANCH), pl.BlockSpec((N_LORA, BRANCH), cmap),
              vec(BRANCH), vec(BRANCH), vec(BRANCH), pl.BlockSpec((BRANCH, BRANCH), cmap)]
    if has_vres:
        args += [vfirst, lw["vres_0"], lw["vres_a"], lw["vres_b"]]
        specs += [pl.BlockSpec((tm, BRANCH), rmap(0)), vec(BRANCH),
                  pl.BlockSpec(lw["vres_a"].shape, cmap), pl.BlockSpec(lw["vres_b"].shape, cmap)]
    out_spec = pl.BlockSpec((tm, BRANCH), rmap(0))
    scratch = [pltpu.VMEM((1, BRANCH), F32)] * 3 + [pltpu.VMEM((1, N_LORA), F32)] if shift else []
    return pl.pallas_call(
        functools.partial(_rwkv_pre_kernel, shift=shift, has_vres=has_vres, tm=tm),
        grid=grid, in_specs=specs, out_specs=[out_spec] * 7,
        out_shape=[jax.ShapeDtypeStruct((nb * t, BRANCH), F32)] * 7,
        scratch_shapes=scratch,
        compiler_params=_cparams(sem),
        name="rwkv_pre",
    )(*args)


RWKV_CHUNK = 64

_NN = ((1,), (0,))
_NT = ((1,), (1,))
_TN = ((0,), (0,))


def _split(x):
    hi = x.astype(BF16)
    return hi, (x - hi.astype(F32)).astype(BF16)


def _mm3(a, b, dims):
    dg = lambda x, y: lax.dot_general(x, y, (dims, ((), ())), preferred_element_type=F32)
    return dg(a[0], b[0]) + (dg(a[0], b[1]) + dg(a[1], b[0]))


def _rwkv_scan_kernel(r_ref, lw_ref, k_ref, v_ref, a_ref, b_ref, y_ref, s_ref):
    C = RWKV_CHUNK

    @pl.when(pl.program_id(1) == 0)
    def _():
        s_ref[...] = jnp.zeros(s_ref.shape, F32)

    rowi = lax.broadcasted_iota(jnp.int32, (C, C), 0)
    coli = lax.broadcasted_iota(jnp.int32, (C, C), 1)
    incl = rowi >= coli
    strict = rowi > coli

    lw = lw_ref[...]
    tri = incl.astype(BF16)
    l1 = lw.astype(BF16)
    rem = lw - l1.astype(F32)
    l2 = rem.astype(BF16)
    l3 = (rem - l2.astype(F32)).astype(BF16)
    cs = (jnp.dot(tri, l1, preferred_element_type=F32) + jnp.dot(tri, l2, preferred_element_type=F32)
          + jnp.dot(tri, l3, preferred_element_type=F32))
    p_in = jnp.exp(cs)
    p_inv = jnp.exp(-cs)
    rt = r_ref[...] * p_in
    at = a_ref[...] * jnp.exp(cs - lw)
    bt = b_ref[...] * p_inv
    kt = k_ref[...] * p_inv
    v = v_ref[...]
    p_last = p_in[C - 1:C, :]

    heads = range(H_RWKV)
    sls = [slice(h * R_HEAD, (h + 1) * R_HEAD) for h in heads]
    ops = [[_split(z[:, sl]) for z in (at, rt, bt, kt, v)] for sl in sls]
    s0 = [s_ref[h] for h in heads]
    s0_s = [_split(z) for z in s0]
    n_ab = [jnp.where(strict, _mm3(o[0], o[2], _NT), 0.0) for o in ops]
    a_ak = [jnp.where(strict, _mm3(o[0], o[3], _NT), 0.0) for o in ops]
    a_rb = [_split(jnp.where(incl, _mm3(o[1], o[2], _NT), 0.0)) for o in ops]
    a_rk = [_split(jnp.where(incl, _mm3(o[1], o[3], _NT), 0.0)) for o in ops]
    rhs = [_mm3(ops[h][0], s0_s[h], _NT) + _mm3(_split(a_ak[h]), ops[h][4], _NN) for h in heads]
    m_s = [_split(z) for z in n_ab]
    u = [rhs[h] + _mm3(m_s[h], _split(rhs[h]), _NN) for h in heads]
    for _ in range(int(math.log2(C)) - 1):
        m_s = [_split(_mm3(z, z, _NN)) for z in m_s]
        u = [u[h] + _mm3(m_s[h], _split(u[h]), _NN) for h in heads]
    u_s = [_split(z) for z in u]
    for h in heads:
        y_ref[:, sls[h]] = (_mm3(ops[h][1], s0_s[h], _NT) + _mm3(a_rb[h], u_s[h], _NN)
                            + _mm3(a_rk[h], ops[h][4], _NN))
        s_ref[h] = (s0[h] + _mm3(u_s[h], ops[h][2], _TN) + _mm3(ops[h][4], ops[h][3], _TN)) * p_last[:, sls[h]]


def _rwkv_scan(seqs, nb, t):
    tc = RWKV_CHUNK
    blk = pl.BlockSpec((None, tc, BRANCH), lambda b, c: (b, c, 0))
    return pl.pallas_call(
        _rwkv_scan_kernel,
        grid=(nb, t // tc),
        in_specs=[blk] * 6,
        out_specs=[blk, pl.BlockSpec((H_RWKV, R_HEAD, R_HEAD), lambda b, c: (b, 0, 0))],
        out_shape=[jax.ShapeDtypeStruct((nb, t, BRANCH), F32),
                   jax.ShapeDtypeStruct((nb * H_RWKV, R_HEAD, R_HEAD), F32)],
        compiler_params=_cparams(("parallel", "arbitrary")),
        name="rwkv_scan",
    )(*seqs)


def _rwkv_step_kernel(r_ref, w_ref, k_ref, a_ref, b_ref, vt_ref, s_ref, yt_ref, so_ref, *, bb):
    units = [(j, slice(h * R_HEAD, (h + 1) * R_HEAD)) for j in range(bb) for h in range(H_RWKV)]
    row = lambda ref, j, sl: ref[j:j + 1, sl]
    st = [s_ref[j, sl, :] for j, sl in units]
    sa = [jnp.sum(z * row(a_ref, j, sl), axis=-1, keepdims=True) for z, (j, sl) in zip(st, units)]
    st = [z * row(w_ref, j, sl) + x * row(b_ref, j, sl) + vt_ref[sl, j:j + 1] * row(k_ref, j, sl)
          for z, x, (j, sl) in zip(st, sa, units)]
    for z, (j, sl) in zip(st, units):
        so_ref[j, sl, :] = z
    ys = [jnp.sum(z * row(r_ref, j, sl), axis=-1, keepdims=True) for z, (j, sl) in zip(st, units)]
    for y, (j, sl) in zip(ys, units):
        yt_ref[sl, j:j + 1] = y


def _rwkv_step(r, w, k, na, b, vt, state, bb=8):
    n = r.shape[0]
    rowb = pl.BlockSpec((bb, BRANCH), lambda i: (i, 0))
    colb = pl.BlockSpec((None, BRANCH, bb), lambda i: (i, 0, 0))
    stb = pl.BlockSpec((bb, BRANCH, R_HEAD), lambda i: (i, 0, 0))
    return pl.pallas_call(
        functools.partial(_rwkv_step_kernel, bb=bb),
        grid=(n // bb,),
        in_specs=[rowb] * 5 + [colb, stb],
        out_specs=[colb, stb],
        out_shape=[jax.ShapeDtypeStruct((n // bb, BRANCH, bb), F32),
                   jax.ShapeDtypeStruct((n, BRANCH, R_HEAD), F32)],
        compiler_params=_cparams(("parallel",)),
        name="rwkv_step",
    )(r, w, k, na, b, vt, state)


def _rwkv_post_kernel(y_ref, bonus_ref, g_ref, lg_ref, lb_ref, bd_ref, o_ref):
    y = y_ref[...]
    mu = _seg_sum(y, bd_ref) * (1.0 / R_HEAD)
    d = y - mu
    var = _seg_sum(d * d, bd_ref) * (1.0 / R_HEAD)
    o = d * lax.rsqrt(var + LNX_EPS) * lg_ref[...] + lb_ref[...] + bonus_ref[...]
    o_ref[...] = (o * _silu(g_ref[...])).astype(o_ref.dtype)


def _rwkv_post(y, bonus, p, lw, tm):
    m = y.shape[0]
    blk = pl.BlockSpec((tm, BRANCH), lambda i: (i, 0))
    vec = pl.BlockSpec((1, BRANCH), lambda i: (0, 0))
    return pl.pallas_call(
        _rwkv_post_kernel,
        grid=(m // tm,),
        in_specs=[blk, blk, pl.BlockSpec((tm, BRANCH), lambda i: (i, C_RG)), vec, vec,
                  pl.BlockSpec((BRANCH, BRANCH), lambda i: (0, 0))],
        out_specs=blk,
        out_shape=jax.ShapeDtypeStruct((m, BRANCH), BF16),
        compiler_params=_cparams(("parallel",)),
        name="rwkv_post",
    )(y, bonus, p, lw["lnx_g"], lw["lnx_b"], lw["bd"])


def _softplus(x):
    return jnp.maximum(x, 0.0) + jnp.log1p(jnp.exp(-jnp.abs(x)))


def _neg_exp_alog(alog_ref):
    lanes = lax.broadcasted_iota(jnp.int32, alog_ref.shape, 1)
    return jnp.where(lanes < H_MAMBA, -jnp.exp(alog_ref[...]), 0.0)


def _gated_norm(ym, z, g):
    u = ym * _silu(z)
    ms = jnp.mean(u * u, axis=-1, keepdims=True)
    return u * lax.rsqrt(ms + RMS_EPS) * g


def _mamba_prompt_kernel(mx_ref, mbc_ref, mdt_ref, mz_ref, cwx_ref, cwb_ref, cbx_ref, cbb_ref, dtb_ref, alog_ref,
                         dskip_ref, gn_ref, o_ref, s_ref, extx_ref, extb_ref, ym_ref):
    L = SSD_CHUNK
    pad = 8

    @pl.when(pl.program_id(1) == 0)
    def _():
        extx_ref[0:pad, :] = jnp.zeros((pad, BRANCH), F32)
        extb_ref[0:pad, :] = jnp.zeros((pad, BRANCH), F32)
        s_ref[...] = jnp.zeros(s_ref.shape, F32)

    def conv(ext_ref, cur_ref, cw_ref, cb_ref):
        ext_ref[pad:pad + L, :] = cur_ref[...]
        acc = cb_ref[...]
        for j in range(M_CONV):
            lo = pad - (M_CONV - 1) + j
            acc = acc + ext_ref[lo:lo + L, :] * cw_ref[j:j + 1, :]
        ext_ref[0:pad, :] = ext_ref[L:L + pad, :]
        return _silu(acc)

    x = conv(extx_ref, mx_ref, cwx_ref, cbx_ref)
    bc = conv(extb_ref, mbc_ref, cwb_ref, cbb_ref)
    dt = _softplus(mdt_ref[...] + dtb_ref[...])
    adt = dt * _neg_exp_alog(alog_ref)
    rowi = lax.broadcasted_iota(jnp.int32, (L, L), 0)
    coli = lax.broadcasted_iota(jnp.int32, (L, L), 1)
    lower = rowi >= coli
    acs = jnp.dot(lower.astype(F32), adt, preferred_element_type=F32, precision=HIGHEST)
    acs_t = acs.T
    last = acs[L - 1:L, :]
    dstate = jnp.exp(last - acs)
    eacs = jnp.exp(acs)
    elast = jnp.exp(last)
    bcb = bc.astype(BF16)
    grp = []
    for g in range(M_GROUPS):
        bg = bcb[:, g * M_STATE:(g + 1) * M_STATE]
        cg = bcb[:, (M_GROUPS + g) * M_STATE:(M_GROUPS + g + 1) * M_STATE]
        grp.append((bg, cg, _dot_t(cg, bg)))
    for h in range(H_MAMBA):
        bg, cg, cb = grp[h // (H_MAMBA // M_GROUPS)]
        sl = slice(h * M_HEADDIM, (h + 1) * M_HEADDIM)
        decay = jnp.where(lower, jnp.exp(acs[:, h:h + 1] - acs_t[h:h + 1, :]), 0.0)
        xh = x[:, sl]
        xdt = xh * dt[:, h:h + 1]
        yd = jnp.dot((cb * decay).astype(BF16), xdt.astype(BF16), preferred_element_type=F32)
        st = s_ref[h]
        yo = _dot_t(cg, st.astype(BF16)) * eacs[:, h:h + 1]
        upd = lax.dot_general((xdt * dstate[:, h:h + 1]).astype(BF16), bg, (((0,), (0,)), ((), ())),
                              preferred_element_type=F32)
        s_ref[h] = st * elast[:, h:h + 1] + upd
        ym_ref[:, sl] = yd + yo + dskip_ref[:, sl] * xh
    o_ref[...] = _gated_norm(ym_ref[...], mz_ref[...], gn_ref[...]).astype(o_ref.dtype)


def _mamba_prompt(p, lw, nb, t):
    L = SSD_CHUNK
    nt = t // L
    rmap = lambda c: (lambda b, i: (b * nt + i, c))
    cmap = lambda b, i: (0, 0)
    wide = lambda c: pl.BlockSpec((L, BRANCH), rmap(c))
    vec = lambda n: pl.BlockSpec((1, n), cmap)
    cw = pl.BlockSpec((M_CONV, BRANCH), cmap)
    return pl.pallas_call(
        _mamba_prompt_kernel,
        grid=(nb, nt),
        in_specs=[wide(C_MX), wide(C_MBC), pl.BlockSpec((L, 128), rmap(C_DT128)), wide(C_MZ),
                  cw, cw, vec(BRANCH), vec(BRANCH), vec(128), vec(128), vec(BRANCH), vec(BRANCH)],
        out_specs=[pl.BlockSpec((L, BRANCH), rmap(0)),
                   pl.BlockSpec((None, H_MAMBA, M_HEADDIM, M_STATE), lambda b, i: (b, 0, 0, 0))],
        out_shape=[jax.ShapeDtypeStruct((nb * t, BRANCH), BF16),
                   jax.ShapeDtypeStruct((nb, H_MAMBA, M_HEADDIM, M_STATE), F32)],
        scratch_shapes=[pltpu.VMEM((L + 8, BRANCH), F32), pltpu.VMEM((L + 8, BRANCH), F32),
                        pltpu.VMEM((L, BRANCH), F32)],
        compiler_params=_cparams(("parallel", "arbitrary")),
        name="mamba_prompt",
    )(p, p, p, p, lw["cw_x"], lw["cw_bc"], lw["cb_x"], lw["cb_bc"], lw["dt_bias"], lw["a_log"],
      lw["d_skip"], lw["gnorm_g"])


def _mamba_dec_pre_kernel(mx_ref, mbc_ref, mdt_ref, sx_ref, sb_ref, cwx_ref, cwb_ref, cbx_ref, cbb_ref,
                          dtb_ref, alog_ref, x_ref, bc_ref, xdt_ref, da_ref):
    def conv(st_ref, cur_ref, cw_ref, cb_ref):
        acc = cb_ref[...]
        for j in range(M_CONV - 1):
            acc = acc + st_ref[j] * cw_ref[j:j + 1, :]
        acc = acc + cur_ref[...] * cw_ref[M_CONV - 1:M_CONV, :]
        return _silu(acc)

    x = conv(sx_ref, mx_ref, cwx_ref, cbx_ref)
    x_ref[...] = x
    bc_ref[...] = conv(sb_ref, mbc_ref, cwb_ref, cbb_ref)
    dt = _softplus(mdt_ref[...] + dtb_ref[...])
    da = jnp.exp(dt * _neg_exp_alog(alog_ref))
    n = x.shape[0]
    for h in range(H_MAMBA):
        sl = slice(h * M_HEADDIM, (h + 1) * M_HEADDIM)
        xdt_ref[:, sl] = x[:, sl] * dt[:, h:h + 1]
        da_ref[:, sl] = jnp.broadcast_to(da[:, h:h + 1], (n, M_HEADDIM))


def _mamba_dec_pre(p, conv_x, conv_bc, lw):
    n = p.shape[0]
    cmap = lambda i: (0, 0)
    wide = lambda c: pl.BlockSpec((n, BRANCH), lambda i: (0, c))
    vec = lambda m: pl.BlockSpec((1, m), cmap)
    cw = pl.BlockSpec((M_CONV, BRANCH), cmap)
    st = pl.BlockSpec((M_CONV - 1, n, BRANCH), lambda i: (0, 0, 0))
    out = pl.BlockSpec((n, BRANCH), cmap)
    return pl.pallas_call(
        _mamba_dec_pre_kernel,
        grid=(1,),
        in_specs=[wide(C_MX), wide(C_MBC), pl.BlockSpec((n, 128), lambda i: (0, C_DT128)), st, st,
                  cw, cw, vec(BRANCH), vec(BRANCH), vec(128), vec(128)],
        out_specs=[out] * 4,
        out_shape=[jax.ShapeDtypeStruct((n, BRANCH), F32)] * 4,
        compiler_params=_cparams(("arbitrary",)),
        name="mamba_dec_pre",
    )(p, p, p, conv_x, conv_bc, lw["cw_x"], lw["cw_bc"], lw["cb_x"], lw["cb_bc"], lw["dt_bias"], lw["a_log"])


def _mamba_step_kernel(bc_ref, xdtt_ref, dat_ref, s_ref, yt_ref, so_ref, *, bb):
    units = [(j, h // (H_MAMBA // M_GROUPS), slice(h * M_HEADDIM, (h + 1) * M_HEADDIM))
             for j in range(bb) for h in range(H_MAMBA)]
    st = []
    for j, g, sl in units:
        bg = bc_ref[j:j + 1, g * M_STATE:(g + 1) * M_STATE]
        z = s_ref[j, sl, :] * dat_ref[sl, j:j + 1] + xdtt_ref[sl, j:j + 1] * bg
        so_ref[j, sl, :] = z
        st.append(z)
    ys = [jnp.sum(z * bc_ref[j:j + 1, (M_GROUPS + g) * M_STATE:(M_GROUPS + g + 1) * M_STATE], axis=-1, keepdims=True)
          for z, (j, g, sl) in zip(st, units)]
    for y, (j, g, sl) in zip(ys, units):
        yt_ref[sl, j:j + 1] = y


def _mamba_step(bc, xdtt, dat, state, bb=8):
    n = bc.shape[0]
    rowb = pl.BlockSpec((bb, BRANCH), lambda i: (i, 0))
    colb = pl.BlockSpec((None, BRANCH, bb), lambda i: (i, 0, 0))
    stb = pl.BlockSpec((bb, BRANCH, M_STATE), lambda i: (i, 0, 0))
    return pl.pallas_call(
        functools.partial(_mamba_step_kernel, bb=bb),
        grid=(n // bb,),
        in_specs=[rowb, colb, colb, stb],
        out_specs=[colb, stb],
        out_shape=[jax.ShapeDtypeStruct((n // bb, BRANCH, bb), F32),
                   jax.ShapeDtypeStruct((n, BRANCH, M_STATE), F32)],
        compiler_params=_cparams(("parallel",)),
        name="mamba_step",
    )(bc, xdtt, dat, state)


def _mamba_dec_post_kernel(y_ref, x_ref, z_ref, dskip_ref, gn_ref, o_ref):
    ym = y_ref[...] + dskip_ref[...] * x_ref[...]
    o_ref[...] = _gated_norm(ym, z_ref[...], gn_ref[...]).astype(o_ref.dtype)


def _mamba_dec_post(y, x, p, lw):
    n = y.shape[0]
    blk = pl.BlockSpec((n, BRANCH), lambda i: (0, 0))
    vec = pl.BlockSpec((1, BRANCH), lambda i: (0, 0))
    return pl.pallas_call(
        _mamba_dec_post_kernel,
        grid=(1,),
        in_specs=[blk, blk, pl.BlockSpec((n, BRANCH), lambda i: (0, C_MZ)), vec, vec],
        out_specs=blk,
        out_shape=jax.ShapeDtypeStruct((n, BRANCH), BF16),
        compiler_params=_cparams(("arbitrary",)),
        name="mamba_dec_post",
    )(y, x, p, lw["d_skip"], lw["gnorm_g"])


def _xattn_prompt_kernel(q_ref, g_ref, mk_ref, mv_ref, o_ref):
    q = q_ref[...].astype(BF16)
    mk = mk_ref[...].astype(BF16)
    mv = mv_ref[...].astype(BF16)
    for h in range(H_X):
        sl = slice(h * D_X, (h + 1) * D_X)
        s = _dot_t(q[:, sl], mk[:, sl]) * (D_X ** -0.5)
        e = jnp.exp(s - jnp.max(s, axis=-1, keepdims=True))
        o = jnp.dot(e.astype(BF16), mv[:, sl], preferred_element_type=F32) / jnp.sum(e, axis=-1, keepdims=True)
        o_ref[:, sl] = (o * _silu(g_ref[:, sl])).astype(o_ref.dtype)


def _xattn_prompt(p, mkv, nb, t):
    tq = min(512, t)
    nq = t // tq
    return pl.pallas_call(
        _xattn_prompt_kernel,
        grid=(nb, nq),
        in_specs=[pl.BlockSpec((tq, BRANCH), lambda b, i: (b * nq + i, C_XQ)),
                  pl.BlockSpec((tq, BRANCH), lambda b, i: (b * nq + i, C_XG)),
                  pl.BlockSpec((N_MEM, BRANCH), lambda b, i: (b, 0)),
                  pl.BlockSpec((N_MEM, BRANCH), lambda b, i: (b, 1))],
        out_specs=pl.BlockSpec((tq, BRANCH), lambda b, i: (b * nq + i, 0)),
        out_shape=jax.ShapeDtypeStruct((nb * t, BRANCH), BF16),
        compiler_params=_cparams(("parallel", "parallel")),
        name="xattn_prompt",
    )(p, p, mkv, mkv)


def _xattn_decode_kernel(q_ref, g_ref, mk_ref, mv_ref, o_ref):
    nrow = 8
    q = q_ref[...].astype(BF16)
    heads = range(H_X)
    s = [_dot_t(jnp.broadcast_to(q[:, h * D_X:(h + 1) * D_X], (nrow, D_X)),
                mk_ref[pl.ds(h, N_MEM, stride=H_X), :].astype(BF16)) * (D_X ** -0.5) for h in heads]
    e = [jnp.exp(z - jnp.max(z, axis=-1, keepdims=True)) for z in s]
    r = [jnp.dot(e[h].astype(BF16), mv_ref[pl.ds(h, N_MEM, stride=H_X), :].astype(BF16),
                 preferred_element_type=F32) for h in heads]
    outs = [(r[h] / jnp.sum(e[h], axis=-1, keepdims=True))[0:1, :] for h in heads]
    o_ref[...] = (jnp.concatenate(outs, axis=-1) * _silu(g_ref[...])).astype(o_ref.dtype)


def _xattn_decode(p3, layer, mem_k, mem_v):
    n = p3.shape[0]
    mem = pl.BlockSpec((None, None, N_MEM * H_X, D_X), lambda b: (layer, b, 0, 0))
    return pl.pallas_call(
        _xattn_decode_kernel,
        grid=(n,),
        in_specs=[pl.BlockSpec((None, 1, BRANCH), lambda b: (b, 0, C_XQ)),
                  pl.BlockSpec((None, 1, BRANCH), lambda b: (b, 0, C_XG)), mem, mem],
        out_specs=pl.BlockSpec((None, 1, BRANCH), lambda b: (b, 0, 0)),
        out_shape=jax.ShapeDtypeStruct((n, 1, BRANCH), BF16),
        compiler_params=_cparams(("parallel",)),
        name="xattn_decode",
    )(p3, p3, mem_k, mem_v)


def _layer_weights(l, prm):
    w = prm["w_in"][l]
    o_rs, o_rg, o_mz, o_mxbc, o_dt, o_xq, o_xg = 2048, 3712, 4224, 4736, 5760, 5768, 6280
    w_perm = jnp.concatenate([
        w[:, :o_rs + 3 * BRANCH], w[:, o_rg:o_mz], w[:, o_mz:o_mxbc], w[:, o_xq:o_xg], w[:, o_xg:],
        w[:, o_mxbc:o_dt], w[:, o_rs + 3 * BRANCH:o_rg], w[:, o_dt:o_xq],
        jnp.zeros((D_MODEL, 128 - H_MAMBA), w.dtype)], axis=1).astype(BF16)
    row = lambda v: v.reshape(1, -1).astype(F32)
    mu = prm["shift_mu"][l].astype(F32)
    seg = jnp.arange(BRANCH) // R_HEAD
    zl = jnp.zeros((N_LORA // 2, BRANCH), F32)
    cw = prm["conv_w"][l].astype(F32)
    cb = prm["conv_b"][l].astype(F32)
    pad8 = lambda v: jnp.pad(v.astype(F32), (0, 128 - H_MAMBA)).reshape(1, 128)
    lw = {
        "w_in": w_perm, "norm_g": prm["norm_g"][l], "w_out": prm["w_out"][l].astype(BF16),
        "lam": [row(prm[n][l]) for n in ("lam_q1", "lam_k1", "lam_q2", "lam_k2")],
        "subln_g": prm["subln_g"][l].astype(F32),
        "mu": [row(mu[:BRANCH]), row(mu[BRANCH:2 * BRANCH]), row(mu[2 * BRANCH:3 * BRANCH]), row(mu[3 * BRANCH:])],
        "w0": row(prm["w0"][l]), "a0": row(prm["a0"][l]),
        "w_up": jnp.concatenate([prm["w_up"][l].astype(F32), zl], axis=0).astype(BF16),
        "a_up": jnp.concatenate([zl, prm["a_up"][l].astype(F32)], axis=0).astype(BF16),
        "k_k": row(prm["k_k"][l]), "k_a": row(prm["k_a"][l]), "r_k": row(prm["r_k"][l]),
        "lnx_g": row(prm["lnx_g"][l]), "lnx_b": row(prm["lnx_b"][l]),
        "bd": (seg[:, None] == seg[None, :]).astype(F32),
        "cw_x": cw[:, :BRANCH], "cw_bc": cw[:, BRANCH:], "cb_x": row(cb[:BRANCH]), "cb_bc": row(cb[BRANCH:]),
        "dt_bias": pad8(prm["dt_bias"][l]), "a_log": pad8(prm["a_log"][l]),
        "d_skip": row(jnp.repeat(prm["d_skip"][l].astype(F32), M_HEADDIM)),
        "gnorm_g": row(prm["gnorm_g"][l]),
        "w_mkv": jnp.concatenate([prm["w_mk"][l], prm["w_mv"][l]], axis=1).astype(BF16),
        "mem_norm_g": prm["mem_norm_g"][l],
    }
    if l > 0:
        lw["vres_0"] = row(prm["vres_0"][l - 1])
        lw["vres_a"] = prm["vres_a"][l - 1].astype(BF16)
        lw["vres_b"] = prm["vres_b"][l - 1].astype(BF16)
    return lw


def _lam_init(l):
    return 0.8 - 0.6 * math.exp(-0.3 * l)


def _prompt_layer(x, l, lw, mem, vfirst, nb, t, fg, final, tm_proj, tm_row):
    p = _rms_matmul(x, lw["norm_g"], lw["w_in"], tm_proj, 1152)
    mkv = _rms_matmul(mem, lw["mem_norm_g"], lw["w_mkv"], mem.shape[0], 2 * BRANCH)
    out_a = _diff_prompt(p, lw["lam"], lw["subln_g"], nb, t, _lam_init(l))
    r, w, k, v, na, b, bonus = _rwkv_pre(p, None, lw, vfirst, nb, t, tm_row)
    seqs = [z.reshape(nb, t, BRANCH) for z in (r, w, k, v, na, b)]
    y, rw_state = _rwkv_scan(seqs, nb, t)
    out_b = _rwkv_post(y.reshape(nb * t, BRANCH), bonus, p, lw, tm_row)
    out_m, ssm_state = _mamba_prompt(p, lw, nb, t)
    out_x = _xattn_prompt(p, mkv, nb, t)
    x = _outproj((out_a, out_b, out_m, out_x), lw["w_out"], x, fg, tm_row, final)
    return x, p, mkv, rw_state, ssm_state, (v if vfirst is None else vfirst)


def _decode_layer(x, l, lw, caches, vfirst, fg, final):
    cache_k, cache_v, mem_k, mem_v, page_table, rw_prev, rw_state, conv_state, ssm_state = caches
    n = x.shape[0]
    bb = 8
    p = _rms_matmul(x, lw["norm_g"], lw["w_in"], n, 1152)
    p3 = p.reshape(n, 1, N_PROJ)
    out_a = _diff_decode(p3, l, cache_k, cache_v, page_table, lw["lam"], lw["subln_g"], _lam_init(l)).reshape(n, BRANCH)

    r, w, k, v, na, b, bonus = _rwkv_pre(p, rw_prev, lw, vfirst, 1, n, n)
    to_cols = lambda z: z.reshape(n // bb, bb, BRANCH).transpose(0, 2, 1)
    from_cols = lambda z: z.transpose(0, 2, 1).reshape(n, BRANCH)
    yt, rw_new = _rwkv_step(r, w, k, na, b, to_cols(v), rw_state.reshape(n, BRANCH, R_HEAD), bb)
    out_b = _rwkv_post(from_cols(yt), bonus, p, lw, n)

    cs = conv_state.transpose(1, 0, 2)
    xm, bc, xdt, da = _mamba_dec_pre(p, cs[:, :, :BRANCH], cs[:, :, BRANCH:], lw)
    ymt, ssm_new = _mamba_step(bc, to_cols(xdt), to_cols(da), ssm_state.reshape(n, BRANCH, M_STATE), bb)
    out_m = _mamba_dec_post(from_cols(ymt), xm, p, lw)

    out_x = _xattn_decode(p3, l, mem_k, mem_v).reshape(n, BRANCH)
    x = _outproj((out_a, out_b, out_m, out_x), lw["w_out"], x, fg, n, final)
    return x, p, rw_new, ssm_new, (v if vfirst is None else vfirst)


def _cols(p, c, width=BRANCH):
    return p[..., c * width:(c + 1) * width]


def kernel(x_prompt, x_sample, cache_diff_k, cache_diff_v, cache_mem_k, cache_mem_v, state_rwkv_shift, state_rwkv, state_conv, state_ssm, page_table, mem_prompt, norm_g, w_in, w_out, lam_q1, lam_k1, lam_q2, lam_k2, subln_g, shift_mu, w0, w_up, a0, a_up, k_k, k_a, r_k, lnx_g, lnx_b, vres_0, vres_a, vres_b, conv_w, conv_b, dt_bias, a_log, d_skip, gnorm_g, mem_norm_g, w_mk, w_mv, final_g):
    prm = dict(norm_g=norm_g, w_in=w_in, w_out=w_out, lam_q1=lam_q1, lam_k1=lam_k1, lam_q2=lam_q2, lam_k2=lam_k2,
               subln_g=subln_g, shift_mu=shift_mu, w0=w0, w_up=w_up, a0=a0, a_up=a_up, k_k=k_k, k_a=k_a, r_k=r_k,
               lnx_g=lnx_g, lnx_b=lnx_b, vres_0=vres_0, vres_a=vres_a, vres_b=vres_b, conv_w=conv_w, conv_b=conv_b,
               dt_bias=dt_bias, a_log=a_log, d_skip=d_skip, gnorm_g=gnorm_g, mem_norm_g=mem_norm_g,
               w_mk=w_mk, w_mv=w_mv)
    depth = w_in.shape[0]
    nb, t, _ = x_prompt.shape
    n = x_sample.shape[0]
    n_pool = cache_diff_k.shape[1]
    ck = jnp.transpose(cache_diff_k, (0, 1, 3, 4, 5, 2)).reshape(depth, n_pool, BRANCH, PAGE_SIZE)
    cv = cache_diff_v.reshape(depth, n_pool, PAGE_SIZE * H_DIFF, DV)
    cmk = cache_mem_k.reshape(depth, n, N_MEM * H_X, D_X)
    cmv = cache_mem_v.reshape(depth, n, N_MEM * H_X, D_X)
    mem = mem_prompt.reshape(nb * N_MEM, D_MODEL)

    xp = x_prompt.reshape(nb * t, D_MODEL)
    xs = x_sample.reshape(n, D_MODEL)
    vf_p = vf_s = None
    outs = [[] for _ in range(14)]
    for l in range(depth):
        lw = _layer_weights(l, prm)
        final = l == depth - 1
        xp, p, mkv, rw_p, ssm_p, vf_p = _prompt_layer(xp, l, lw, mem, vf_p, nb, t, final_g, final,
                                                      min(1024, nb * t), min(512, t))
        p3 = p.reshape(nb, t, N_PROJ)
        last = p3[:, t - 1]
        outs[0].append(_cols(p3, C_DK).reshape(nb, t, H_DIFF, 2, DQK))
        outs[1].append(_cols(p3, C_DV).reshape(nb, t, H_DIFF, DV))
        outs[2].append(mkv[:, :BRANCH].reshape(nb, N_MEM, H_X, D_X))
        outs[3].append(mkv[:, BRANCH:].reshape(nb, N_MEM, H_X, D_X))
        outs[4].append(jnp.concatenate([_cols(last, C_RR), _cols(last, C_RK), _cols(last, C_RV),
                                        _cols(last, C_LORA128, 128)], axis=-1))
        outs[5].append(rw_p.reshape(nb, H_RWKV, R_HEAD, R_HEAD))
        tail = p3[:, t - (M_CONV - 1):]
        outs[6].append(jnp.concatenate([_cols(tail, C_MX), _cols(tail, C_MBC)], axis=-1))
        outs[7].append(ssm_p)

        caches = (ck, cv, cmk, cmv, page_table, state_rwkv_shift[l], state_rwkv[l], state_conv[l], state_ssm[l])
        xs, ps, rw_s, ssm_s, vf_s = _decode_layer(xs, l, lw, caches, vf_s, final_g, final)
        outs[8].append(_cols(ps, C_DK).reshape(n, 1, H_DIFF, 2, DQK))
        outs[9].append(_cols(ps, C_DV).reshape(n, 1, H_DIFF, DV))
        outs[10].append(jnp.concatenate([_cols(ps, C_RR), _cols(ps, C_RK), _cols(ps, C_RV),
                                         _cols(ps, C_LORA128, 128)], axis=-1))
        outs[11].append(rw_s.reshape(n, H_RWKV, R_HEAD, R_HEAD))
        outs[12].append(jnp.concatenate(
            [state_conv[l][:, 1:], jnp.concatenate([_cols(ps, C_MX), _cols(ps, C_MBC)], axis=-1)[:, None]], axis=1))
        outs[13].append(ssm_s.reshape(n, H_MAMBA, M_HEADDIM, M_STATE))

    return (xp.reshape(nb, t, D_MODEL), xs.reshape(n, 1, D_MODEL)) + tuple(jnp.stack(o) for o in outs)
```

```python
import functools
import math

import jax
import jax.numpy as jnp
from jax import lax
from jax.experimental import pallas as pl
from jax.experimental.pallas import tpu as pltpu

F32 = jnp.float32
BF16 = jnp.bfloat16
HIGHEST = lax.Precision.HIGHEST

D_MODEL = 2048
BRANCH = 512
H_DIFF = 4
DQK = 64
DV = 128
R_HEAD = 64
H_RWKV = 8
N_LORA = 128
N_SHIFT = 3 * BRANCH + N_LORA
DECAY_SCALE = 0.6065306597
LNX_EPS = 64e-5
M_HEADDIM = 64
H_MAMBA = 8
M_GROUPS = 2
M_STATE = 128
M_CONV = 4
SSD_CHUNK = 128
H_X = 4
D_X = 128
N_MEM = 256
RMS_EPS = 1e-5
PAGE_SIZE = 128

C_DQ, C_DK, C_DV, C_DG, C_RR, C_RK, C_RV, C_RG, C_MZ, C_XQ, C_XG, C_MX, C_MBC = range(13)
C_LORA128 = 13 * BRANCH // 128
C_DT128 = C_LORA128 + 1
N_PROJ = 13 * BRANCH + 2 * 128

VMEM_LIMIT = 56 * 1024 * 1024


def _cparams(sem):
    return pltpu.CompilerParams(dimension_semantics=sem, vmem_limit_bytes=VMEM_LIMIT)


def _silu(x):
    return x * jax.nn.sigmoid(x)


def _dot_t(a, b):
    return lax.dot_general(a, b, (((1,), (1,)), ((), ())), preferred_element_type=F32)


def _rms_matmul_kernel(x_ref, g_ref, w_ref, o_ref, h_ref):
    @pl.when(pl.program_id(1) == 0)
    def _():
        x = x_ref[...]
        ms = jnp.mean(x * x, axis=-1, keepdims=True)
        h_ref[...] = (x * lax.rsqrt(ms + RMS_EPS) * g_ref[...]).astype(BF16)

    o_ref[...] = jnp.dot(h_ref[...], w_ref[...], preferred_element_type=F32)


def _rms_matmul(x, g, w, tm, tn):
    m, k = x.shape
    n = w.shape[1]
    return pl.pallas_call(
        _rms_matmul_kernel,
        grid=(m // tm, n // tn),
        in_specs=[pl.BlockSpec((tm, k), lambda i, j: (i, 0)),
                  pl.BlockSpec((1, k), lambda i, j: (0, 0)),
                  pl.BlockSpec((k, tn), lambda i, j: (0, j))],
        out_specs=pl.BlockSpec((tm, tn), lambda i, j: (i, j)),
        out_shape=jax.ShapeDtypeStruct((m, n), F32),
        scratch_shapes=[pltpu.VMEM((tm, k), BF16)],
        compiler_params=_cparams(("parallel", "arbitrary")),
        name="rms_matmul",
    )(x, g.reshape(1, k), w)


def _outproj_kernel(a_ref, b_ref, m_ref, c_ref, w_ref, x_ref, fg_ref, o_ref, *, final):
    acc = x_ref[...]
    for g, r in enumerate((a_ref, b_ref, m_ref, c_ref)):
        acc = acc + jnp.dot(r[...], w_ref[g * BRANCH:(g + 1) * BRANCH, :], preferred_element_type=F32)
    if final:
        ms = jnp.mean(acc * acc, axis=-1, keepdims=True)
        acc = acc * lax.rsqrt(ms + RMS_EPS) * fg_ref[...]
    o_ref[...] = acc


def _outproj(parts, w, x, fg, tm, final):
    m = x.shape[0]
    part_spec = pl.BlockSpec((tm, BRANCH), lambda i: (i, 0))
    return pl.pallas_call(
        functools.partial(_outproj_kernel, final=final),
        grid=(m // tm,),
        in_specs=[part_spec] * 4 + [pl.BlockSpec((D_MODEL, D_MODEL), lambda i: (0, 0)),
                                    pl.BlockSpec((tm, D_MODEL), lambda i: (i, 0)),
                                    pl.BlockSpec((1, D_MODEL), lambda i: (0, 0))],
        out_specs=pl.BlockSpec((tm, D_MODEL), lambda i: (i, 0)),
        out_shape=jax.ShapeDtypeStruct((m, D_MODEL), F32),
        compiler_params=_cparams(("parallel",)),
        name="outproj",
    )(*parts, w, x, fg.reshape(1, D_MODEL))


def _alibi_slope(h):
    return 2.0 ** (-8.0 * (h + 1) / H_DIFF)


def _lambda(lq1, lk1, lq2, lk2, lam_init):
    s1 = jnp.sum(lq1[...] * lk1[...], axis=-1, keepdims=True)
    s2 = jnp.sum(lq2[...] * lk2[...], axis=-1, keepdims=True)
    return jnp.exp(s1) - jnp.exp(s2) + lam_init


def _subln_gate(o, subg, gate, lam_init):
    ms = jnp.mean(o * o, axis=-1, keepdims=True)
    o = (o * lax.rsqrt(ms + RMS_EPS) * subg) * (1.0 - lam_init)
    return o * _silu(gate)


def _diff_prompt_kernel(lq1, lk1, lq2, lk2, subg_ref, q_ref, k_ref, v_ref, g_ref, o_ref,
                        m_ref, l_ref, acc_ref, *, tq, lam_init):
    qi = pl.program_id(1)
    ki = pl.program_id(2)

    @pl.when(ki == 0)
    def _():
        m_ref[...] = jnp.full(m_ref.shape, -1e30, F32)
        l_ref[...] = jnp.zeros(l_ref.shape, F32)
        acc_ref[...] = jnp.zeros(acc_ref.shape, F32)

    def block(masked):
        lane = lax.broadcasted_iota(jnp.int32, (tq, 2 * DQK), 1)
        slot = lane % DQK
        rows = lax.broadcasted_iota(jnp.int32, (tq, 2 * DQK), 0)
        lo = (rows % 256).astype(F32)
        hi = (rows // 256).astype(F32)
        off = ((qi - ki) * tq).astype(F32)
        zero = jnp.zeros((tq, 2 * DQK), F32)
        fq = jnp.where(slot == 0, -lo, jnp.where(slot == 1, -256.0 * hi, jnp.where(
            slot == 2, 1.0, jnp.where(slot == 3, 256.0, jnp.where(slot == 4, -off, zero)))))
        fk = jnp.where(slot == 0, 1.0, jnp.where(slot == 1, 1.0, jnp.where(
            slot == 2, lo, jnp.where(slot == 3, hi, jnp.where(slot == 4, 1.0, zero))))).astype(BF16)
        first = lane < DQK
        if masked:
            causal = (lax.broadcasted_iota(jnp.int32, (tq, tq), 0) >= lax.broadcasted_iota(jnp.int32, (tq, tq), 1))
        q = (q_ref[...] * (DQK ** -0.5)).astype(BF16)
        k = k_ref[...].astype(BF16)
        v = v_ref[...].astype(BF16)
        for h in range(H_DIFF):
            hs = slice(h * DV, (h + 1) * DV)
            fqh = (fq * _alibi_slope(h)).astype(BF16)
            qh, kh, vh = q[:, hs], k[:, hs], v[:, hs]
            for c in range(2):
                j = 2 * h + c
                keep = first if c == 0 else jnp.logical_not(first)
                s = _dot_t(jnp.where(keep, qh, fqh), jnp.where(keep, kh, fk))
                if masked:
                    s = jnp.where(causal, s, -jnp.inf)
                m_prev = m_ref[j]
                m_new = jnp.maximum(m_prev, jnp.max(s, axis=-1, keepdims=True))
                alpha = jnp.exp(m_prev - m_new)
                p = jnp.exp(s - jnp.concatenate([m_new] * (tq // DV), axis=1))
                l_ref[j] = alpha * l_ref[j] + jnp.sum(p, axis=-1, keepdims=True)
                acc_ref[j] = alpha * acc_ref[j] + jnp.dot(p.astype(BF16), vh, preferred_element_type=F32)
                m_ref[j] = m_new

    @pl.when(ki < qi)
    def _():
        block(False)

    @pl.when(ki == qi)
    def _():
        block(True)
        lam = _lambda(lq1, lk1, lq2, lk2, lam_init)
        for h in range(H_DIFF):
            o = acc_ref[2 * h] / l_ref[2 * h] - lam * (acc_ref[2 * h + 1] / l_ref[2 * h + 1])
            sl = slice(h * DV, (h + 1) * DV)
            o_ref[:, sl] = _subln_gate(o, subg_ref[...], g_ref[:, sl], lam_init).astype(o_ref.dtype)


def _diff_prompt(p, lam_params, subg, nb, t, lam_init):
    tq = min(512, t)
    nq = t // tq
    vec = pl.BlockSpec((1, DQK), lambda b, i, j: (0, 0))

    def qmap(c):
        return lambda b, i, j: (b * nq + i, c)

    def kmap(c):
        return lambda b, i, j: (b * nq + jnp.minimum(i, j), c)

    return pl.pallas_call(
        functools.partial(_diff_prompt_kernel, tq=tq, lam_init=lam_init),
        grid=(nb, nq, nq),
        in_specs=[vec, vec, vec, vec, pl.BlockSpec((1, DV), lambda b, i, j: (0, 0)),
                  pl.BlockSpec((tq, BRANCH), qmap(C_DQ)),
                  pl.BlockSpec((tq, BRANCH), kmap(C_DK)),
                  pl.BlockSpec((tq, BRANCH), kmap(C_DV)),
                  pl.BlockSpec((tq, BRANCH), qmap(C_DG))],
        out_specs=pl.BlockSpec((tq, BRANCH), lambda b, i, j: (b * nq + i, 0)),
        out_shape=jax.ShapeDtypeStruct((nb * t, BRANCH), BF16),
        scratch_shapes=[pltpu.VMEM((2 * H_DIFF, tq, DV), F32)] * 3,
        compiler_params=_cparams(("parallel", "parallel", "arbitrary")),
        name="diff_attn_prompt",
    )(*lam_params, subg.reshape(1, DV), p, p, p, p)


def _diff_decode_kernel(pt_ref, lq1, lk1, lq2, lk2, subg_ref, q_ref, kc_ref, vc_ref, g_ref, *rest,
                        n_pages, lam_init):
    kp = rest[:n_pages]
    vp = rest[n_pages:2 * n_pages]
    o_ref = rest[2 * n_pages]
    nrow = 2 * H_DIFF
    row = lax.broadcasted_iota(jnp.int32, (nrow, BRANCH), 0)
    lane = lax.broadcasted_iota(jnp.int32, (nrow, BRANCH), 1)
    q = q_ref[...] * (DQK ** -0.5)
    qm = jnp.where(lane // DQK == row, q, 0.0).astype(BF16)

    rowc = lax.broadcasted_iota(jnp.int32, (nrow, 1), 0)
    slope = jnp.zeros((nrow, 1), F32)
    for h in range(H_DIFF):
        slope = jnp.where(rowc // 2 == h, _alibi_slope(h), slope)

    past = n_pages * PAGE_SIZE
    s = jnp.concatenate([jnp.dot(qm, kp[j][...].astype(BF16), preferred_element_type=F32)
                         for j in range(n_pages)], axis=-1)
    dist = (past - lax.broadcasted_iota(jnp.int32, (nrow, past), 1)).astype(F32)
    s = s + (-slope) * dist
    kc = kc_ref[...].astype(BF16).astype(F32)
    vc = vc_ref[...].astype(BF16).astype(F32)
    sc = jnp.sum(qm.astype(F32) * kc, axis=-1, keepdims=True)
    m = jnp.maximum(jnp.max(s, axis=-1, keepdims=True), sc)
    p = jnp.exp(s - m)
    pc = jnp.exp(sc - m)
    l_fin = jnp.sum(p, axis=-1, keepdims=True) + pc
    pb = p.astype(BF16)
    pcb = pc.astype(BF16).astype(F32)
    lam = _lambda(lq1, lk1, lq2, lk2, lam_init)
    sign = jnp.where(rowc % 2 == 0, 1.0, -lam) / l_fin
    g = g_ref[...]
    outs = []
    for h in range(H_DIFF):
        hs = slice(h * DV, (h + 1) * DV)
        acc = pcb * vc[:, hs]
        for j in range(n_pages):
            vh = vp[j][pl.ds(h, PAGE_SIZE, stride=H_DIFF), :].astype(BF16)
            acc = acc + jnp.dot(pb[:, j * PAGE_SIZE:(j + 1) * PAGE_SIZE], vh, preferred_element_type=F32)
        o = jnp.sum(jnp.where(rowc // 2 == h, acc * sign, 0.0), axis=0, keepdims=True)
        outs.append(_subln_gate(o, subg_ref[...], g[:, hs], lam_init))
    o_ref[...] = jnp.concatenate(outs, axis=-1).astype(o_ref.dtype)


def _diff_decode(p3, layer, cache_k, cache_v, page_table, lam_params, subg, lam_init):
    nb, n_pages = page_table.shape
    vec = pl.BlockSpec((1, DQK), lambda b, pt: (0, 0))

    def row(c):
        return pl.BlockSpec((None, 1, BRANCH), lambda b, pt: (b, 0, c))

    def page(j, shape):
        return pl.BlockSpec((None, None) + shape, lambda b, pt: (layer, pt[b * n_pages + j], 0, 0))

    kpages = [page(j, (BRANCH, PAGE_SIZE)) for j in range(n_pages)]
    vpages = [page(j, (PAGE_SIZE * H_DIFF, DV)) for j in range(n_pages)]
    return pl.pallas_call(
        functools.partial(_diff_decode_kernel, n_pages=n_pages, lam_init=lam_init),
        grid_spec=pltpu.PrefetchScalarGridSpec(
            num_scalar_prefetch=1,
            grid=(nb,),
            in_specs=[vec, vec, vec, vec, pl.BlockSpec((1, DV), lambda b, pt: (0, 0)),
                      row(C_DQ), row(C_DK), row(C_DV), row(C_DG)] + kpages + vpages,
            out_specs=pl.BlockSpec((None, 1, BRANCH), lambda b, pt: (b, 0, 0))),
        out_shape=jax.ShapeDtypeStruct((nb, 1, BRANCH), BF16),
        compiler_params=_cparams(("parallel",)),
        name="diff_attn_decode",
    )(page_table.reshape(-1), *lam_params, subg.reshape(1, DV), p3, p3, p3, p3,
      *([cache_k] * n_pages), *([cache_v] * n_pages))


def _seg_sum(x, bd_ref):
    return jnp.dot(x, bd_ref[...], preferred_element_type=F32, precision=HIGHEST)


def _rwkv_pre_kernel(*refs, shift, has_vres, tm):
    it = iter(refs)
    cur = [next(it) for _ in range(4)]
    prev = None if shift else [next(it) for _ in range(4)]
    mu = [next(it) for _ in range(4)]
    w0, wup, a0, aup, kk_ref, ka_ref, rk_ref, bd_ref = [next(it) for _ in range(8)]
    if has_vres:
        vf_ref, vr0, vra, vrb = [next(it) for _ in range(4)]
    outs = [next(it) for _ in range(7)]
    carry = [next(it) for _ in range(4)] if shift else None

    xs = []
    for n in range(4):
        c = cur[n][...]
        if shift:
            @pl.when(pl.program_id(1) == 0)
            def _():
                carry[n][...] = jnp.zeros(carry[n].shape, F32)
            rolled = pltpu.roll(c, 1, 0)
            rows = lax.broadcasted_iota(jnp.int32, c.shape, 0)
            pv = jnp.where(rows == 0, carry[n][...], rolled)
            carry[n][...] = c[tm - 1:tm, :]
        else:
            pv = prev[n][...]
        xs.append(c + (pv - c) * mu[n][...])
    xr, xk, xv, xl = xs

    hw = jnp.dot(jnp.tanh(xl).astype(BF16), wup[...], preferred_element_type=F32)
    w = -DECAY_SCALE * jax.nn.sigmoid(w0[...] + hw)
    if not shift:
        w = jnp.exp(w)
    a = jax.nn.sigmoid(a0[...] + jnp.dot(xl.astype(BF16), aup[...], preferred_element_type=F32))
    if has_vres:
        t1 = jnp.dot(xv.astype(BF16), vra[...], preferred_element_type=F32)
        t2 = jnp.dot(t1.astype(BF16), vrb[...], preferred_element_type=F32)
        xv = xv + (vf_ref[...] - xv) * jax.nn.sigmoid(vr0[...] + t2)
    kk = xk * kk_ref[...]
    kk = kk / jnp.maximum(jnp.sqrt(_seg_sum(kk * kk, bd_ref)), 1e-12)
    k2 = xk * (1.0 + (a - 1.0) * ka_ref[...])
    bonus = _seg_sum(xr * k2 * rk_ref[...], bd_ref) * xv
    for o, val in zip(outs, (xr, w, k2, xv, -kk, kk * a, bonus)):
        o[...] = val


def _rwkv_pre(p, prev, lw, vfirst, nb, t, tm):
    shift = prev is None
    has_vres = vfirst is not None
    nt = t // tm
    if shift:
        grid = (nb, nt)
        rmap = lambda c: (lambda b, i: (b * nt + i, c))
        cmap = lambda b, i: (0, 0)
        sem = ("parallel", "arbitrary")
    else:
        grid = (nt,)
        rmap = lambda c: (lambda i: (i, c))
        cmap = lambda i: (0, 0)
        sem = ("parallel",)
    wide = lambda c: pl.BlockSpec((tm, BRANCH), rmap(c))
    vec = lambda n: pl.BlockSpec((1, n), cmap)
    args, specs = [], []
    args += [p, p, p, p]
    specs += [wide(C_RR), wide(C_RK), wide(C_RV), pl.BlockSpec((tm, N_LORA), rmap(C_LORA128))]
    if not shift:
        args += [prev, prev, prev, prev]
        specs += [wide(0), wide(1), wide(2), pl.BlockSpec((tm, N_LORA), rmap(3 * BRANCH // N_LORA))]
    args += lw["mu"]
    specs += [vec(BRANCH)] * 3 + [vec(N_LORA)]
    args += [lw["w0"], lw["w_up"], lw["a0"], lw["a_up"], lw["k_k"], lw["k_a"], lw["r_k"], lw["bd"]]
    specs += [vec(BRANCH), pl.BlockSpec((N_LORA, BRANCH), cmap), vec(BRANCH), pl.BlockSpec((N_LORA, BRANCH), cmap),
              vec(BRANCH), vec(BRANCH), vec(BRANCH), pl.BlockSpec((BRANCH, BRANCH), cmap)]
    if has_vres:
        args += [vfirst, lw["vres_0"], lw["vres_a"], lw["vres_b"]]
        specs += [pl.BlockSpec((tm, BRANCH), rmap(0)), vec(BRANCH),
                  pl.BlockSpec(lw["vres_a"].shape, cmap), pl.BlockSpec(lw["vres_b"].shape, cmap)]
    out_spec = pl.BlockSpec((tm, BRANCH), rmap(0))
    scratch = [pltpu.VMEM((1, BRANCH), F32)] * 3 + [pltpu.VMEM((1, N_LORA), F32)] if shift else []
    return pl.pallas_call(
        functools.partial(_rwkv_pre_kernel, shift=shift, has_vres=has_vres, tm=tm),
        grid=grid, in_specs=specs, out_specs=[out_spec] * 7,
        out_shape=[jax.ShapeDtypeStruct((nb * t, BRANCH), F32)] * 7,
        scratch_shapes=scratch,
        compiler_params=_cparams(sem),
        name="rwkv_pre",
    )(*args)


RWKV_CHUNK = 64

_NN = ((1,), (0,))
_NT = ((1,), (1,))
_TN = ((0,), (0,))


def _split(x):
    hi = x.astype(BF16)
    return hi, (x - hi.astype(F32)).astype(BF16)


def _mm3(a, b, dims):
    dg = lambda x, y: lax.dot_general(x, y, (dims, ((), ())), preferred_element_type=F32)
    return dg(a[0], b[0]) + (dg(a[0], b[1]) + dg(a[1], b[0]))


def _rwkv_scan_kernel(r_ref, lw_ref, k_ref, v_ref, a_ref, b_ref, y_ref, s_ref):
    C = RWKV_CHUNK

    @pl.when(pl.program_id(1) == 0)
    def _():
        s_ref[...] = jnp.zeros(s_ref.shape, F32)

    rowi = lax.broadcasted_iota(jnp.int32, (C, C), 0)
    coli = lax.broadcasted_iota(jnp.int32, (C, C), 1)
    incl = rowi >= coli
    strict = rowi > coli

    lw = lw_ref[...]
    tri = incl.astype(BF16)
    l1 = lw.astype(BF16)
    rem = lw - l1.astype(F32)
    l2 = rem.astype(BF16)
    l3 = (rem - l2.astype(F32)).astype(BF16)
    cs = (jnp.dot(tri, l1, preferred_element_type=F32) + jnp.dot(tri, l2, preferred_element_type=F32)
          + jnp.dot(tri, l3, preferred_element_type=F32))
    p_in = jnp.exp(cs)
    p_inv = jnp.exp(-cs)
    rt = r_ref[...] * p_in
    at = a_ref[...] * jnp.exp(cs - lw)
    bt = b_ref[...] * p_inv
    kt = k_ref[...] * p_inv
    v = v_ref[...]
    p_last = p_in[C - 1:C, :]

    heads = range(H_RWKV)
    sls = [slice(h * R_HEAD, (h + 1) * R_HEAD) for h in heads]
    ops = [[_split(z[:, sl]) for z in (at, rt, bt, kt, v)] for sl in sls]
    s0 = [s_ref[h] for h in heads]
    s0_s = [_split(z) for z in s0]
    n_ab = [jnp.where(strict, _mm3(o[0], o[2], _NT), 0.0) for o in ops]
    a_ak = [jnp.where(strict, _mm3(o[0], o[3], _NT), 0.0) for o in ops]
    a_rb = [_split(jnp.where(incl, _mm3(o[1], o[2], _NT), 0.0)) for o in ops]
    a_rk = [_split(jnp.where(incl, _mm3(o[1], o[3], _NT), 0.0)) for o in ops]
    rhs = [_mm3(ops[h][0], s0_s[h], _NT) + _mm3(_split(a_ak[h]), ops[h][4], _NN) for h in heads]
    m_s = [_split(z) for z in n_ab]
    u = [rhs[h] + _mm3(m_s[h], _split(rhs[h]), _NN) for h in heads]
    for _ in range(int(math.log2(C)) - 1):
        m_s = [_split(_mm3(z, z, _NN)) for z in m_s]
        u = [u[h] + _mm3(m_s[h], _split(u[h]), _NN) for h in heads]
    u_s = [_split(z) for z in u]
    for h in heads:
        y_ref[:, sls[h]] = (_mm3(ops[h][1], s0_s[h], _NT) + _mm3(a_rb[h], u_s[h], _NN)
                            + _mm3(a_rk[h], ops[h][4], _NN))
        s_ref[h] = (s0[h] + _mm3(u_s[h], ops[h][2], _TN) + _mm3(ops[h][4], ops[h][3], _TN)) * p_last[:, sls[h]]


def _rwkv_scan(seqs, nb, t):
    tc = RWKV_CHUNK
    blk = pl.BlockSpec((None, tc, BRANCH), lambda b, c: (b, c, 0))
    return pl.pallas_call(
        _rwkv_scan_kernel,
        grid=(nb, t // tc),
        in_specs=[blk] * 6,
        out_specs=[blk, pl.BlockSpec((H_RWKV, R_HEAD, R_HEAD), lambda b, c: (b, 0, 0))],
        out_shape=[jax.ShapeDtypeStruct((nb, t, BRANCH), F32),
                   jax.ShapeDtypeStruct((nb * H_RWKV, R_HEAD, R_HEAD), F32)],
        compiler_params=_cparams(("parallel", "arbitrary")),
        name="rwkv_scan",
    )(*seqs)


def _rwkv_step_kernel(r_ref, w_ref, k_ref, a_ref, b_ref, vt_ref, s_ref, yt_ref, so_ref, *, bb):
    units = [(j, slice(h * R_HEAD, (h + 1) * R_HEAD)) for j in range(bb) for h in range(H_RWKV)]
    row = lambda ref, j, sl: ref[j:j + 1, sl]
    st = [s_ref[j, sl, :] for j, sl in units]
    sa = [jnp.sum(z * row(a_ref, j, sl), axis=-1, keepdims=True) for z, (j, sl) in zip(st, units)]
    st = [z * row(w_ref, j, sl) + x * row(b_ref, j, sl) + vt_ref[sl, j:j + 1] * row(k_ref, j, sl)
          for z, x, (j, sl) in zip(st, sa, units)]
    for z, (j, sl) in zip(st, units):
        so_ref[j, sl, :] = z
    ys = [jnp.sum(z * row(r_ref, j, sl), axis=-1, keepdims=True) for z, (j, sl) in zip(st, units)]
    for y, (j, sl) in zip(ys, units):
        yt_ref[sl, j:j + 1] = y


def _rwkv_step(r, w, k, na, b, vt, state, bb=8):
    n = r.shape[0]
    rowb = pl.BlockSpec((bb, BRANCH), lambda i: (i, 0))
    colb = pl.BlockSpec((None, BRANCH, bb), lambda i: (i, 0, 0))
    stb = pl.BlockSpec((bb, BRANCH, R_HEAD), lambda i: (i, 0, 0))
    return pl.pallas_call(
        functools.partial(_rwkv_step_kernel, bb=bb),
        grid=(n // bb,),
        in_specs=[rowb] * 5 + [colb, stb],
        out_specs=[colb, stb],
        out_shape=[jax.ShapeDtypeStruct((n // bb, BRANCH, bb), F32),
                   jax.ShapeDtypeStruct((n, BRANCH, R_HEAD), F32)],
        compiler_params=_cparams(("parallel",)),
        name="rwkv_step",
    )(r, w, k, na, b, vt, state)


def _rwkv_post_kernel(y_ref, bonus_ref, g_ref, lg_ref, lb_ref, bd_ref, o_ref):
    y = y_ref[...]
    mu = _seg_sum(y, bd_ref) * (1.0 / R_HEAD)
    d = y - mu
    var = _seg_sum(d * d, bd_ref) * (1.0 / R_HEAD)
    o = d * lax.rsqrt(var + LNX_EPS) * lg_ref[...] + lb_ref[...] + bonus_ref[...]
    o_ref[...] = (o * _silu(g_ref[...])).astype(o_ref.dtype)


def _rwkv_post(y, bonus, p, lw, tm):
    m = y.shape[0]
    blk = pl.BlockSpec((tm, BRANCH), lambda i: (i, 0))
    vec = pl.BlockSpec((1, BRANCH), lambda i: (0, 0))
    return pl.pallas_call(
        _rwkv_post_kernel,
        grid=(m // tm,),
        in_specs=[blk, blk, pl.BlockSpec((tm, BRANCH), lambda i: (i, C_RG)), vec, vec,
                  pl.BlockSpec((BRANCH, BRANCH), lambda i: (0, 0))],
        out_specs=blk,
        out_shape=jax.ShapeDtypeStruct((m, BRANCH), BF16),
        compiler_params=_cparams(("parallel",)),
        name="rwkv_post",
    )(y, bonus, p, lw["lnx_g"], lw["lnx_b"], lw["bd"])


def _softplus(x):
    return jnp.maximum(x, 0.0) + jnp.log1p(jnp.exp(-jnp.abs(x)))


def _neg_exp_alog(alog_ref):
    lanes = lax.broadcasted_iota(jnp.int32, alog_ref.shape, 1)
    return jnp.where(lanes < H_MAMBA, -jnp.exp(alog_ref[...]), 0.0)


def _gated_norm(ym, z, g):
    u = ym * _silu(z)
    ms = jnp.mean(u * u, axis=-1, keepdims=True)
    return u * lax.rsqrt(ms + RMS_EPS) * g


def _mamba_prompt_kernel(mx_ref, mbc_ref, mdt_ref, mz_ref, cwx_ref, cwb_ref, cbx_ref, cbb_ref, dtb_ref, alog_ref,
                         dskip_ref, gn_ref, o_ref, s_ref, extx_ref, extb_ref, ym_ref):
    L = SSD_CHUNK
    pad = 8

    @pl.when(pl.program_id(1) == 0)
    def _():
        extx_ref[0:pad, :] = jnp.zeros((pad, BRANCH), F32)
        extb_ref[0:pad, :] = jnp.zeros((pad, BRANCH), F32)
        s_ref[...] = jnp.zeros(s_ref.shape, F32)

    def conv(ext_ref, cur_ref, cw_ref, cb_ref):
        ext_ref[pad:pad + L, :] = cur_ref[...]
        acc = cb_ref[...]
        for j in range(M_CONV):
            lo = pad - (M_CONV - 1) + j
            acc = acc + ext_ref[lo:lo + L, :] * cw_ref[j:j + 1, :]
        ext_ref[0:pad, :] = ext_ref[L:L + pad, :]
        return _silu(acc)

    x = conv(extx_ref, mx_ref, cwx_ref, cbx_ref)
    bc = conv(extb_ref, mbc_ref, cwb_ref, cbb_ref)
    dt = _softplus(mdt_ref[...] + dtb_ref[...])
    adt = dt * _neg_exp_alog(alog_ref)
    rowi = lax.broadcasted_iota(jnp.int32, (L, L), 0)
    coli = lax.broadcasted_iota(jnp.int32, (L, L), 1)
    lower = rowi >= coli
    acs = jnp.dot(lower.astype(F32), adt, preferred_element_type=F32, precision=HIGHEST)
    acs_t = acs.T
    last = acs[L - 1:L, :]
    dstate = jnp.exp(last - acs)
    eacs = jnp.exp(acs)
    elast = jnp.exp(last)
    bcb = bc.astype(BF16)
    grp = []
    for g in range(M_GROUPS):
        bg = bcb[:, g * M_STATE:(g + 1) * M_STATE]
        cg = bcb[:, (M_GROUPS + g) * M_STATE:(M_GROUPS + g + 1) * M_STATE]
        grp.append((bg, cg, _dot_t(cg, bg)))
    for h in range(H_MAMBA):
        bg, cg, cb = grp[h // (H_MAMBA // M_GROUPS)]
        sl = slice(h * M_HEADDIM, (h + 1) * M_HEADDIM)
        decay = jnp.where(lower, jnp.exp(acs[:, h:h + 1] - acs_t[h:h + 1, :]), 0.0)
        xh = x[:, sl]
        xdt = xh * dt[:, h:h + 1]
        yd = jnp.dot((cb * decay).astype(BF16), xdt.astype(BF16), preferred_element_type=F32)
        st = s_ref[h]
        yo = _dot_t(cg, st.astype(BF16)) * eacs[:, h:h + 1]
        upd = lax.dot_general((xdt * dstate[:, h:h + 1]).astype(BF16), bg, (((0,), (0,)), ((), ())),
                              preferred_element_type=F32)
        s_ref[h] = st * elast[:, h:h + 1] + upd
        ym_ref[:, sl] = yd + yo + dskip_ref[:, sl] * xh
    o_ref[...] = _gated_norm(ym_ref[...], mz_ref[...], gn_ref[...]).astype(o_ref.dtype)


def _mamba_prompt(p, lw, nb, t):
    L = SSD_CHUNK
    nt = t // L
    rmap = lambda c: (lambda b, i: (b * nt + i, c))
    cmap = lambda b, i: (0, 0)
    wide = lambda c: pl.BlockSpec((L, BRANCH), rmap(c))
    vec = lambda n: pl.BlockSpec((1, n), cmap)
    cw = pl.BlockSpec((M_CONV, BRANCH), cmap)
    return pl.pallas_call(
        _mamba_prompt_kernel,
        grid=(nb, nt),
        in_specs=[wide(C_MX), wide(C_MBC), pl.BlockSpec((L, 128), rmap(C_DT128)), wide(C_MZ),
                  cw, cw, vec(BRANCH), vec(BRANCH), vec(128), vec(128), vec(BRANCH), vec(BRANCH)],
        out_specs=[pl.BlockSpec((L, BRANCH), rmap(0)),
                   pl.BlockSpec((None, H_MAMBA, M_HEADDIM, M_STATE), lambda b, i: (b, 0, 0, 0))],
        out_shape=[jax.ShapeDtypeStruct((nb * t, BRANCH), BF16),
                   jax.ShapeDtypeStruct((nb, H_MAMBA, M_HEADDIM, M_STATE), F32)],
        scratch_shapes=[pltpu.VMEM((L + 8, BRANCH), F32), pltpu.VMEM((L + 8, BRANCH), F32),
                        pltpu.VMEM((L, BRANCH), F32)],
        compiler_params=_cparams(("parallel", "arbitrary")),
        name="mamba_prompt",
    )(p, p, p, p, lw["cw_x"], lw["cw_bc"], lw["cb_x"], lw["cb_bc"], lw["dt_bias"], lw["a_log"],
      lw["d_skip"], lw["gnorm_g"])


def _mamba_dec_pre_kernel(mx_ref, mbc_ref, mdt_ref, sx_ref, sb_ref, cwx_ref, cwb_ref, cbx_ref, cbb_ref,
                          dtb_ref, alog_ref, x_ref, bc_ref, xdt_ref, da_ref):
    def conv(st_ref, cur_ref, cw_ref, cb_ref):
        acc = cb_ref[...]
        for j in range(M_CONV - 1):
            acc = acc + st_ref[j] * cw_ref[j:j + 1, :]
        acc = acc + cur_ref[...] * cw_ref[M_CONV - 1:M_CONV, :]
        return _silu(acc)

    x = conv(sx_ref, mx_ref, cwx_ref, cbx_ref)
    x_ref[...] = x
    bc_ref[...] = conv(sb_ref, mbc_ref, cwb_ref, cbb_ref)
    dt = _softplus(mdt_ref[...] + dtb_ref[...])
    da = jnp.exp(dt * _neg_exp_alog(alog_ref))
    n = x.shape[0]
    for h in range(H_MAMBA):
        sl = slice(h * M_HEADDIM, (h + 1) * M_HEADDIM)
        xdt_ref[:, sl] = x[:, sl] * dt[:, h:h + 1]
        da_ref[:, sl] = jnp.broadcast_to(da[:, h:h + 1], (n, M_HEADDIM))


def _mamba_dec_pre(p, conv_x, conv_bc, lw):
    n = p.shape[0]
    cmap = lambda i: (0, 0)
    wide = lambda c: pl.BlockSpec((n, BRANCH), lambda i: (0, c))
    vec = lambda m: pl.BlockSpec((1, m), cmap)
    cw = pl.BlockSpec((M_CONV, BRANCH), cmap)
    st = pl.BlockSpec((M_CONV - 1, n, BRANCH), lambda i: (0, 0, 0))
    out = pl.BlockSpec((n, BRANCH), cmap)
    return pl.pallas_call(
        _mamba_dec_pre_kernel,
        grid=(1,),
        in_specs=[wide(C_MX), wide(C_MBC), pl.BlockSpec((n, 128), lambda i: (0, C_DT128)), st, st,
                  cw, cw, vec(BRANCH), vec(BRANCH), vec(128), vec(128)],
        out_specs=[out] * 4,
        out_shape=[jax.ShapeDtypeStruct((n, BRANCH), F32)] * 4,
        compiler_params=_cparams(("arbitrary",)),
        name="mamba_dec_pre",
    )(p, p, p, conv_x, conv_bc, lw["cw_x"], lw["cw_bc"], lw["cb_x"], lw["cb_bc"], lw["dt_bias"], lw["a_log"])


def _mamba_step_kernel(bc_ref, xdtt_ref, dat_ref, s_ref, yt_ref, so_ref, *, bb):
    units = [(j, h // (H_MAMBA // M_GROUPS), slice(h * M_HEADDIM, (h + 1) * M_HEADDIM))
             for j in range(bb) for h in range(H_MAMBA)]
    st = []
    for j, g, sl in units:
        bg = bc_ref[j:j + 1, g * M_STATE:(g + 1) * M_STATE]
        z = s_ref[j, sl, :] * dat_ref[sl, j:j + 1] + xdtt_ref[sl, j:j + 1] * bg
        so_ref[j, sl, :] = z
        st.append(z)
    ys = [jnp.sum(z * bc_ref[j:j + 1, (M_GROUPS + g) * M_STATE:(M_GROUPS + g + 1) * M_STATE], axis=-1, keepdims=True)
          for z, (j, g, sl) in zip(st, units)]
    for y, (j, g, sl) in zip(ys, units):
        yt_ref[sl, j:j + 1] = y


def _mamba_step(bc, xdtt, dat, state, bb=8):
    n = bc.shape[0]
    rowb = pl.BlockSpec((bb, BRANCH), lambda i: (i, 0))
    colb = pl.BlockSpec((None, BRANCH, bb), lambda i: (i, 0, 0))
    stb = pl.BlockSpec((bb, BRANCH, M_STATE), lambda i: (i, 0, 0))
    return pl.pallas_call(
        functools.partial(_mamba_step_kernel, bb=bb),
        grid=(n // bb,),
        in_specs=[rowb, colb, colb, stb],
        out_specs=[colb, stb],
        out_shape=[jax.ShapeDtypeStruct((n // bb, BRANCH, bb), F32),
                   jax.ShapeDtypeStruct((n, BRANCH, M_STATE), F32)],
        compiler_params=_cparams(("parallel",)),
        name="mamba_step",
    )(bc, xdtt, dat, state)


def _mamba_dec_post_kernel(y_ref, x_ref, z_ref, dskip_ref, gn_ref, o_ref):
    ym = y_ref[...] + dskip_ref[...] * x_ref[...]
    o_ref[...] = _gated_norm(ym, z_ref[...], gn_ref[...]).astype(o_ref.dtype)


def _mamba_dec_post(y, x, p, lw):
    n = y.shape[0]
    blk = pl.BlockSpec((n, BRANCH), lambda i: (0, 0))
    vec = pl.BlockSpec((1, BRANCH), lambda i: (0, 0))
    return pl.pallas_call(
        _mamba_dec_post_kernel,
        grid=(1,),
        in_specs=[blk, blk, pl.BlockSpec((n, BRANCH), lambda i: (0, C_MZ)), vec, vec],
        out_specs=blk,
        out_shape=jax.ShapeDtypeStruct((n, BRANCH), BF16),
        compiler_params=_cparams(("arbitrary",)),
        name="mamba_dec_post",
    )(y, x, p, lw["d_skip"], lw["gnorm_g"])


def _xattn_prompt_kernel(q_ref, g_ref, mk_ref, mv_ref, o_ref):
    q = q_ref[...].astype(BF16)
    mk = mk_ref[...].astype(BF16)
    mv = mv_ref[...].astype(BF16)
    for h in range(H_X):
        sl = slice(h * D_X, (h + 1) * D_X)
        s = _dot_t(q[:, sl], mk[:, sl]) * (D_X ** -0.5)
        e = jnp.exp(s - jnp.max(s, axis=-1, keepdims=True))
        o = jnp.dot(e.astype(BF16), mv[:, sl], preferred_element_type=F32) / jnp.sum(e, axis=-1, keepdims=True)
        o_ref[:, sl] = (o * _silu(g_ref[:, sl])).astype(o_ref.dtype)


def _xattn_prompt(p, mkv, nb, t):
    tq = min(512, t)
    nq = t // tq
    return pl.pallas_call(
        _xattn_prompt_kernel,
        grid=(nb, nq),
        in_specs=[pl.BlockSpec((tq, BRANCH), lambda b, i: (b * nq + i, C_XQ)),
                  pl.BlockSpec((tq, BRANCH), lambda b, i: (b * nq + i, C_XG)),
                  pl.BlockSpec((N_MEM, BRANCH), lambda b, i: (b, 0)),
                  pl.BlockSpec((N_MEM, BRANCH), lambda b, i: (b, 1))],
        out_specs=pl.BlockSpec((tq, BRANCH), lambda b, i: (b * nq + i, 0)),
        out_shape=jax.ShapeDtypeStruct((nb * t, BRANCH), BF16),
        compiler_params=_cparams(("parallel", "parallel")),
        name="xattn_prompt",
    )(p, p, mkv, mkv)


def _xattn_decode_kernel(q_ref, g_ref, mk_ref, mv_ref, o_ref):
    nrow = 8
    q = q_ref[...].astype(BF16)
    heads = range(H_X)
    s = [_dot_t(jnp.broadcast_to(q[:, h * D_X:(h + 1) * D_X], (nrow, D_X)),
                mk_ref[pl.ds(h, N_MEM, stride=H_X), :].astype(BF16)) * (D_X ** -0.5) for h in heads]
    e = [jnp.exp(z - jnp.max(z, axis=-1, keepdims=True)) for z in s]
    r = [jnp.dot(e[h].astype(BF16), mv_ref[pl.ds(h, N_MEM, stride=H_X), :].astype(BF16),
                 preferred_element_type=F32) for h in heads]
    outs = [(r[h] / jnp.sum(e[h], axis=-1, keepdims=True))[0:1, :] for h in heads]
    o_ref[...] = (jnp.concatenate(outs, axis=-1) * _silu(g_ref[...])).astype(o_ref.dtype)


def _xattn_decode(p3, layer, mem_k, mem_v):
    n = p3.shape[0]
    mem = pl.BlockSpec((None, None, N_MEM * H_X, D_X), lambda b: (layer, b, 0, 0))
    return pl.pallas_call(
        _xattn_decode_kernel,
        grid=(n,),
        in_specs=[pl.BlockSpec((None, 1, BRANCH), lambda b: (b, 0, C_XQ)),
                  pl.BlockSpec((None, 1, BRANCH), lambda b: (b, 0, C_XG)), mem, mem],
        out_specs=pl.BlockSpec((None, 1, BRANCH), lambda b: (b, 0, 0)),
        out_shape=jax.ShapeDtypeStruct((n, 1, BRANCH), BF16),
        compiler_params=_cparams(("parallel",)),
        name="xattn_decode",
    )(p3, p3, mem_k, mem_v)


def _layer_weights(l, prm):
    w = prm["w_in"][l].astype(BF16)
    o_rs, o_rg, o_mz, o_mxbc, o_dt, o_xq, o_xg = 2048, 3712, 4224, 4736, 5760, 5768, 6280
    w_perm = jnp.concatenate([
        w[:, :o_rs + 3 * BRANCH], w[:, o_rg:o_mz], w[:, o_mz:o_mxbc], w[:, o_xq:o_xg], w[:, o_xg:],
        w[:, o_mxbc:o_dt], w[:, o_rs + 3 * BRANCH:o_rg], w[:, o_dt:o_xq],
        jnp.zeros((D_MODEL, 128 - H_MAMBA), w.dtype)], axis=1).astype(BF16)
    row = lambda v: v.reshape(1, -1).astype(F32)
    mu = prm["shift_mu"][l].astype(F32)
    seg = jnp.arange(BRANCH) // R_HEAD
    zl = jnp.zeros((N_LORA // 2, BRANCH), F32)
    cw = prm["conv_w"][l].astype(F32)
    cb = prm["conv_b"][l].astype(F32)
    pad8 = lambda v: jnp.pad(v.astype(F32), (0, 128 - H_MAMBA)).reshape(1, 128)
    lw = {
        "w_in": w_perm, "norm_g": prm["norm_g"][l], "w_out": prm["w_out"][l].astype(BF16),
        "lam": [row(prm[n][l]) for n in ("lam_q1", "lam_k1", "lam_q2", "lam_k2")],
        "subln_g": prm["subln_g"][l].astype(F32),
        "mu": [row(mu[:BRANCH]), row(mu[BRANCH:2 * BRANCH]), row(mu[2 * BRANCH:3 * BRANCH]), row(mu[3 * BRANCH:])],
        "w0": row(prm["w0"][l]), "a0": row(prm["a0"][l]),
        "w_up": jnp.concatenate([prm["w_up"][l].astype(F32), zl], axis=0).astype(BF16),
        "a_up": jnp.concatenate([zl, prm["a_up"][l].astype(F32)], axis=0).astype(BF16),
        "k_k": row(prm["k_k"][l]), "k_a": row(prm["k_a"][l]), "r_k": row(prm["r_k"][l]),
        "lnx_g": row(prm["lnx_g"][l]), "lnx_b": row(prm["lnx_b"][l]),
        "bd": (seg[:, None] == seg[None, :]).astype(F32),
        "cw_x": cw[:, :BRANCH], "cw_bc": cw[:, BRANCH:], "cb_x": row(cb[:BRANCH]), "cb_bc": row(cb[BRANCH:]),
        "dt_bias": pad8(prm["dt_bias"][l]), "a_log": pad8(prm["a_log"][l]),
        "d_skip": row(jnp.repeat(prm["d_skip"][l].astype(F32), M_HEADDIM)),
        "gnorm_g": row(prm["gnorm_g"][l]),
        "w_mkv": jnp.concatenate([prm["w_mk"][l], prm["w_mv"][l]], axis=1).astype(BF16),
        "mem_norm_g": prm["mem_norm_g"][l],
    }
    if l > 0:
        lw["vres_0"] = row(prm["vres_0"][l - 1])
        lw["vres_a"] = prm["vres_a"][l - 1].astype(BF16)
        lw["vres_b"] = prm["vres_b"][l - 1].astype(BF16)
    return lw


def _lam_init(l):
    return 0.8 - 0.6 * math.exp(-0.3 * l)


def _prompt_layer(x, l, lw, mem, vfirst, nb, t, fg, final, tm_proj, tm_row):
    p = _rms_matmul(x, lw["norm_g"], lw["w_in"], tm_proj, 1152)
    mkv = _rms_matmul(mem, lw["mem_norm_g"], lw["w_mkv"], mem.shape[0], 2 * BRANCH)
    out_a = _diff_prompt(p, lw["lam"], lw["subln_g"], nb, t, _lam_init(l))
    r, w, k, v, na, b, bonus = _rwkv_pre(p, None, lw, vfirst, nb, t, tm_row)
    seqs = [z.reshape(nb, t, BRANCH) for z in (r, w, k, v, na, b)]
    y, rw_state = _rwkv_scan(seqs, nb, t)
    out_b = _rwkv_post(y.reshape(nb * t, BRANCH), bonus, p, lw, tm_row)
    out_m, ssm_state = _mamba_prompt(p, lw, nb, t)
    out_x = _xattn_prompt(p, mkv, nb, t)
    x = _outproj((out_a, out_b, out_m, out_x), lw["w_out"], x, fg, tm_row, final)
    return x, p, mkv, rw_state, ssm_state, (v if vfirst is None else vfirst)


def _decode_layer(x, l, lw, caches, vfirst, fg, final):
    cache_k, cache_v, mem_k, mem_v, page_table, rw_prev, rw_state, conv_state, ssm_state = caches
    n = x.shape[0]
    bb = 8
    p = _rms_matmul(x, lw["norm_g"], lw["w_in"], n, 1152)
    p3 = p.reshape(n, 1, N_PROJ)
    out_a = _diff_decode(p3, l, cache_k, cache_v, page_table, lw["lam"], lw["subln_g"], _lam_init(l)).reshape(n, BRANCH)

    r, w, k, v, na, b, bonus = _rwkv_pre(p, rw_prev, lw, vfirst, 1, n, n)
    to_cols = lambda z: z.reshape(n // bb, bb, BRANCH).transpose(0, 2, 1)
    from_cols = lambda z: z.transpose(0, 2, 1).reshape(n, BRANCH)
    yt, rw_new = _rwkv_step(r, w, k, na, b, to_cols(v), rw_state.reshape(n, BRANCH, R_HEAD), bb)
    out_b = _rwkv_post(from_cols(yt), bonus, p, lw, n)

    cs = conv_state.transpose(1, 0, 2)
    xm, bc, xdt, da = _mamba_dec_pre(p, cs[:, :, :BRANCH], cs[:, :, BRANCH:], lw)
    ymt, ssm_new = _mamba_step(bc, to_cols(xdt), to_cols(da), ssm_state.reshape(n, BRANCH, M_STATE), bb)
    out_m = _mamba_dec_post(from_cols(ymt), xm, p, lw)

    out_x = _xattn_decode(p3, l, mem_k, mem_v).reshape(n, BRANCH)
    x = _outproj((out_a, out_b, out_m, out_x), lw["w_out"], x, fg, n, final)
    return x, p, rw_new, ssm_new, (v if vfirst is None else vfirst)


def _cols(p, c, width=BRANCH):
    return p[..., c * width:(c + 1) * width]


def kernel(x_prompt, x_sample, cache_diff_k, cache_diff_v, cache_mem_k, cache_mem_v, state_rwkv_shift, state_rwkv, state_conv, state_ssm, page_table, mem_prompt, norm_g, w_in, w_out, lam_q1, lam_k1, lam_q2, lam_k2, subln_g, shift_mu, w0, w_up, a0, a_up, k_k, k_a, r_k, lnx_g, lnx_b, vres_0, vres_a, vres_b, conv_w, conv_b, dt_bias, a_log, d_skip, gnorm_g, mem_norm_g, w_mk, w_mv, final_g):
    prm = dict(norm_g=norm_g, w_in=w_in, w_out=w_out, lam_q1=lam_q1, lam_k1=lam_k1, lam_q2=lam_q2, lam_k2=lam_k2,
               subln_g=subln_g, shift_mu=shift_mu, w0=w0, w_up=w_up, a0=a0, a_up=a_up, k_k=k_k, k_a=k_a, r_k=r_k,
               lnx_g=lnx_g, lnx_b=lnx_b, vres_0=vres_0, vres_a=vres_a, vres_b=vres_b, conv_w=conv_w, conv_b=conv_b,
               dt_bias=dt_bias, a_log=a_log, d_skip=d_skip, gnorm_g=gnorm_g, mem_norm_g=mem_norm_g,
               w_mk=w_mk, w_mv=w_mv)
    depth = w_in.shape[0]
    nb, t, _ = x_prompt.shape
    n = x_sample.shape[0]
    n_pool = cache_diff_k.shape[1]
    ck = jnp.transpose(cache_diff_k, (0, 1, 3, 4, 5, 2)).reshape(depth, n_pool, BRANCH, PAGE_SIZE)
    cv = cache_diff_v.reshape(depth, n_pool, PAGE_SIZE * H_DIFF, DV)
    cmk = cache_mem_k.reshape(depth, n, N_MEM * H_X, D_X)
    cmv = cache_mem_v.reshape(depth, n, N_MEM * H_X, D_X)
    mem = mem_prompt.reshape(nb * N_MEM, D_MODEL)

    xp = x_prompt.reshape(nb * t, D_MODEL)
    xs = x_sample.reshape(n, D_MODEL)
    vf_p = vf_s = None
    outs = [[] for _ in range(14)]
    for l in range(depth):
        lw = _layer_weights(l, prm)
        final = l == depth - 1
        xp, p, mkv, rw_p, ssm_p, vf_p = _prompt_layer(xp, l, lw, mem, vf_p, nb, t, final_g, final,
                                                      min(1024, nb * t), min(512, t))
        p3 = p.reshape(nb, t, N_PROJ)
        last = p3[:, t - 1]
        outs[0].append(_cols(p3, C_DK).reshape(nb, t, H_DIFF, 2, DQK))
        outs[1].append(_cols(p3, C_DV).reshape(nb, t, H_DIFF, DV))
        outs[2].append(mkv[:, :BRANCH].reshape(nb, N_MEM, H_X, D_X))
        outs[3].append(mkv[:, BRANCH:].reshape(nb, N_MEM, H_X, D_X))
        outs[4].append(jnp.concatenate([_cols(last, C_RR), _cols(last, C_RK), _cols(last, C_RV),
                                        _cols(last, C_LORA128, 128)], axis=-1))
        outs[5].append(rw_p.reshape(nb, H_RWKV, R_HEAD, R_HEAD))
        tail = p3[:, t - (M_CONV - 1):]
        outs[6].append(jnp.concatenate([_cols(tail, C_MX), _cols(tail, C_MBC)], axis=-1))
        outs[7].append(ssm_p)

        caches = (ck, cv, cmk, cmv, page_table, state_rwkv_shift[l], state_rwkv[l], state_conv[l], state_ssm[l])
        xs, ps, rw_s, ssm_s, vf_s = _decode_layer(xs, l, lw, caches, vf_s, final_g, final)
        outs[8].append(_cols(ps, C_DK).reshape(n, 1, H_DIFF, 2, DQK))
        outs[9].append(_cols(ps, C_DV).reshape(n, 1, H_DIFF, DV))
        outs[10].append(jnp.concatenate([_cols(ps, C_RR), _cols(ps, C_RK), _cols(ps, C_RV),
                                         _cols(ps, C_LORA128, 128)], axis=-1))
        outs[11].append(rw_s.reshape(n, H_RWKV, R_HEAD, R_HEAD))
        outs[12].append(jnp.concatenate(
            [state_conv[l][:, 1:], jnp.concatenate([_cols(ps, C_MX), _cols(ps, C_MBC)], axis=-1)[:, None]], axis=1))
        outs[13].append(ssm_s.reshape(n, H_MAMBA, M_HEADDIM, M_STATE))

    return (xp.reshape(nb, t, D_MODEL), xs.reshape(n, 1, D_MODEL)) + tuple(jnp.stack(o) for o in outs)
```

```python
import functools
import math

import jax
import jax.numpy as jnp
from jax import lax
from jax.experimental import pallas as pl
from jax.experimental.pallas import tpu as pltpu

F32 = jnp.float32
BF16 = jnp.bfloat16
HIGHEST = lax.Precision.HIGHEST

D_MODEL = 2048
BRANCH = 512
H_DIFF = 4
DQK = 64
DV = 128
R_HEAD = 64
H_RWKV = 8
N_LORA = 128
N_SHIFT = 3 * BRANCH + N_LORA
DECAY_SCALE = 0.6065306597
LNX_EPS = 64e-5
M_HEADDIM = 64
H_MAMBA = 8
M_GROUPS = 2
M_STATE = 128
M_CONV = 4
SSD_CHUNK = 128
H_X = 4
D_X = 128
N_MEM = 256
RMS_EPS = 1e-5
PAGE_SIZE = 128

C_DQ, C_DK, C_DV, C_DG, C_RR, C_RK, C_RV, C_RG, C_MZ, C_XQ, C_XG, C_MX, C_MBC = range(13)
C_LORA128 = 13 * BRANCH // 128
C_DT128 = C_LORA128 + 1
N_PROJ = 13 * BRANCH + 2 * 128

VMEM_LIMIT = 56 * 1024 * 1024


def _cparams(sem):
    return pltpu.CompilerParams(dimension_semantics=sem, vmem_limit_bytes=VMEM_LIMIT)


def _silu(x):
    return x * jax.nn.sigmoid(x)


def _dot_t(a, b):
    return lax.dot_general(a, b, (((1,), (1,)), ((), ())), preferred_element_type=F32)


def _rms_matmul_kernel(x_ref, g_ref, w_ref, o_ref, h_ref):
    @pl.when(pl.program_id(1) == 0)
    def _():
        x = x_ref[...]
        ms = jnp.mean(x * x, axis=-1, keepdims=True)
        h_ref[...] = (x * lax.rsqrt(ms + RMS_EPS) * g_ref[...]).astype(BF16)

    o_ref[...] = jnp.dot(h_ref[...], w_ref[...], preferred_element_type=F32)


def _rms_matmul(x, g, w, tm, tn):
    m, k = x.shape
    n = w.shape[1]
    return pl.pallas_call(
        _rms_matmul_kernel,
        grid=(m // tm, n // tn),
        in_specs=[pl.BlockSpec((tm, k), lambda i, j: (i, 0)),
                  pl.BlockSpec((1, k), lambda i, j: (0, 0)),
                  pl.BlockSpec((k, tn), lambda i, j: (0, j))],
        out_specs=pl.BlockSpec((tm, tn), lambda i, j: (i, j)),
        out_shape=jax.ShapeDtypeStruct((m, n), F32),
        scratch_shapes=[pltpu.VMEM((tm, k), BF16)],
        compiler_params=_cparams(("parallel", "arbitrary")),
        name="rms_matmul",
    )(x, g.reshape(1, k), w)


def _outproj_kernel(a_ref, b_ref, m_ref, c_ref, w_ref, x_ref, fg_ref, o_ref, *, final):
    acc = x_ref[...]
    for g, r in enumerate((a_ref, b_ref, m_ref, c_ref)):
        acc = acc + jnp.dot(r[...], w_ref[g * BRANCH:(g + 1) * BRANCH, :], preferred_element_type=F32)
    if final:
        ms = jnp.mean(acc * acc, axis=-1, keepdims=True)
        acc = acc * lax.rsqrt(ms + RMS_EPS) * fg_ref[...]
    o_ref[...] = acc


def _outproj(parts, w, x, fg, tm, final):
    m = x.shape[0]
    part_spec = pl.BlockSpec((tm, BRANCH), lambda i: (i, 0))
    return pl.pallas_call(
        functools.partial(_outproj_kernel, final=final),
        grid=(m // tm,),
        in_specs=[part_spec] * 4 + [pl.BlockSpec((D_MODEL, D_MODEL), lambda i: (0, 0)),
                                    pl.BlockSpec((tm, D_MODEL), lambda i: (i, 0)),
                                    pl.BlockSpec((1, D_MODEL), lambda i: (0, 0))],
        out_specs=pl.BlockSpec((tm, D_MODEL), lambda i: (i, 0)),
        out_shape=jax.ShapeDtypeStruct((m, D_MODEL), F32),
        compiler_params=_cparams(("parallel",)),
        name="outproj",
    )(*parts, w, x, fg.reshape(1, D_MODEL))


def _alibi_slope(h):
    return 2.0 ** (-8.0 * (h + 1) / H_DIFF)


def _lambda(lq1, lk1, lq2, lk2, lam_init):
    s1 = jnp.sum(lq1[...] * lk1[...], axis=-1, keepdims=True)
    s2 = jnp.sum(lq2[...] * lk2[...], axis=-1, keepdims=True)
    return jnp.exp(s1) - jnp.exp(s2) + lam_init


def _subln_gate(o, subg, gate, lam_init):
    ms = jnp.mean(o * o, axis=-1, keepdims=True)
    o = (o * lax.rsqrt(ms + RMS_EPS) * subg) * (1.0 - lam_init)
    return o * _silu(gate)


def _diff_prompt_kernel(lq1, lk1, lq2, lk2, subg_ref, q_ref, k_ref, v_ref, g_ref, o_ref,
                        m_ref, l_ref, acc_ref, *, tq, lam_init):
    qi = pl.program_id(1)
    ki = pl.program_id(2)

    @pl.when(ki == 0)
    def _():
        m_ref[...] = jnp.full(m_ref.shape, -1e30, F32)
        l_ref[...] = jnp.zeros(l_ref.shape, F32)
        acc_ref[...] = jnp.zeros(acc_ref.shape, F32)

    def block(masked):
        lane = lax.broadcasted_iota(jnp.int32, (tq, 2 * DQK), 1)
        slot = lane % DQK
        rows = lax.broadcasted_iota(jnp.int32, (tq, 2 * DQK), 0)
        lo = (rows % 256).astype(F32)
        hi = (rows // 256).astype(F32)
        off = ((qi - ki) * tq).astype(F32)
        zero = jnp.zeros((tq, 2 * DQK), F32)
        fq = jnp.where(slot == 0, -lo, jnp.where(slot == 1, -256.0 * hi, jnp.where(
            slot == 2, 1.0, jnp.where(slot == 3, 256.0, jnp.where(slot == 4, -off, zero)))))
        fk = jnp.where(slot == 0, 1.0, jnp.where(slot == 1, 1.0, jnp.where(
            slot == 2, lo, jnp.where(slot == 3, hi, jnp.where(slot == 4, 1.0, zero))))).astype(BF16)
        first = lane < DQK
        if masked:
            causal = (lax.broadcasted_iota(jnp.int32, (tq, tq), 0) >= lax.broadcasted_iota(jnp.int32, (tq, tq), 1))
        q = (q_ref[...] * (DQK ** -0.5)).astype(BF16)
        k = k_ref[...].astype(BF16)
        v = v_ref[...].astype(BF16)
        for h in range(H_DIFF):
            hs = slice(h * DV, (h + 1) * DV)
            fqh = (fq * _alibi_slope(h)).astype(BF16)
            qh, kh, vh = q[:, hs], k[:, hs], v[:, hs]
            for c in range(2):
                j = 2 * h + c
                keep = first if c == 0 else jnp.logical_not(first)
                s = _dot_t(jnp.where(keep, qh, fqh), jnp.where(keep, kh, fk))
                if masked:
                    s = jnp.where(causal, s, -jnp.inf)
                m_prev = m_ref[j]
                m_new = jnp.maximum(m_prev, jnp.max(s, axis=-1, keepdims=True))
                alpha = jnp.exp(m_prev - m_new)
                p = jnp.exp(s - jnp.concatenate([m_new] * (tq // DV), axis=1))
                l_ref[j] = alpha * l_ref[j] + jnp.sum(p, axis=-1, keepdims=True)
                acc_ref[j] = alpha * acc_ref[j] + jnp.dot(p.astype(BF16), vh, preferred_element_type=F32)
                m_ref[j] = m_new

    @pl.when(ki < qi)
    def _():
        block(False)

    @pl.when(ki == qi)
    def _():
        block(True)
        lam = _lambda(lq1, lk1, lq2, lk2, lam_init)
        for h in range(H_DIFF):
            o = acc_ref[2 * h] / l_ref[2 * h] - lam * (acc_ref[2 * h + 1] / l_ref[2 * h + 1])
            sl = slice(h * DV, (h + 1) * DV)
            o_ref[:, sl] = _subln_gate(o, subg_ref[...], g_ref[:, sl], lam_init).astype(o_ref.dtype)


def _diff_prompt(p, lam_params, subg, nb, t, lam_init):
    tq = min(512, t)
    nq = t // tq
    vec = pl.BlockSpec((1, DQK), lambda b, i, j: (0, 0))

    def qmap(c):
        return lambda b, i, j: (b * nq + i, c)

    def kmap(c):
        return lambda b, i, j: (b * nq + jnp.minimum(i, j), c)

    return pl.pallas_call(
        functools.partial(_diff_prompt_kernel, tq=tq, lam_init=lam_init),
        grid=(nb, nq, nq),
        in_specs=[vec, vec, vec, vec, pl.BlockSpec((1, DV), lambda b, i, j: (0, 0)),
                  pl.BlockSpec((tq, BRANCH), qmap(C_DQ)),
                  pl.BlockSpec((tq, BRANCH), kmap(C_DK)),
                  pl.BlockSpec((tq, BRANCH), kmap(C_DV)),
                  pl.BlockSpec((tq, BRANCH), qmap(C_DG))],
        out_specs=pl.BlockSpec((tq, BRANCH), lambda b, i, j: (b * nq + i, 0)),
        out_shape=jax.ShapeDtypeStruct((nb * t, BRANCH), BF16),
        scratch_shapes=[pltpu.VMEM((2 * H_DIFF, tq, DV), F32)] * 3,
        compiler_params=_cparams(("parallel", "parallel", "arbitrary")),
        name="diff_attn_prompt",
    )(*lam_params, subg.reshape(1, DV), p, p, p, p)


def _diff_decode_kernel(pt_ref, lq1, lk1, lq2, lk2, subg_ref, q_ref, kc_ref, vc_ref, g_ref, *rest,
                        n_pages, lam_init):
    kp = rest[:n_pages]
    vp = rest[n_pages:2 * n_pages]
    o_ref = rest[2 * n_pages]
    nrow = 2 * H_DIFF
    row = lax.broadcasted_iota(jnp.int32, (nrow, BRANCH), 0)
    lane = lax.broadcasted_iota(jnp.int32, (nrow, BRANCH), 1)
    q = q_ref[...] * (DQK ** -0.5)
    qm = jnp.where(lane // DQK == row, q, 0.0).astype(BF16)

    rowc = lax.broadcasted_iota(jnp.int32, (nrow, 1), 0)
    slope = jnp.zeros((nrow, 1), F32)
    for h in range(H_DIFF):
        slope = jnp.where(rowc // 2 == h, _alibi_slope(h), slope)

    past = n_pages * PAGE_SIZE
    s = jnp.concatenate([jnp.dot(qm, kp[j][...].astype(BF16), preferred_element_type=F32)
                         for j in range(n_pages)], axis=-1)
    dist = (past - lax.broadcasted_iota(jnp.int32, (nrow, past), 1)).astype(F32)
    s = s + (-slope) * dist
    kc = kc_ref[...].astype(BF16).astype(F32)
    vc = vc_ref[...].astype(BF16).astype(F32)
    sc = jnp.sum(qm.astype(F32) * kc, axis=-1, keepdims=True)
    m = jnp.maximum(jnp.max(s, axis=-1, keepdims=True), sc)
    p = jnp.exp(s - m)
    pc = jnp.exp(sc - m)
    l_fin = jnp.sum(p, axis=-1, keepdims=True) + pc
    pb = p.astype(BF16)
    pcb = pc.astype(BF16).astype(F32)
    lam = _lambda(lq1, lk1, lq2, lk2, lam_init)
    sign = jnp.where(rowc % 2 == 0, 1.0, -lam) / l_fin
    g = g_ref[...]
    outs = []
    for h in range(H_DIFF):
        hs = slice(h * DV, (h + 1) * DV)
        acc = pcb * vc[:, hs]
        for j in range(n_pages):
            vh = vp[j][pl.ds(h, PAGE_SIZE, stride=H_DIFF), :].astype(BF16)
            acc = acc + jnp.dot(pb[:, j * PAGE_SIZE:(j + 1) * PAGE_SIZE], vh, preferred_element_type=F32)
        o = jnp.sum(jnp.where(rowc // 2 == h, acc * sign, 0.0), axis=0, keepdims=True)
        outs.append(_subln_gate(o, subg_ref[...], g[:, hs], lam_init))
    o_ref[...] = jnp.concatenate(outs, axis=-1).astype(o_ref.dtype)


def _diff_decode(p3, layer, cache_k, cache_v, page_table, lam_params, subg, lam_init):
    nb, n_pages = page_table.shape
    vec = pl.BlockSpec((1, DQK), lambda b, pt: (0, 0))

    def row(c):
        return pl.BlockSpec((None, 1, BRANCH), lambda b, pt: (b, 0, c))

    def page(j, shape):
        return pl.BlockSpec((None, None) + shape, lambda b, pt: (layer, pt[b * n_pages + j], 0, 0))

    kpages = [page(j, (BRANCH, PAGE_SIZE)) for j in range(n_pages)]
    vpages = [page(j, (PAGE_SIZE * H_DIFF, DV)) for j in range(n_pages)]
    return pl.pallas_call(
        functools.partial(_diff_decode_kernel, n_pages=n_pages, lam_init=lam_init),
        grid_spec=pltpu.PrefetchScalarGridSpec(
            num_scalar_prefetch=1,
            grid=(nb,),
            in_specs=[vec, vec, vec, vec, pl.BlockSpec((1, DV), lambda b, pt: (0, 0)),
                      row(C_DQ), row(C_DK), row(C_DV), row(C_DG)] + kpages + vpages,
            out_specs=pl.BlockSpec((None, 1, BRANCH), lambda b, pt: (b, 0, 0))),
        out_shape=jax.ShapeDtypeStruct((nb, 1, BRANCH), BF16),
        compiler_params=_cparams(("parallel",)),
        name="diff_attn_decode",
    )(page_table.reshape(-1), *lam_params, subg.reshape(1, DV), p3, p3, p3, p3,
      *([cache_k] * n_pages), *([cache_v] * n_pages))


def _seg_sum(x, bd_ref):
    return jnp.dot(x, bd_ref[...], preferred_element_type=F32, precision=HIGHEST)


def _rwkv_pre_kernel(*refs, shift, has_vres, tm):
    it = iter(refs)
    cur = [next(it) for _ in range(4)]
    prev = None if shift else [next(it) for _ in range(4)]
    mu = [next(it) for _ in range(4)]
    w0, wup, a0, aup, kk_ref, ka_ref, rk_ref, bd_ref = [next(it) for _ in range(8)]
    if has_vres:
        vf_ref, vr0, vra, vrb = [next(it) for _ in range(4)]
    outs = [next(it) for _ in range(7)]
    carry = [next(it) for _ in range(4)] if shift else None

    xs = []
    for n in range(4):
        c = cur[n][...]
        if shift:
            @pl.when(pl.program_id(1) == 0)
            def _():
                carry[n][...] = jnp.zeros(carry[n].shape, F32)
            rolled = pltpu.roll(c, 1, 0)
            rows = lax.broadcasted_iota(jnp.int32, c.shape, 0)
            pv = jnp.where(rows == 0, carry[n][...], rolled)
            carry[n][...] = c[tm - 1:tm, :]
        else:
            pv = prev[n][...]
        xs.append(c + (pv - c) * mu[n][...])
    xr, xk, xv, xl = xs

    hw = jnp.dot(jnp.tanh(xl).astype(BF16), wup[...], preferred_element_type=F32)
    w = -DECAY_SCALE * jax.nn.sigmoid(w0[...] + hw)
    if not shift:
        w = jnp.exp(w)
    a = jax.nn.sigmoid(a0[...] + jnp.dot(xl.astype(BF16), aup[...], preferred_element_type=F32))
    if has_vres:
        t1 = jnp.dot(xv.astype(BF16), vra[...], preferred_element_type=F32)
        t2 = jnp.dot(t1.astype(BF16), vrb[...], preferred_element_type=F32)
        xv = xv + (vf_ref[...] - xv) * jax.nn.sigmoid(vr0[...] + t2)
    kk = xk * kk_ref[...]
    kk = kk / jnp.maximum(jnp.sqrt(_seg_sum(kk * kk, bd_ref)), 1e-12)
    k2 = xk * (1.0 + (a - 1.0) * ka_ref[...])
    bonus = _seg_sum(xr * k2 * rk_ref[...], bd_ref) * xv
    for o, val in zip(outs, (xr, w, k2, xv, -kk, kk * a, bonus)):
        o[...] = val


def _rwkv_pre(p, prev, lw, vfirst, nb, t, tm):
    shift = prev is None
    has_vres = vfirst is not None
    nt = t // tm
    if shift:
        grid = (nb, nt)
        rmap = lambda c: (lambda b, i: (b * nt + i, c))
        cmap = lambda b, i: (0, 0)
        sem = ("parallel", "arbitrary")
    else:
        grid = (nt,)
        rmap = lambda c: (lambda i: (i, c))
        cmap = lambda i: (0, 0)
        sem = ("parallel",)
    wide = lambda c: pl.BlockSpec((tm, BRANCH), rmap(c))
    vec = lambda n: pl.BlockSpec((1, n), cmap)
    args, specs = [], []
    args += [p, p, p, p]
    specs += [wide(C_RR), wide(C_RK), wide(C_RV), pl.BlockSpec((tm, N_LORA), rmap(C_LORA128))]
    if not shift:
        args += [prev, prev, prev, prev]
        specs += [wide(0), wide(1), wide(2), pl.BlockSpec((tm, N_LORA), rmap(3 * BRANCH // N_LORA))]
    args += lw["mu"]
    specs += [vec(BRANCH)] * 3 + [vec(N_LORA)]
    args += [lw["w0"], lw["w_up"], lw["a0"], lw["a_up"], lw["k_k"], lw["k_a"], lw["r_k"], lw["bd"]]
    specs += [vec(BRANCH), pl.BlockSpec((N_LORA, BRANCH), cmap), vec(BRANCH), pl.BlockSpec((N_LORA, BRANCH), cmap),
              vec(BRANCH), vec(BRANCH), vec(BRANCH), pl.BlockSpec((BRANCH, BRANCH), cmap)]
    if has_vres:
        args += [vfirst, lw["vres_0"], lw["vres_a"], lw["vres_b"]]
        specs += [pl.BlockSpec((tm, BRANCH), rmap(0)), vec(BRANCH),
                  pl.BlockSpec(lw["vres_a"].shape, cmap), pl.BlockSpec(lw["vres_b"].shape, cmap)]
    out_spec = pl.BlockSpec((tm, BRANCH), rmap(0))
    scratch = [pltpu.VMEM((1, BRANCH), F32)] * 3 + [pltpu.VMEM((1, N_LORA), F32)] if shift else []
    return pl.pallas_call(
        functools.partial(_rwkv_pre_kernel, shift=shift, has_vres=has_vres, tm=tm),
        grid=grid, in_specs=specs, out_specs=[out_spec] * 7,
        out_shape=[jax.ShapeDtypeStruct((nb * t, BRANCH), F32)] * 7,
        scratch_shapes=scratch,
        compiler_params=_cparams(sem),
        name="rwkv_pre",
    )(*args)


RWKV_CHUNK = 64

_NN = ((1,), (0,))
_NT = ((1,), (1,))
_TN = ((0,), (0,))


def _split(x):
    hi = x.astype(BF16)
    return hi, (x - hi.astype(F32)).astype(BF16)


def _mm3(a, b, dims):
    dg = lambda x, y: lax.dot_general(x, y, (dims, ((), ())), preferred_element_type=F32)
    return dg(a[0], b[0]) + (dg(a[0], b[1]) + dg(a[1], b[0]))


def _rwkv_scan_kernel(r_ref, lw_ref, k_ref, v_ref, a_ref, b_ref, y_ref, s_ref):
    C = RWKV_CHUNK
    W2 = 2 * R_HEAD

    @pl.when(pl.program_id(1) == 0)
    def _():
        s_ref[...] = jnp.zeros(s_ref.shape, F32)

    lw = lw_ref[...]
    tri = (lax.broadcasted_iota(jnp.int32, (C, C), 0) >= lax.broadcasted_iota(jnp.int32, (C, C), 1)).astype(BF16)
    l1 = lw.astype(BF16)
    rem = lw - l1.astype(F32)
    l2 = rem.astype(BF16)
    l3 = (rem - l2.astype(F32)).astype(BF16)
    cs = (jnp.dot(tri, l1, preferred_element_type=F32) + jnp.dot(tri, l2, preferred_element_type=F32)
          + jnp.dot(tri, l3, preferred_element_type=F32))
    p_in = jnp.exp(cs)
    p_inv = jnp.exp(-cs)
    rt = r_ref[...] * p_in
    at = a_ref[...] * jnp.exp(cs - lw)
    bt = b_ref[...] * p_inv
    kt = k_ref[...] * p_inv
    v = v_ref[...]
    p_last = p_in[C - 1:C, :]

    heads = range(H_RWKV)
    pairs = range(H_RWKV // 2)
    psl = [slice(p * W2, (p + 1) * W2) for p in pairs]
    lane2 = lax.broadcasted_iota(jnp.int32, (2 * C, W2), 1)
    keep = [lane2 < R_HEAD, lane2 >= R_HEAD]
    zb = jnp.zeros((2 * C, W2), BF16)
    gr = lax.broadcasted_iota(jnp.int32, (2 * C, 2 * C), 0)
    gc = lax.broadcasted_iota(jnp.int32, (2 * C, 2 * C), 1) % C
    gmask = jnp.where(gr < C, gr - 1, gr - C) >= gc
    second = lax.broadcasted_iota(jnp.int32, (C, W2), 1) >= R_HEAD

    ar_s = [_split(jnp.concatenate([at[:, s], rt[:, s]], axis=0)) for s in psl]
    bk_s = [_split(jnp.concatenate([bt[:, s], kt[:, s]], axis=0)) for s in psl]
    s0 = [s_ref[p] for p in pairs]
    s0_s = [_split(z) for z in s0]
    mask2 = lambda z, h: tuple(jnp.where(keep[h % 2], x, zb) for x in z[h // 2])
    arm = [mask2(ar_s, h) for h in heads]
    vh = [v[:, h * R_HEAD:(h + 1) * R_HEAD] for h in heads]
    zv_s = [_split(jnp.concatenate([jnp.zeros((C, R_HEAD), F32), z], axis=0)) for z in vh]
    g = [jnp.where(gmask, _mm3(arm[h], bk_s[h // 2], _NT), 0.0) for h in heads]
    x = [_mm3(arm[h], s0_s[h // 2], _NT) for h in heads]
    rhs = [x[h][:C] + _mm3(_split(g[h][:C]), zv_s[h], _NN) for h in heads]
    w = [jnp.concatenate([g[h][:C, :C], rhs[h]], axis=1) for h in heads]
    for _ in range(int(math.log2(C))):
        w_s = [_split(z) for z in w]
        w = [_mm3((w_s[h][0][:, :R_HEAD], w_s[h][1][:, :R_HEAD]), w_s[h], _NN) + jnp.where(second, w[h], 0.0)
             for h in heads]
    uv_s = [_split(jnp.concatenate([w[h][:, R_HEAD:], vh[h]], axis=0)) for h in heads]
    for h in heads:
        y_ref[:, h * R_HEAD:(h + 1) * R_HEAD] = x[h][C:] + _mm3(_split(g[h][C:]), uv_s[h], _NN)
    ds = [_mm3(uv_s[h], mask2(bk_s, h), _TN) for h in heads]
    for p in pairs:
        s_ref[p] = (s0[p] + ds[2 * p] + ds[2 * p + 1]) * p_last[:, psl[p]]


def _rwkv_scan(seqs, nb, t):
    tc = RWKV_CHUNK
    hp = H_RWKV // 2
    blk = pl.BlockSpec((None, tc, BRANCH), lambda b, c: (b, c, 0))
    y, s = pl.pallas_call(
        _rwkv_scan_kernel,
        grid=(nb, t // tc),
        in_specs=[blk] * 6,
        out_specs=[blk, pl.BlockSpec((hp, R_HEAD, 2 * R_HEAD), lambda b, c: (b, 0, 0))],
        out_shape=[jax.ShapeDtypeStruct((nb, t, BRANCH), F32),
                   jax.ShapeDtypeStruct((nb * hp, R_HEAD, 2 * R_HEAD), F32)],
        compiler_params=_cparams(("parallel", "arbitrary")),
        name="rwkv_scan",
    )(*seqs)
    s = s.reshape(nb, hp, R_HEAD, 2, R_HEAD).transpose(0, 1, 3, 2, 4).reshape(nb, H_RWKV, R_HEAD, R_HEAD)
    return y, s


def _rwkv_step_kernel(r_ref, w_ref, k_ref, a_ref, b_ref, vt_ref, s_ref, yt_ref, so_ref, *, bb):
    units = [(j, slice(h * R_HEAD, (h + 1) * R_HEAD)) for j in range(bb) for h in range(H_RWKV)]
    row = lambda ref, j, sl: ref[j:j + 1, sl]
    st = [s_ref[j, sl, :] for j, sl in units]
    sa = [jnp.sum(z * row(a_ref, j, sl), axis=-1, keepdims=True) for z, (j, sl) in zip(st, units)]
    st = [z * row(w_ref, j, sl) + x * row(b_ref, j, sl) + vt_ref[sl, j:j + 1] * row(k_ref, j, sl)
          for z, x, (j, sl) in zip(st, sa, units)]
    for z, (j, sl) in zip(st, units):
        so_ref[j, sl, :] = z
    ys = [jnp.sum(z * row(r_ref, j, sl), axis=-1, keepdims=True) for z, (j, sl) in zip(st, units)]
    for y, (j, sl) in zip(ys, units):
        yt_ref[sl, j:j + 1] = y


def _rwkv_step(r, w, k, na, b, vt, state, layer, bb=8):
    n = r.shape[0]
    rowb = pl.BlockSpec((bb, BRANCH), lambda i: (i, 0))
    colb = pl.BlockSpec((None, BRANCH, bb), lambda i: (i, 0, 0))
    stb = pl.BlockSpec((bb, BRANCH, R_HEAD), lambda i: (i, 0, 0))
    return pl.pallas_call(
        functools.partial(_rwkv_step_kernel, bb=bb),
        grid=(n // bb,),
        in_specs=[rowb] * 5 + [colb, pl.BlockSpec((None, bb, BRANCH, R_HEAD), lambda i: (layer, i, 0, 0))],
        out_specs=[colb, stb],
        out_shape=[jax.ShapeDtypeStruct((n // bb, BRANCH, bb), F32),
                   jax.ShapeDtypeStruct((n, BRANCH, R_HEAD), F32)],
        compiler_params=_cparams(("parallel",)),
        name="rwkv_step",
    )(r, w, k, na, b, vt, state)


def _rwkv_post_kernel(y_ref, bonus_ref, g_ref, lg_ref, lb_ref, bd_ref, o_ref):
    y = y_ref[...]
    mu = _seg_sum(y, bd_ref) * (1.0 / R_HEAD)
    d = y - mu
    var = _seg_sum(d * d, bd_ref) * (1.0 / R_HEAD)
    o = d * lax.rsqrt(var + LNX_EPS) * lg_ref[...] + lb_ref[...] + bonus_ref[...]
    o_ref[...] = (o * _silu(g_ref[...])).astype(o_ref.dtype)


def _rwkv_post(y, bonus, p, lw, tm):
    m = y.shape[0]
    blk = pl.BlockSpec((tm, BRANCH), lambda i: (i, 0))
    vec = pl.BlockSpec((1, BRANCH), lambda i: (0, 0))
    return pl.pallas_call(
        _rwkv_post_kernel,
        grid=(m // tm,),
        in_specs=[blk, blk, pl.BlockSpec((tm, BRANCH), lambda i: (i, C_RG)), vec, vec,
                  pl.BlockSpec((BRANCH, BRANCH), lambda i: (0, 0))],
        out_specs=blk,
        out_shape=jax.ShapeDtypeStruct((m, BRANCH), BF16),
        compiler_params=_cparams(("parallel",)),
        name="rwkv_post",
    )(y, bonus, p, lw["lnx_g"], lw["lnx_b"], lw["bd"])


def _softplus(x):
    return jnp.maximum(x, 0.0) + jnp.log1p(jnp.exp(-jnp.abs(x)))


def _neg_exp_alog(alog_ref):
    lanes = lax.broadcasted_iota(jnp.int32, alog_ref.shape, 1)
    return jnp.where(lanes < H_MAMBA, -jnp.exp(alog_ref[...]), 0.0)


def _gated_norm(ym, z, g):
    u = ym * _silu(z)
    ms = jnp.mean(u * u, axis=-1, keepdims=True)
    return u * lax.rsqrt(ms + RMS_EPS) * g


def _mamba_prompt_kernel(mx_ref, mbc_ref, mdt_ref, mz_ref, cwx_ref, cwb_ref, cbx_ref, cbb_ref, dtb_ref, alog_ref,
                         dskip_ref, gn_ref, o_ref, s_ref, extx_ref, extb_ref, ym_ref):
    L = SSD_CHUNK
    pad = 8

    @pl.when(pl.program_id(1) == 0)
    def _():
        extx_ref[0:pad, :] = jnp.zeros((pad, BRANCH), F32)
        extb_ref[0:pad, :] = jnp.zeros((pad, BRANCH), F32)
        s_ref[...] = jnp.zeros(s_ref.shape, F32)

    def conv(ext_ref, cur_ref, cw_ref, cb_ref):
        ext_ref[pad:pad + L, :] = cur_ref[...]
        acc = cb_ref[...]
        for j in range(M_CONV):
            lo = pad - (M_CONV - 1) + j
            acc = acc + ext_ref[lo:lo + L, :] * cw_ref[j:j + 1, :]
        ext_ref[0:pad, :] = ext_ref[L:L + pad, :]
        return _silu(acc)

    x = conv(extx_ref, mx_ref, cwx_ref, cbx_ref)
    bc = conv(extb_ref, mbc_ref, cwb_ref, cbb_ref)
    dt = _softplus(mdt_ref[...] + dtb_ref[...])
    adt = dt * _neg_exp_alog(alog_ref)
    rowi = lax.broadcasted_iota(jnp.int32, (L, L), 0)
    coli = lax.broadcasted_iota(jnp.int32, (L, L), 1)
    lower = rowi >= coli
    acs = jnp.dot(lower.astype(F32), adt, preferred_element_type=F32, precision=HIGHEST)
    acs_t = acs.T
    last = acs[L - 1:L, :]
    dstate = jnp.exp(last - acs)
    eacs = jnp.exp(acs)
    elast = jnp.exp(last)
    bcb = bc.astype(BF16)
    grp = []
    for g in range(M_GROUPS):
        bg = bcb[:, g * M_STATE:(g + 1) * M_STATE]
        cg = bcb[:, (M_GROUPS + g) * M_STATE:(M_GROUPS + g + 1) * M_STATE]
        grp.append((bg, cg, _dot_t(cg, bg)))
    for h in range(H_MAMBA):
        bg, cg, cb = grp[h // (H_MAMBA // M_GROUPS)]
        sl = slice(h * M_HEADDIM, (h + 1) * M_HEADDIM)
        decay = jnp.where(lower, jnp.exp(acs[:, h:h + 1] - acs_t[h:h + 1, :]), 0.0)
        xh = x[:, sl]
        xdt = xh * dt[:, h:h + 1]
        yd = jnp.dot((cb * decay).astype(BF16), xdt.astype(BF16), preferred_element_type=F32)
        st = s_ref[h]
        yo = _dot_t(cg, st.astype(BF16)) * eacs[:, h:h + 1]
        upd = lax.dot_general((xdt * dstate[:, h:h + 1]).astype(BF16), bg, (((0,), (0,)), ((), ())),
                              preferred_element_type=F32)
        s_ref[h] = st * elast[:, h:h + 1] + upd
        ym_ref[:, sl] = yd + yo + dskip_ref[:, sl] * xh
    o_ref[...] = _gated_norm(ym_ref[...], mz_ref[...], gn_ref[...]).astype(o_ref.dtype)


def _mamba_prompt(p, lw, nb, t):
    L = SSD_CHUNK
    nt = t // L
    rmap = lambda c: (lambda b, i: (b * nt + i, c))
    cmap = lambda b, i: (0, 0)
    wide = lambda c: pl.BlockSpec((L, BRANCH), rmap(c))
    vec = lambda n: pl.BlockSpec((1, n), cmap)
    cw = pl.BlockSpec((M_CONV, BRANCH), cmap)
    return pl.pallas_call(
        _mamba_prompt_kernel,
        grid=(nb, nt),
        in_specs=[wide(C_MX), wide(C_MBC), pl.BlockSpec((L, 128), rmap(C_DT128)), wide(C_MZ),
                  cw, cw, vec(BRANCH), vec(BRANCH), vec(128), vec(128), vec(BRANCH), vec(BRANCH)],
        out_specs=[pl.BlockSpec((L, BRANCH), rmap(0)),
                   pl.BlockSpec((None, H_MAMBA, M_HEADDIM, M_STATE), lambda b, i: (b, 0, 0, 0))],
        out_shape=[jax.ShapeDtypeStruct((nb * t, BRANCH), BF16),
                   jax.ShapeDtypeStruct((nb, H_MAMBA, M_HEADDIM, M_STATE), F32)],
        scratch_shapes=[pltpu.VMEM((L + 8, BRANCH), F32), pltpu.VMEM((L + 8, BRANCH), F32),
                        pltpu.VMEM((L, BRANCH), F32)],
        compiler_params=_cparams(("parallel", "arbitrary")),
        name="mamba_prompt",
    )(p, p, p, p, lw["cw_x"], lw["cw_bc"], lw["cb_x"], lw["cb_bc"], lw["dt_bias"], lw["a_log"],
      lw["d_skip"], lw["gnorm_g"])


def _mamba_dec_pre_kernel(mx_ref, mbc_ref, mdt_ref, sx_ref, sb_ref, cwx_ref, cwb_ref, cbx_ref, cbb_ref,
                          dtb_ref, alog_ref, x_ref, bc_ref, xdt_ref, da_ref):
    def conv(st_ref, cur_ref, cw_ref, cb_ref):
        acc = cb_ref[...]
        for j in range(M_CONV - 1):
            acc = acc + st_ref[j] * cw_ref[j:j + 1, :]
        acc = acc + cur_ref[...] * cw_ref[M_CONV - 1:M_CONV, :]
        return _silu(acc)

    x = conv(sx_ref, mx_ref, cwx_ref, cbx_ref)
    x_ref[...] = x
    bc_ref[...] = conv(sb_ref, mbc_ref, cwb_ref, cbb_ref)
    dt = _softplus(mdt_ref[...] + dtb_ref[...])
    da = jnp.exp(dt * _neg_exp_alog(alog_ref))
    n = x.shape[0]
    for h in range(H_MAMBA):
        sl = slice(h * M_HEADDIM, (h + 1) * M_HEADDIM)
        xdt_ref[:, sl] = x[:, sl] * dt[:, h:h + 1]
        da_ref[:, sl] = jnp.broadcast_to(da[:, h:h + 1], (n, M_HEADDIM))


def _mamba_dec_pre(p, conv_x, conv_bc, lw):
    n = p.shape[0]
    cmap = lambda i: (0, 0)
    wide = lambda c: pl.BlockSpec((n, BRANCH), lambda i: (0, c))
    vec = lambda m: pl.BlockSpec((1, m), cmap)
    cw = pl.BlockSpec((M_CONV, BRANCH), cmap)
    st = pl.BlockSpec((M_CONV - 1, n, BRANCH), lambda i: (0, 0, 0))
    out = pl.BlockSpec((n, BRANCH), cmap)
    return pl.pallas_call(
        _mamba_dec_pre_kernel,
        grid=(1,),
        in_specs=[wide(C_MX), wide(C_MBC), pl.BlockSpec((n, 128), lambda i: (0, C_DT128)), st, st,
                  cw, cw, vec(BRANCH), vec(BRANCH), vec(128), vec(128)],
        out_specs=[out] * 4,
        out_shape=[jax.ShapeDtypeStruct((n, BRANCH), F32)] * 4,
        compiler_params=_cparams(("arbitrary",)),
        name="mamba_dec_pre",
    )(p, p, p, conv_x, conv_bc, lw["cw_x"], lw["cw_bc"], lw["cb_x"], lw["cb_bc"], lw["dt_bias"], lw["a_log"])


def _mamba_step_kernel(bc_ref, xdtt_ref, dat_ref, s_ref, yt_ref, so_ref, *, bb):
    units = [(j, h // (H_MAMBA // M_GROUPS), slice(h * M_HEADDIM, (h + 1) * M_HEADDIM))
             for j in range(bb) for h in range(H_MAMBA)]
    st = []
    for j, g, sl in units:
        bg = bc_ref[j:j + 1, g * M_STATE:(g + 1) * M_STATE]
        z = s_ref[j, sl, :] * dat_ref[sl, j:j + 1] + xdtt_ref[sl, j:j + 1] * bg
        so_ref[j, sl, :] = z
        st.append(z)
    ys = [jnp.sum(z * bc_ref[j:j + 1, (M_GROUPS + g) * M_STATE:(M_GROUPS + g + 1) * M_STATE], axis=-1, keepdims=True)
          for z, (j, g, sl) in zip(st, units)]
    for y, (j, g, sl) in zip(ys, units):
        yt_ref[sl, j:j + 1] = y


def _mamba_step(bc, xdtt, dat, state, layer, bb=8):
    n = bc.shape[0]
    rowb = pl.BlockSpec((bb, BRANCH), lambda i: (i, 0))
    colb = pl.BlockSpec((None, BRANCH, bb), lambda i: (i, 0, 0))
    stb = pl.BlockSpec((bb, BRANCH, M_STATE), lambda i: (i, 0, 0))
    return pl.pallas_call(
        functools.partial(_mamba_step_kernel, bb=bb),
        grid=(n // bb,),
        in_specs=[rowb, colb, colb, pl.BlockSpec((None, bb, BRANCH, M_STATE), lambda i: (layer, i, 0, 0))],
        out_specs=[colb, stb],
        out_shape=[jax.ShapeDtypeStruct((n // bb, BRANCH, bb), F32),
                   jax.ShapeDtypeStruct((n, BRANCH, M_STATE), F32)],
        compiler_params=_cparams(("parallel",)),
        name="mamba_step",
    )(bc, xdtt, dat, state)


def _mamba_dec_post_kernel(y_ref, x_ref, z_ref, dskip_ref, gn_ref, o_ref):
    ym = y_ref[...] + dskip_ref[...] * x_ref[...]
    o_ref[...] = _gated_norm(ym, z_ref[...], gn_ref[...]).astype(o_ref.dtype)


def _mamba_dec_post(y, x, p, lw):
    n = y.shape[0]
    blk = pl.BlockSpec((n, BRANCH), lambda i: (0, 0))
    vec = pl.BlockSpec((1, BRANCH), lambda i: (0, 0))
    return pl.pallas_call(
        _mamba_dec_post_kernel,
        grid=(1,),
        in_specs=[blk, blk, pl.BlockSpec((n, BRANCH), lambda i: (0, C_MZ)), vec, vec],
        out_specs=blk,
        out_shape=jax.ShapeDtypeStruct((n, BRANCH), BF16),
        compiler_params=_cparams(("arbitrary",)),
        name="mamba_dec_post",
    )(y, x, p, lw["d_skip"], lw["gnorm_g"])


def _xattn_prompt_kernel(q_ref, g_ref, mk_ref, mv_ref, o_ref):
    q = q_ref[...].astype(BF16)
    mk = mk_ref[...].astype(BF16)
    mv = mv_ref[...].astype(BF16)
    for h in range(H_X):
        sl = slice(h * D_X, (h + 1) * D_X)
        s = _dot_t(q[:, sl], mk[:, sl]) * (D_X ** -0.5)
        e = jnp.exp(s - jnp.max(s, axis=-1, keepdims=True))
        o = jnp.dot(e.astype(BF16), mv[:, sl], preferred_element_type=F32) / jnp.sum(e, axis=-1, keepdims=True)
        o_ref[:, sl] = (o * _silu(g_ref[:, sl])).astype(o_ref.dtype)


def _xattn_prompt(p, mkv, nb, t):
    tq = min(512, t)
    nq = t // tq
    return pl.pallas_call(
        _xattn_prompt_kernel,
        grid=(nb, nq),
        in_specs=[pl.BlockSpec((tq, BRANCH), lambda b, i: (b * nq + i, C_XQ)),
                  pl.BlockSpec((tq, BRANCH), lambda b, i: (b * nq + i, C_XG)),
                  pl.BlockSpec((N_MEM, BRANCH), lambda b, i: (b, 0)),
                  pl.BlockSpec((N_MEM, BRANCH), lambda b, i: (b, 1))],
        out_specs=pl.BlockSpec((tq, BRANCH), lambda b, i: (b * nq + i, 0)),
        out_shape=jax.ShapeDtypeStruct((nb * t, BRANCH), BF16),
        compiler_params=_cparams(("parallel", "parallel")),
        name="xattn_prompt",
    )(p, p, mkv, mkv)


def _xattn_decode_kernel(q_ref, g_ref, mk_ref, mv_ref, o_ref, *, bb):
    nrow = 8
    units = [(j, h) for j in range(bb) for h in range(H_X)]
    q = [q_ref[j].astype(BF16) for j in range(bb)]
    s = [_dot_t(jnp.broadcast_to(q[j][:, h * D_X:(h + 1) * D_X], (nrow, D_X)),
                mk_ref[j, pl.ds(h, N_MEM, stride=H_X), :].astype(BF16)) * (D_X ** -0.5) for j, h in units]
    e = [jnp.exp(z - jnp.max(z, axis=-1, keepdims=True)) for z in s]
    r = [jnp.dot(z.astype(BF16), mv_ref[j, pl.ds(h, N_MEM, stride=H_X), :].astype(BF16),
                 preferred_element_type=F32) for z, (j, h) in zip(e, units)]
    outs = [(x / jnp.sum(z, axis=-1, keepdims=True))[0:1, :] for x, z in zip(r, e)]
    for j in range(bb):
        o = jnp.concatenate(outs[j * H_X:(j + 1) * H_X], axis=-1)
        o_ref[j] = (o * _silu(g_ref[j])).astype(o_ref.dtype)


def _xattn_decode(p3, layer, mem_k, mem_v, bb=4):
    n = p3.shape[0]
    mem = pl.BlockSpec((None, bb, N_MEM * H_X, D_X), lambda b: (layer, b, 0, 0))
    return pl.pallas_call(
        functools.partial(_xattn_decode_kernel, bb=bb),
        grid=(n // bb,),
        in_specs=[pl.BlockSpec((bb, 1, BRANCH), lambda b: (b, 0, C_XQ)),
                  pl.BlockSpec((bb, 1, BRANCH), lambda b: (b, 0, C_XG)), mem, mem],
        out_specs=pl.BlockSpec((bb, 1, BRANCH), lambda b: (b, 0, 0)),
        out_shape=jax.ShapeDtypeStruct((n, 1, BRANCH), BF16),
        compiler_params=_cparams(("parallel",)),
        name="xattn_decode",
    )(p3, p3, mem_k, mem_v)


def _layer_weights(l, prm):
    w = prm["w_in"][l].astype(BF16)
    o_rs, o_rg, o_mz, o_mxbc, o_dt, o_xq, o_xg = 2048, 3712, 4224, 4736, 5760, 5768, 6280
    w_perm = jnp.concatenate([
        w[:, :o_rs + 3 * BRANCH], w[:, o_rg:o_mz], w[:, o_mz:o_mxbc], w[:, o_xq:o_xg], w[:, o_xg:],
        w[:, o_mxbc:o_dt], w[:, o_rs + 3 * BRANCH:o_rg], w[:, o_dt:o_xq],
        jnp.zeros((D_MODEL, 128 - H_MAMBA), w.dtype)], axis=1).astype(BF16)
    row = lambda v: v.reshape(1, -1).astype(F32)
    mu = prm["shift_mu"][l].astype(F32)
    seg = jnp.arange(BRANCH) // R_HEAD
    zl = jnp.zeros((N_LORA // 2, BRANCH), F32)
    cw = prm["conv_w"][l].astype(F32)
    cb = prm["conv_b"][l].astype(F32)
    pad8 = lambda v: jnp.pad(v.astype(F32), (0, 128 - H_MAMBA)).reshape(1, 128)
    lw = {
        "w_in": w_perm, "norm_g": prm["norm_g"][l], "w_out": prm["w_out"][l].astype(BF16),
        "lam": [row(prm[n][l]) for n in ("lam_q1", "lam_k1", "lam_q2", "lam_k2")],
        "subln_g": prm["subln_g"][l].astype(F32),
        "mu": [row(mu[:BRANCH]), row(mu[BRANCH:2 * BRANCH]), row(mu[2 * BRANCH:3 * BRANCH]), row(mu[3 * BRANCH:])],
        "w0": row(prm["w0"][l]), "a0": row(prm["a0"][l]),
        "w_up": jnp.concatenate([prm["w_up"][l].astype(F32), zl], axis=0).astype(BF16),
        "a_up": jnp.concatenate([zl, prm["a_up"][l].astype(F32)], axis=0).astype(BF16),
        "k_k": row(prm["k_k"][l]), "k_a": row(prm["k_a"][l]), "r_k": row(prm["r_k"][l]),
        "lnx_g": row(prm["lnx_g"][l]), "lnx_b": row(prm["lnx_b"][l]),
        "bd": (seg[:, None] == seg[None, :]).astype(F32),
        "cw_x": cw[:, :BRANCH], "cw_bc": cw[:, BRANCH:], "cb_x": row(cb[:BRANCH]), "cb_bc": row(cb[BRANCH:]),
        "dt_bias": pad8(prm["dt_bias"][l]), "a_log": pad8(prm["a_log"][l]),
        "d_skip": row(jnp.repeat(prm["d_skip"][l].astype(F32), M_HEADDIM)),
        "gnorm_g": row(prm["gnorm_g"][l]),
        "w_mkv": jnp.concatenate([prm["w_mk"][l], prm["w_mv"][l]], axis=1).astype(BF16),
        "mem_norm_g": prm["mem_norm_g"][l],
    }
    if l > 0:
        lw["vres_0"] = row(prm["vres_0"][l - 1])
        lw["vres_a"] = prm["vres_a"][l - 1].astype(BF16)
        lw["vres_b"] = prm["vres_b"][l - 1].astype(BF16)
    return lw


def _lam_init(l):
    return 0.8 - 0.6 * math.exp(-0.3 * l)


def _prompt_layer(x, l, lw, mem, vfirst, nb, t, fg, final, tm_proj, tm_row):
    p = _rms_matmul(x, lw["norm_g"], lw["w_in"], tm_proj, 1152)
    mkv = _rms_matmul(mem, lw["mem_norm_g"], lw["w_mkv"], mem.shape[0], 2 * BRANCH)
    out_a = _diff_prompt(p, lw["lam"], lw["subln_g"], nb, t, _lam_init(l))
    r, w, k, v, na, b, bonus = _rwkv_pre(p, None, lw, vfirst, nb, t, tm_row)
    seqs = [z.reshape(nb, t, BRANCH) for z in (r, w, k, v, na, b)]
    y, rw_state = _rwkv_scan(seqs, nb, t)
    out_b = _rwkv_post(y.reshape(nb * t, BRANCH), bonus, p, lw, tm_row)
    out_m, ssm_state = _mamba_prompt(p, lw, nb, t)
    out_x = _xattn_prompt(p, mkv, nb, t)
    x = _outproj((out_a, out_b, out_m, out_x), lw["w_out"], x, fg, tm_row, final)
    return x, p, mkv, rw_state, ssm_state, (v if vfirst is None else vfirst)


def _decode_layer(x, l, lw, caches, vfirst, fg, final):
    cache_k, cache_v, mem_k, mem_v, page_table, rw_prev, rw_state, conv_state, ssm_state = caches
    n = x.shape[0]
    bb = 8
    p = _rms_matmul(x, lw["norm_g"], lw["w_in"], n, 1152)
    p3 = p.reshape(n, 1, N_PROJ)
    out_a = _diff_decode(p3, l, cache_k, cache_v, page_table, lw["lam"], lw["subln_g"], _lam_init(l)).reshape(n, BRANCH)

    r, w, k, v, na, b, bonus = _rwkv_pre(p, rw_prev, lw, vfirst, 1, n, n)
    to_cols = lambda z: z.reshape(n // bb, bb, BRANCH).transpose(0, 2, 1)
    from_cols = lambda z: z.transpose(0, 2, 1).reshape(n, BRANCH)
    yt, rw_new = _rwkv_step(r, w, k, na, b, to_cols(v), rw_state.reshape(-1, n, BRANCH, R_HEAD), l, bb)
    out_b = _rwkv_post(from_cols(yt), bonus, p, lw, n)

    cs = conv_state.transpose(1, 0, 2)
    xm, bc, xdt, da = _mamba_dec_pre(p, cs[:, :, :BRANCH], cs[:, :, BRANCH:], lw)
    ymt, ssm_new = _mamba_step(bc, to_cols(xdt), to_cols(da), ssm_state.reshape(-1, n, BRANCH, M_STATE), l, bb)
    out_m = _mamba_dec_post(from_cols(ymt), xm, p, lw)

    out_x = _xattn_decode(p3, l, mem_k, mem_v).reshape(n, BRANCH)
    x = _outproj((out_a, out_b, out_m, out_x), lw["w_out"], x, fg, n, final)
    return x, p, rw_new, ssm_new, (v if vfirst is None else vfirst)


def _cols(p, c, width=BRANCH):
    return p[..., c * width:(c + 1) * width]


def kernel(x_prompt, x_sample, cache_diff_k, cache_diff_v, cache_mem_k, cache_mem_v, state_rwkv_shift, state_rwkv, state_conv, state_ssm, page_table, mem_prompt, norm_g, w_in, w_out, lam_q1, lam_k1, lam_q2, lam_k2, subln_g, shift_mu, w0, w_up, a0, a_up, k_k, k_a, r_k, lnx_g, lnx_b, vres_0, vres_a, vres_b, conv_w, conv_b, dt_bias, a_log, d_skip, gnorm_g, mem_norm_g, w_mk, w_mv, final_g):
    prm = dict(norm_g=norm_g, w_in=w_in, w_out=w_out, lam_q1=lam_q1, lam_k1=lam_k1, lam_q2=lam_q2, lam_k2=lam_k2,
               subln_g=subln_g, shift_mu=shift_mu, w0=w0, w_up=w_up, a0=a0, a_up=a_up, k_k=k_k, k_a=k_a, r_k=r_k,
               lnx_g=lnx_g, lnx_b=lnx_b, vres_0=vres_0, vres_a=vres_a, vres_b=vres_b, conv_w=conv_w, conv_b=conv_b,
               dt_bias=dt_bias, a_log=a_log, d_skip=d_skip, gnorm_g=gnorm_g, mem_norm_g=mem_norm_g,
               w_mk=w_mk, w_mv=w_mv)
    depth = w_in.shape[0]
    nb, t, _ = x_prompt.shape
    n = x_sample.shape[0]
    n_pool = cache_diff_k.shape[1]
    ck = jnp.transpose(cache_diff_k, (0, 1, 3, 4, 5, 2)).reshape(depth, n_pool, BRANCH, PAGE_SIZE)
    cv = cache_diff_v.reshape(depth, n_pool, PAGE_SIZE * H_DIFF, DV)
    cmk = cache_mem_k.reshape(depth, n, N_MEM * H_X, D_X)
    cmv = cache_mem_v.reshape(depth, n, N_MEM * H_X, D_X)
    mem = mem_prompt.reshape(nb * N_MEM, D_MODEL)

    xp = x_prompt.reshape(nb * t, D_MODEL)
    xs = x_sample.reshape(n, D_MODEL)
    vf_p = vf_s = None
    outs = [[] for _ in range(14)]
    for l in range(depth):
        lw = _layer_weights(l, prm)
        final = l == depth - 1
        xp, p, mkv, rw_p, ssm_p, vf_p = _prompt_layer(xp, l, lw, mem, vf_p, nb, t, final_g, final,
                                                      min(1024, nb * t), min(512, t))
        p3 = p.reshape(nb, t, N_PROJ)
        last = p3[:, t - 1]
        outs[0].append(_cols(p3, C_DK).reshape(nb, t, H_DIFF, 2, DQK))
        outs[1].append(_cols(p3, C_DV).reshape(nb, t, H_DIFF, DV))
        outs[2].append(mkv[:, :BRANCH].reshape(nb, N_MEM, H_X, D_X))
        outs[3].append(mkv[:, BRANCH:].reshape(nb, N_MEM, H_X, D_X))
        outs[4].append(jnp.concatenate([_cols(last, C_RR), _cols(last, C_RK), _cols(last, C_RV),
                                        _cols(last, C_LORA128, 128)], axis=-1))
        outs[5].append(rw_p.reshape(nb, H_RWKV, R_HEAD, R_HEAD))
        tail = p3[:, t - (M_CONV - 1):]
        outs[6].append(jnp.concatenate([_cols(tail, C_MX), _cols(tail, C_MBC)], axis=-1))
        outs[7].append(ssm_p)

        caches = (ck, cv, cmk, cmv, page_table, state_rwkv_shift[l], state_rwkv, state_conv[l], state_ssm)
        xs, ps, rw_s, ssm_s, vf_s = _decode_layer(xs, l, lw, caches, vf_s, final_g, final)
        outs[8].append(_cols(ps, C_DK).reshape(n, 1, H_DIFF, 2, DQK))
        outs[9].append(_cols(ps, C_DV).reshape(n, 1, H_DIFF, DV))
        outs[10].append(jnp.concatenate([_cols(ps, C_RR), _cols(ps, C_RK), _cols(ps, C_RV),
                                         _cols(ps, C_LORA128, 128)], axis=-1))
        outs[11].append(rw_s.reshape(n, H_RWKV, R_HEAD, R_HEAD))
        outs[12].append(jnp.concatenate(
            [state_conv[l][:, 1:], jnp.concatenate([_cols(ps, C_MX), _cols(ps, C_MBC)], axis=-1)[:, None]], axis=1))
        outs[13].append(ssm_s.reshape(n, H_MAMBA, M_HEADDIM, M_STATE))

    return (xp.reshape(nb, t, D_MODEL), xs.reshape(n, 1, D_MODEL)) + tuple(jnp.stack(o) for o in outs)
```

```python
import functools
import math

import jax
import jax.numpy as jnp
from jax import lax
from jax.experimental import pallas as pl
from jax.experimental.pallas import tpu as pltpu

F32 = jnp.float32
BF16 = jnp.bfloat16
HIGHEST = lax.Precision.HIGHEST

D_MODEL = 2048
BRANCH = 512
H_DIFF = 4
DQK = 64
DV = 128
R_HEAD = 64
H_RWKV = 8
N_LORA = 128
N_SHIFT = 3 * BRANCH + N_LORA
DECAY_SCALE = 0.6065306597
LNX_EPS = 64e-5
M_HEADDIM = 64
H_MAMBA = 8
M_GROUPS = 2
M_STATE = 128
M_CONV = 4
SSD_CHUNK = 128
H_X = 4
D_X = 128
N_MEM = 256
RMS_EPS = 1e-5
PAGE_SIZE = 128

C_DQ, C_DK, C_DV, C_DG, C_RR, C_RK, C_RV, C_RG, C_MZ, C_XQ, C_XG, C_MX, C_MBC = range(13)
C_LORA128 = 13 * BRANCH // 128
C_DT128 = C_LORA128 + 1
N_PROJ = 13 * BRANCH + 2 * 128

VMEM_LIMIT = 56 * 1024 * 1024


def _cparams(sem):
    return pltpu.CompilerParams(dimension_semantics=sem, vmem_limit_bytes=VMEM_LIMIT)


def _silu(x):
    return x * jax.nn.sigmoid(x)


def _dot_t(a, b):
    return lax.dot_general(a, b, (((1,), (1,)), ((), ())), preferred_element_type=F32)


def _rms_matmul_kernel(x_ref, g_ref, w_ref, o_ref, h_ref):
    @pl.when(pl.program_id(1) == 0)
    def _():
        x = x_ref[...]
        ms = jnp.mean(x * x, axis=-1, keepdims=True)
        h_ref[...] = (x * lax.rsqrt(ms + RMS_EPS) * g_ref[...]).astype(BF16)

    o_ref[...] = jnp.dot(h_ref[...], w_ref[...], preferred_element_type=F32)


def _rms_matmul(x, g, w, tm, tn):
    m, k = x.shape
    n = w.shape[1]
    return pl.pallas_call(
        _rms_matmul_kernel,
        grid=(m // tm, n // tn),
        in_specs=[pl.BlockSpec((tm, k), lambda i, j: (i, 0)),
                  pl.BlockSpec((1, k), lambda i, j: (0, 0)),
                  pl.BlockSpec((k, tn), lambda i, j: (0, j))],
        out_specs=pl.BlockSpec((tm, tn), lambda i, j: (i, j)),
        out_shape=jax.ShapeDtypeStruct((m, n), F32),
        scratch_shapes=[pltpu.VMEM((tm, k), BF16)],
        compiler_params=_cparams(("parallel", "arbitrary")),
        name="rms_matmul",
    )(x, g.reshape(1, k), w)


def _outproj_kernel(a_ref, b_ref, m_ref, c_ref, w_ref, x_ref, fg_ref, o_ref, *, final):
    acc = x_ref[...]
    for g, r in enumerate((a_ref, b_ref, m_ref, c_ref)):
        acc = acc + jnp.dot(r[...], w_ref[g * BRANCH:(g + 1) * BRANCH, :], preferred_element_type=F32)
    if final:
        ms = jnp.mean(acc * acc, axis=-1, keepdims=True)
        acc = acc * lax.rsqrt(ms + RMS_EPS) * fg_ref[...]
    o_ref[...] = acc


def _outproj(parts, w, x, fg, tm, final):
    m = x.shape[0]
    part_spec = pl.BlockSpec((tm, BRANCH), lambda i: (i, 0))
    return pl.pallas_call(
        functools.partial(_outproj_kernel, final=final),
        grid=(m // tm,),
        in_specs=[part_spec] * 4 + [pl.BlockSpec((D_MODEL, D_MODEL), lambda i: (0, 0)),
                                    pl.BlockSpec((tm, D_MODEL), lambda i: (i, 0)),
                                    pl.BlockSpec((1, D_MODEL), lambda i: (0, 0))],
        out_specs=pl.BlockSpec((tm, D_MODEL), lambda i: (i, 0)),
        out_shape=jax.ShapeDtypeStruct((m, D_MODEL), F32),
        compiler_params=_cparams(("parallel",)),
        name="outproj",
    )(*parts, w, x, fg.reshape(1, D_MODEL))


def _alibi_slope(h):
    return 2.0 ** (-8.0 * (h + 1) / H_DIFF)


def _lambda(lq1, lk1, lq2, lk2, lam_init):
    s1 = jnp.sum(lq1[...] * lk1[...], axis=-1, keepdims=True)
    s2 = jnp.sum(lq2[...] * lk2[...], axis=-1, keepdims=True)
    return jnp.exp(s1) - jnp.exp(s2) + lam_init


def _subln_gate(o, subg, gate, lam_init):
    ms = jnp.mean(o * o, axis=-1, keepdims=True)
    o = (o * lax.rsqrt(ms + RMS_EPS) * subg) * (1.0 - lam_init)
    return o * _silu(gate)


def _diff_prompt_kernel(lq1, lk1, lq2, lk2, subg_ref, q_ref, k_ref, v_ref, g_ref, o_ref,
                        m_ref, l_ref, acc_ref, *, tq, lam_init):
    qi = pl.program_id(1)
    ki = pl.program_id(2)

    @pl.when(ki == 0)
    def _():
        m_ref[...] = jnp.full(m_ref.shape, -1e30, F32)
        l_ref[...] = jnp.zeros(l_ref.shape, F32)
        acc_ref[...] = jnp.zeros(acc_ref.shape, F32)

    def block(masked):
        lane = lax.broadcasted_iota(jnp.int32, (tq, 2 * DQK), 1)
        slot = lane % DQK
        rows = lax.broadcasted_iota(jnp.int32, (tq, 2 * DQK), 0)
        lo = (rows % 256).astype(F32)
        hi = (rows // 256).astype(F32)
        off = ((qi - ki) * tq).astype(F32)
        zero = jnp.zeros((tq, 2 * DQK), F32)
        fq = jnp.where(slot == 0, -lo, jnp.where(slot == 1, -256.0 * hi, jnp.where(
            slot == 2, 1.0, jnp.where(slot == 3, 256.0, jnp.where(slot == 4, -off, zero)))))
        fk = jnp.where(slot == 0, 1.0, jnp.where(slot == 1, 1.0, jnp.where(
            slot == 2, lo, jnp.where(slot == 3, hi, jnp.where(slot == 4, 1.0, zero))))).astype(BF16)
        first = lane < DQK
        if masked:
            causal = (lax.broadcasted_iota(jnp.int32, (tq, tq), 0) >= lax.broadcasted_iota(jnp.int32, (tq, tq), 1))
        q = (q_ref[...] * (DQK ** -0.5)).astype(BF16)
        k = k_ref[...].astype(BF16)
        v = v_ref[...].astype(BF16)
        for h in range(H_DIFF):
            hs = slice(h * DV, (h + 1) * DV)
            fqh = (fq * _alibi_slope(h)).astype(BF16)
            qh, kh, vh = q[:, hs], k[:, hs], v[:, hs]
            for c in range(2):
                j = 2 * h + c
                keep = first if c == 0 else jnp.logical_not(first)
                s = _dot_t(jnp.where(keep, qh, fqh), jnp.where(keep, kh, fk))
                if masked:
                    s = jnp.where(causal, s, -jnp.inf)
                m_prev = m_ref[j]
                m_new = jnp.maximum(m_prev, jnp.max(s, axis=-1, keepdims=True))
                alpha = jnp.exp(m_prev - m_new)
                p = jnp.exp(s - jnp.concatenate([m_new] * (tq // DV), axis=1))
                l_ref[j] = alpha * l_ref[j] + jnp.sum(p, axis=-1, keepdims=True)
                acc_ref[j] = alpha * acc_ref[j] + jnp.dot(p.astype(BF16), vh, preferred_element_type=F32)
                m_ref[j] = m_new

    @pl.when(ki < qi)
    def _():
        block(False)

    @pl.when(ki == qi)
    def _():
        block(True)
        lam = _lambda(lq1, lk1, lq2, lk2, lam_init)
        for h in range(H_DIFF):
            o = acc_ref[2 * h] / l_ref[2 * h] - lam * (acc_ref[2 * h + 1] / l_ref[2 * h + 1])
            sl = slice(h * DV, (h + 1) * DV)
            o_ref[:, sl] = _subln_gate(o, subg_ref[...], g_ref[:, sl], lam_init).astype(o_ref.dtype)


def _diff_prompt(p, lam_params, subg, nb, t, lam_init):
    tq = min(512, t)
    nq = t // tq
    vec = pl.BlockSpec((1, DQK), lambda b, i, j: (0, 0))

    def qmap(c):
        return lambda b, i, j: (b * nq + i, c)

    def kmap(c):
        return lambda b, i, j: (b * nq + jnp.minimum(i, j), c)

    return pl.pallas_call(
        functools.partial(_diff_prompt_kernel, tq=tq, lam_init=lam_init),
        grid=(nb, nq, nq),
        in_specs=[vec, vec, vec, vec, pl.BlockSpec((1, DV), lambda b, i, j: (0, 0)),
                  pl.BlockSpec((tq, BRANCH), qmap(C_DQ)),
                  pl.BlockSpec((tq, BRANCH), kmap(C_DK)),
                  pl.BlockSpec((tq, BRANCH), kmap(C_DV)),
                  pl.BlockSpec((tq, BRANCH), qmap(C_DG))],
        out_specs=pl.BlockSpec((tq, BRANCH), lambda b, i, j: (b * nq + i, 0)),
        out_shape=jax.ShapeDtypeStruct((nb * t, BRANCH), BF16),
        scratch_shapes=[pltpu.VMEM((2 * H_DIFF, tq, DV), F32)] * 3,
        compiler_params=_cparams(("parallel", "parallel", "arbitrary")),
        name="diff_attn_prompt",
    )(*lam_params, subg.reshape(1, DV), p, p, p, p)


def _diff_decode_kernel(pt_ref, lq1, lk1, lq2, lk2, subg_ref, q_ref, kc_ref, vc_ref, g_ref, *rest,
                        n_pages, lam_init):
    kp = rest[:n_pages]
    vp = rest[n_pages:2 * n_pages]
    o_ref = rest[2 * n_pages]
    nrow = 2 * H_DIFF
    row = lax.broadcasted_iota(jnp.int32, (nrow, BRANCH), 0)
    lane = lax.broadcasted_iota(jnp.int32, (nrow, BRANCH), 1)
    q = q_ref[...] * (DQK ** -0.5)
    qm = jnp.where(lane // DQK == row, q, 0.0).astype(BF16)

    rowc = lax.broadcasted_iota(jnp.int32, (nrow, 1), 0)
    slope = jnp.zeros((nrow, 1), F32)
    for h in range(H_DIFF):
        slope = jnp.where(rowc // 2 == h, _alibi_slope(h), slope)

    past = n_pages * PAGE_SIZE
    s = jnp.concatenate([jnp.dot(qm, kp[j][...].astype(BF16), preferred_element_type=F32)
                         for j in range(n_pages)], axis=-1)
    dist = (past - lax.broadcasted_iota(jnp.int32, (nrow, past), 1)).astype(F32)
    s = s + (-slope) * dist
    kc = kc_ref[...].astype(BF16).astype(F32)
    vc = vc_ref[...].astype(BF16).astype(F32)
    sc = jnp.sum(qm.astype(F32) * kc, axis=-1, keepdims=True)
    m = jnp.maximum(jnp.max(s, axis=-1, keepdims=True), sc)
    p = jnp.exp(s - m)
    pc = jnp.exp(sc - m)
    l_fin = jnp.sum(p, axis=-1, keepdims=True) + pc
    pb = p.astype(BF16)
    pcb = pc.astype(BF16).astype(F32)
    lam = _lambda(lq1, lk1, lq2, lk2, lam_init)
    sign = jnp.where(rowc % 2 == 0, 1.0, -lam) / l_fin
    g = g_ref[...]
    outs = []
    for h in range(H_DIFF):
        hs = slice(h * DV, (h + 1) * DV)
        acc = pcb * vc[:, hs]
        for j in range(n_pages):
            vh = vp[j][pl.ds(h, PAGE_SIZE, stride=H_DIFF), :].astype(BF16)
            acc = acc + jnp.dot(pb[:, j * PAGE_SIZE:(j + 1) * PAGE_SIZE], vh, preferred_element_type=F32)
        o = jnp.sum(jnp.where(rowc // 2 == h, acc * sign, 0.0), axis=0, keepdims=True)
        outs.append(_subln_gate(o, subg_ref[...], g[:, hs], lam_init))
    o_ref[...] = jnp.concatenate(outs, axis=-1).astype(o_ref.dtype)


def _diff_decode(p3, layer, cache_k, cache_v, page_table, lam_params, subg, lam_init):
    nb, n_pages = page_table.shape
    vec = pl.BlockSpec((1, DQK), lambda b, pt: (0, 0))

    def row(c):
        return pl.BlockSpec((None, 1, BRANCH), lambda b, pt: (b, 0, c))

    def page(j, shape):
        return pl.BlockSpec((None, None) + shape, lambda b, pt: (layer, pt[b * n_pages + j], 0, 0))

    kpages = [page(j, (BRANCH, PAGE_SIZE)) for j in range(n_pages)]
    vpages = [page(j, (PAGE_SIZE * H_DIFF, DV)) for j in range(n_pages)]
    return pl.pallas_call(
        functools.partial(_diff_decode_kernel, n_pages=n_pages, lam_init=lam_init),
        grid_spec=pltpu.PrefetchScalarGridSpec(
            num_scalar_prefetch=1,
            grid=(nb,),
            in_specs=[vec, vec, vec, vec, pl.BlockSpec((1, DV), lambda b, pt: (0, 0)),
                      row(C_DQ), row(C_DK), row(C_DV), row(C_DG)] + kpages + vpages,
            out_specs=pl.BlockSpec((None, 1, BRANCH), lambda b, pt: (b, 0, 0))),
        out_shape=jax.ShapeDtypeStruct((nb, 1, BRANCH), BF16),
        compiler_params=_cparams(("parallel",)),
        name="diff_attn_decode",
    )(page_table.reshape(-1), *lam_params, subg.reshape(1, DV), p3, p3, p3, p3,
      *([cache_k] * n_pages), *([cache_v] * n_pages))


def _lane_sum3(x, ones):
    x1 = x.astype(BF16)
    rem = x - x1.astype(F32)
    x2 = rem.astype(BF16)
    x3 = (rem - x2.astype(F32)).astype(BF16)
    d = lambda z: jnp.dot(z, ones, preferred_element_type=F32)
    return d(x1) + (d(x2) + d(x3))


def _seg_sum(x, bd_ref):
    return _lane_sum3(x, bd_ref[...])


def _rwkv_pre_kernel(*refs, shift, has_vres, tm):
    it = iter(refs)
    cur = [next(it) for _ in range(4)]
    prev = None if shift else [next(it) for _ in range(4)]
    mu = [next(it) for _ in range(4)]
    w0, wup, a0, aup, kk_ref, ka_ref, rk_ref, bd_ref = [next(it) for _ in range(8)]
    if has_vres:
        vf_ref, vr0, vra, vrb = [next(it) for _ in range(4)]
    outs = [next(it) for _ in range(7)]
    carry = [next(it) for _ in range(4)] if shift else None

    xs = []
    for n in range(4):
        c = cur[n][...]
        if shift:
            @pl.when(pl.program_id(1) == 0)
            def _():
                carry[n][...] = jnp.zeros(carry[n].shape, F32)
            rolled = pltpu.roll(c, 1, 0)
            rows = lax.broadcasted_iota(jnp.int32, c.shape, 0)
            pv = jnp.where(rows == 0, carry[n][...], rolled)
            carry[n][...] = c[tm - 1:tm, :]
        else:
            pv = prev[n][...]
        xs.append(c + (pv - c) * mu[n][...])
    xr, xk, xv, xl = xs

    hw = jnp.dot(jnp.tanh(xl).astype(BF16), wup[...], preferred_element_type=F32)
    w = -DECAY_SCALE * jax.nn.sigmoid(w0[...] + hw)
    if not shift:
        w = jnp.exp(w)
    a = jax.nn.sigmoid(a0[...] + jnp.dot(xl.astype(BF16), aup[...], preferred_element_type=F32))
    if has_vres:
        t1 = jnp.dot(xv.astype(BF16), vra[...], preferred_element_type=F32)
        t2 = jnp.dot(t1.astype(BF16), vrb[...], preferred_element_type=F32)
        xv = xv + (vf_ref[...] - xv) * jax.nn.sigmoid(vr0[...] + t2)
    kk = xk * kk_ref[...]
    kk = kk / jnp.maximum(jnp.sqrt(_seg_sum(kk * kk, bd_ref)), 1e-12)
    k2 = xk * (1.0 + (a - 1.0) * ka_ref[...])
    bonus = _seg_sum(xr * k2 * rk_ref[...], bd_ref) * xv
    for o, val in zip(outs, (xr, w, k2, xv, -kk, kk * a, bonus)):
        o[...] = val


def _rwkv_pre(p, prev, lw, vfirst, nb, t, tm):
    shift = prev is None
    has_vres = vfirst is not None
    nt = t // tm
    if shift:
        grid = (nb, nt)
        rmap = lambda c: (lambda b, i: (b * nt + i, c))
        cmap = lambda b, i: (0, 0)
        sem = ("parallel", "arbitrary")
    else:
        grid = (nt,)
        rmap = lambda c: (lambda i: (i, c))
        cmap = lambda i: (0, 0)
        sem = ("parallel",)
    wide = lambda c: pl.BlockSpec((tm, BRANCH), rmap(c))
    vec = lambda n: pl.BlockSpec((1, n), cmap)
    args, specs = [], []
    args += [p, p, p, p]
    specs += [wide(C_RR), wide(C_RK), wide(C_RV), pl.BlockSpec((tm, N_LORA), rmap(C_LORA128))]
    if not shift:
        args += [prev, prev, prev, prev]
        specs += [wide(0), wide(1), wide(2), pl.BlockSpec((tm, N_LORA), rmap(3 * BRANCH // N_LORA))]
    args += lw["mu"]
    specs += [vec(BRANCH)] * 3 + [vec(N_LORA)]
    args += [lw["w0"], lw["w_up"], lw["a0"], lw["a_up"], lw["k_k"], lw["k_a"], lw["r_k"], lw["bd"]]
    specs += [vec(BRANCH), pl.BlockSpec((N_LORA, BRANCH), cmap), vec(BRANCH), pl.BlockSpec((N_LORA, BRANCH), cmap),
              vec(BRANCH), vec(BRANCH), vec(BRANCH), pl.BlockSpec((BRANCH, BRANCH), cmap)]
    if has_vres:
        args += [vfirst, lw["vres_0"], lw["vres_a"], lw["vres_b"]]
        specs += [pl.BlockSpec((tm, BRANCH), rmap(0)), vec(BRANCH),
                  pl.BlockSpec(lw["vres_a"].shape, cmap), pl.BlockSpec(lw["vres_b"].shape, cmap)]
    out_spec = pl.BlockSpec((tm, BRANCH), rmap(0))
    scratch = [pltpu.VMEM((1, BRANCH), F32)] * 3 + [pltpu.VMEM((1, N_LORA), F32)] if shift else []
    return pl.pallas_call(
        functools.partial(_rwkv_pre_kernel, shift=shift, has_vres=has_vres, tm=tm),
        grid=grid, in_specs=specs, out_specs=[out_spec] * 7,
        out_shape=[jax.ShapeDtypeStruct((nb * t, BRANCH), F32)] * 7,
        scratch_shapes=scratch,
        compiler_params=_cparams(sem),
        name="rwkv_pre",
    )(*args)


RWKV_CHUNK = 64

_NN = ((1,), (0,))
_NT = ((1,), (1,))
_TN = ((0,), (0,))


def _split(x):
    hi = x.astype(BF16)
    return hi, (x - hi.astype(F32)).astype(BF16)


def _mm3(a, b, dims):
    dg = lambda x, y: lax.dot_general(x, y, (dims, ((), ())), preferred_element_type=F32)
    return dg(a[0], b[0]) + (dg(a[0], b[1]) + dg(a[1], b[0]))


def _rwkv_scan_kernel(r_ref, lw_ref, k_ref, v_ref, a_ref, b_ref, y_ref, s_ref):
    C = RWKV_CHUNK
    W2 = 2 * R_HEAD

    @pl.when(pl.program_id(1) == 0)
    def _():
        s_ref[...] = jnp.zeros(s_ref.shape, F32)

    lw = lw_ref[...]
    tri = (lax.broadcasted_iota(jnp.int32, (C, C), 0) >= lax.broadcasted_iota(jnp.int32, (C, C), 1)).astype(BF16)
    l1 = lw.astype(BF16)
    rem = lw - l1.astype(F32)
    l2 = rem.astype(BF16)
    l3 = (rem - l2.astype(F32)).astype(BF16)
    cs = (jnp.dot(tri, l1, preferred_element_type=F32) + jnp.dot(tri, l2, preferred_element_type=F32)
          + jnp.dot(tri, l3, preferred_element_type=F32))
    p_in = jnp.exp(cs)
    p_inv = jnp.exp(-cs)
    rt = r_ref[...] * p_in
    at = a_ref[...] * jnp.exp(cs - lw)
    bt = b_ref[...] * p_inv
    kt = k_ref[...] * p_inv
    v = v_ref[...]
    p_last = p_in[C - 1:C, :]

    heads = range(H_RWKV)
    pairs = range(H_RWKV // 2)
    psl = [slice(p * W2, (p + 1) * W2) for p in pairs]
    lane2 = lax.broadcasted_iota(jnp.int32, (2 * C, W2), 1)
    keep = [lane2 < R_HEAD, lane2 >= R_HEAD]
    zb = jnp.zeros((2 * C, W2), BF16)
    gr = lax.broadcasted_iota(jnp.int32, (2 * C, 2 * C), 0)
    gc = lax.broadcasted_iota(jnp.int32, (2 * C, 2 * C), 1) % C
    gmask = jnp.where(gr < C, gr - 1, gr - C) >= gc
    second = lax.broadcasted_iota(jnp.int32, (C, W2), 1) >= R_HEAD

    ar_s = [_split(jnp.concatenate([at[:, s], rt[:, s]], axis=0)) for s in psl]
    bk_s = [_split(jnp.concatenate([bt[:, s], kt[:, s]], axis=0)) for s in psl]
    s0 = [s_ref[p] for p in pairs]
    s0_s = [_split(z) for z in s0]
    mask2 = lambda z, h: tuple(jnp.where(keep[h % 2], x, zb) for x in z[h // 2])
    arm = [mask2(ar_s, h) for h in heads]
    vh = [v[:, h * R_HEAD:(h + 1) * R_HEAD] for h in heads]
    zv_s = [_split(jnp.concatenate([jnp.zeros((C, R_HEAD), F32), z], axis=0)) for z in vh]
    g = [jnp.where(gmask, _mm3(arm[h], bk_s[h // 2], _NT), 0.0) for h in heads]
    x = [_mm3(arm[h], s0_s[h // 2], _NT) for h in heads]
    rhs = [x[h][:C] + _mm3(_split(g[h][:C]), zv_s[h], _NN) for h in heads]
    w = [jnp.concatenate([g[h][:C, :C], rhs[h]], axis=1) for h in heads]
    for _ in range(int(math.log2(C))):
        w_s = [_split(z) for z in w]
        w = [_mm3((w_s[h][0][:, :R_HEAD], w_s[h][1][:, :R_HEAD]), w_s[h], _NN) + jnp.where(second, w[h], 0.0)
             for h in heads]
    uv_s = [_split(jnp.concatenate([w[h][:, R_HEAD:], vh[h]], axis=0)) for h in heads]
    for h in heads:
        y_ref[:, h * R_HEAD:(h + 1) * R_HEAD] = x[h][C:] + _mm3(_split(g[h][C:]), uv_s[h], _NN)
    ds = [_mm3(uv_s[h], mask2(bk_s, h), _TN) for h in heads]
    for p in pairs:
        s_ref[p] = (s0[p] + ds[2 * p] + ds[2 * p + 1]) * p_last[:, psl[p]]


def _rwkv_scan(seqs, nb, t):
    tc = RWKV_CHUNK
    hp = H_RWKV // 2
    blk = pl.BlockSpec((None, tc, BRANCH), lambda b, c: (b, c, 0))
    y, s = pl.pallas_call(
        _rwkv_scan_kernel,
        grid=(nb, t // tc),
        in_specs=[blk] * 6,
        out_specs=[blk, pl.BlockSpec((hp, R_HEAD, 2 * R_HEAD), lambda b, c: (b, 0, 0))],
        out_shape=[jax.ShapeDtypeStruct((nb, t, BRANCH), F32),
                   jax.ShapeDtypeStruct((nb * hp, R_HEAD, 2 * R_HEAD), F32)],
        compiler_params=_cparams(("parallel", "arbitrary")),
        name="rwkv_scan",
    )(*seqs)
    s = s.reshape(nb, hp, R_HEAD, 2, R_HEAD).transpose(0, 1, 3, 2, 4).reshape(nb, H_RWKV, R_HEAD, R_HEAD)
    return y, s


def _rwkv_step_kernel(r_ref, w_ref, k_ref, a_ref, b_ref, vt_ref, s_ref, yt_ref, so_ref, *, bb):
    ones = jnp.ones((R_HEAD, R_HEAD), BF16)

    def expand(ref):
        return jnp.concatenate([jnp.broadcast_to(ref[j:j + 1, h * R_HEAD:(h + 1) * R_HEAD], (R_HEAD, R_HEAD))
                                for j in range(bb) for h in range(H_RWKV)], axis=0)

    st = s_ref[...].reshape(bb * BRANCH, R_HEAD)
    sa = _lane_sum3(st * expand(a_ref), ones)
    vcol = jnp.concatenate([jnp.broadcast_to(vt_ref[:, j:j + 1], (BRANCH, R_HEAD)) for j in range(bb)], axis=0)
    st = st * expand(w_ref) + sa * expand(b_ref) + vcol * expand(k_ref)
    so_ref[...] = st.reshape(bb, BRANCH, R_HEAD)
    y = _lane_sum3(st * expand(r_ref), ones)
    for j in range(bb):
        yt_ref[:, j:j + 1] = y[j * BRANCH:(j + 1) * BRANCH, 0:1]


def _rwkv_step(r, w, k, na, b, vt, state, layer, bb=8):
    n = r.shape[0]
    rowb = pl.BlockSpec((bb, BRANCH), lambda i: (i, 0))
    colb = pl.BlockSpec((None, BRANCH, bb), lambda i: (i, 0, 0))
    stb = pl.BlockSpec((bb, BRANCH, R_HEAD), lambda i: (i, 0, 0))
    return pl.pallas_call(
        functools.partial(_rwkv_step_kernel, bb=bb),
        grid=(n // bb,),
        in_specs=[rowb] * 5 + [colb, pl.BlockSpec((None, bb, BRANCH, R_HEAD), lambda i: (layer, i, 0, 0))],
        out_specs=[colb, stb],
        out_shape=[jax.ShapeDtypeStruct((n // bb, BRANCH, bb), F32),
                   jax.ShapeDtypeStruct((n, BRANCH, R_HEAD), F32)],
        compiler_params=_cparams(("parallel",)),
        name="rwkv_step",
    )(r, w, k, na, b, vt, state)


def _rwkv_post_kernel(y_ref, bonus_ref, g_ref, lg_ref, lb_ref, bd_ref, o_ref):
    y = y_ref[...]
    mu = _seg_sum(y, bd_ref) * (1.0 / R_HEAD)
    d = y - mu
    var = _seg_sum(d * d, bd_ref) * (1.0 / R_HEAD)
    o = d * lax.rsqrt(var + LNX_EPS) * lg_ref[...] + lb_ref[...] + bonus_ref[...]
    o_ref[...] = (o * _silu(g_ref[...])).astype(o_ref.dtype)


def _rwkv_post(y, bonus, p, lw, tm):
    m = y.shape[0]
    blk = pl.BlockSpec((tm, BRANCH), lambda i: (i, 0))
    vec = pl.BlockSpec((1, BRANCH), lambda i: (0, 0))
    return pl.pallas_call(
        _rwkv_post_kernel,
        grid=(m // tm,),
        in_specs=[blk, blk, pl.BlockSpec((tm, BRANCH), lambda i: (i, C_RG)), vec, vec,
                  pl.BlockSpec((BRANCH, BRANCH), lambda i: (0, 0))],
        out_specs=blk,
        out_shape=jax.ShapeDtypeStruct((m, BRANCH), BF16),
        compiler_params=_cparams(("parallel",)),
        name="rwkv_post",
    )(y, bonus, p, lw["lnx_g"], lw["lnx_b"], lw["bd"])


def _softplus(x):
    return jnp.maximum(x, 0.0) + jnp.log1p(jnp.exp(-jnp.abs(x)))


def _neg_exp_alog(alog_ref):
    lanes = lax.broadcasted_iota(jnp.int32, alog_ref.shape, 1)
    return jnp.where(lanes < H_MAMBA, -jnp.exp(alog_ref[...]), 0.0)


def _gated_norm(ym, z, g):
    u = ym * _silu(z)
    ms = jnp.mean(u * u, axis=-1, keepdims=True)
    return u * lax.rsqrt(ms + RMS_EPS) * g


def _mamba_prompt_kernel(mx_ref, mbc_ref, mdt_ref, mz_ref, cwx_ref, cwb_ref, cbx_ref, cbb_ref, dtb_ref, alog_ref,
                         dskip_ref, gn_ref, o_ref, s_ref, extx_ref, extb_ref, ym_ref):
    L = SSD_CHUNK
    pad = 8

    @pl.when(pl.program_id(1) == 0)
    def _():
        extx_ref[0:pad, :] = jnp.zeros((pad, BRANCH), F32)
        extb_ref[0:pad, :] = jnp.zeros((pad, BRANCH), F32)
        s_ref[...] = jnp.zeros(s_ref.shape, F32)

    def conv(ext_ref, cur_ref, cw_ref, cb_ref):
        ext_ref[pad:pad + L, :] = cur_ref[...]
        acc = cb_ref[...]
        for j in range(M_CONV):
            lo = pad - (M_CONV - 1) + j
            acc = acc + ext_ref[lo:lo + L, :] * cw_ref[j:j + 1, :]
        ext_ref[0:pad, :] = ext_ref[L:L + pad, :]
        return _silu(acc)

    x = conv(extx_ref, mx_ref, cwx_ref, cbx_ref)
    bc = conv(extb_ref, mbc_ref, cwb_ref, cbb_ref)
    dt = _softplus(mdt_ref[...] + dtb_ref[...])
    adt = dt * _neg_exp_alog(alog_ref)
    rowi = lax.broadcasted_iota(jnp.int32, (L, L), 0)
    coli = lax.broadcasted_iota(jnp.int32, (L, L), 1)
    lower = rowi >= coli
    acs = jnp.dot(lower.astype(F32), adt, preferred_element_type=F32, precision=HIGHEST)
    acs_t = acs.T
    last = acs[L - 1:L, :]
    dstate = jnp.exp(last - acs)
    eacs = jnp.exp(acs)
    elast = jnp.exp(last)
    bcb = bc.astype(BF16)
    grp = []
    for g in range(M_GROUPS):
        bg = bcb[:, g * M_STATE:(g + 1) * M_STATE]
        cg = bcb[:, (M_GROUPS + g) * M_STATE:(M_GROUPS + g + 1) * M_STATE]
        grp.append((bg, cg, _dot_t(cg, bg)))
    for h in range(H_MAMBA):
        bg, cg, cb = grp[h // (H_MAMBA // M_GROUPS)]
        sl = slice(h * M_HEADDIM, (h + 1) * M_HEADDIM)
        decay = jnp.where(lower, jnp.exp(acs[:, h:h + 1] - acs_t[h:h + 1, :]), 0.0)
        xh = x[:, sl]
        xdt = xh * dt[:, h:h + 1]
        yd = jnp.dot((cb * decay).astype(BF16), xdt.astype(BF16), preferred_element_type=F32)
        st = s_ref[h]
        yo = _dot_t(cg, st.astype(BF16)) * eacs[:, h:h + 1]
        upd = lax.dot_general((xdt * dstate[:, h:h + 1]).astype(BF16), bg, (((0,), (0,)), ((), ())),
                              preferred_element_type=F32)
        s_ref[h] = st * elast[:, h:h + 1] + upd
        ym_ref[:, sl] = yd + yo + dskip_ref[:, sl] * xh
    o_ref[...] = _gated_norm(ym_ref[...], mz_ref[...], gn_ref[...]).astype(o_ref.dtype)


def _mamba_prompt(p, lw, nb, t):
    L = SSD_CHUNK
    nt = t // L
    rmap = lambda c: (lambda b, i: (b * nt + i, c))
    cmap = lambda b, i: (0, 0)
    wide = lambda c: pl.BlockSpec((L, BRANCH), rmap(c))
    vec = lambda n: pl.BlockSpec((1, n), cmap)
    cw = pl.BlockSpec((M_CONV, BRANCH), cmap)
    return pl.pallas_call(
        _mamba_prompt_kernel,
        grid=(nb, nt),
        in_specs=[wide(C_MX), wide(C_MBC), pl.BlockSpec((L, 128), rmap(C_DT128)), wide(C_MZ),
                  cw, cw, vec(BRANCH), vec(BRANCH), vec(128), vec(128), vec(BRANCH), vec(BRANCH)],
        out_specs=[pl.BlockSpec((L, BRANCH), rmap(0)),
                   pl.BlockSpec((None, H_MAMBA, M_HEADDIM, M_STATE), lambda b, i: (b, 0, 0, 0))],
        out_shape=[jax.ShapeDtypeStruct((nb * t, BRANCH), BF16),
                   jax.ShapeDtypeStruct((nb, H_MAMBA, M_HEADDIM, M_STATE), F32)],
        scratch_shapes=[pltpu.VMEM((L + 8, BRANCH), F32), pltpu.VMEM((L + 8, BRANCH), F32),
                        pltpu.VMEM((L, BRANCH), F32)],
        compiler_params=_cparams(("parallel", "arbitrary")),
        name="mamba_prompt",
    )(p, p, p, p, lw["cw_x"], lw["cw_bc"], lw["cb_x"], lw["cb_bc"], lw["dt_bias"], lw["a_log"],
      lw["d_skip"], lw["gnorm_g"])


def _mamba_dec_pre_kernel(mx_ref, mbc_ref, mdt_ref, sx_ref, sb_ref, cwx_ref, cwb_ref, cbx_ref, cbb_ref,
                          dtb_ref, alog_ref, x_ref, bc_ref, xdt_ref, da_ref):
    def conv(st_ref, cur_ref, cw_ref, cb_ref):
        acc = cb_ref[...]
        for j in range(M_CONV - 1):
            acc = acc + st_ref[j] * cw_ref[j:j + 1, :]
        acc = acc + cur_ref[...] * cw_ref[M_CONV - 1:M_CONV, :]
        return _silu(acc)

    x = conv(sx_ref, mx_ref, cwx_ref, cbx_ref)
    x_ref[...] = x
    bc_ref[...] = conv(sb_ref, mbc_ref, cwb_ref, cbb_ref)
    dt = _softplus(mdt_ref[...] + dtb_ref[...])
    da = jnp.exp(dt * _neg_exp_alog(alog_ref))
    n = x.shape[0]
    for h in range(H_MAMBA):
        sl = slice(h * M_HEADDIM, (h + 1) * M_HEADDIM)
        xdt_ref[:, sl] = x[:, sl] * dt[:, h:h + 1]
        da_ref[:, sl] = jnp.broadcast_to(da[:, h:h + 1], (n, M_HEADDIM))


def _mamba_dec_pre(p, conv_x, conv_bc, lw):
    n = p.shape[0]
    cmap = lambda i: (0, 0)
    wide = lambda c: pl.BlockSpec((n, BRANCH), lambda i: (0, c))
    vec = lambda m: pl.BlockSpec((1, m), cmap)
    cw = pl.BlockSpec((M_CONV, BRANCH), cmap)
    st = pl.BlockSpec((M_CONV - 1, n, BRANCH), lambda i: (0, 0, 0))
    out = pl.BlockSpec((n, BRANCH), cmap)
    return pl.pallas_call(
        _mamba_dec_pre_kernel,
        grid=(1,),
        in_specs=[wide(C_MX), wide(C_MBC), pl.BlockSpec((n, 128), lambda i: (0, C_DT128)), st, st,
                  cw, cw, vec(BRANCH), vec(BRANCH), vec(128), vec(128)],
        out_specs=[out] * 4,
        out_shape=[jax.ShapeDtypeStruct((n, BRANCH), F32)] * 4,
        compiler_params=_cparams(("arbitrary",)),
        name="mamba_dec_pre",
    )(p, p, p, conv_x, conv_bc, lw["cw_x"], lw["cw_bc"], lw["cb_x"], lw["cb_bc"], lw["dt_bias"], lw["a_log"])


def _mamba_step_kernel(bc_ref, xdtt_ref, dat_ref, s_ref, yt_ref, so_ref, *, bb):
    units = [(j, h // (H_MAMBA // M_GROUPS), slice(h * M_HEADDIM, (h + 1) * M_HEADDIM))
             for j in range(bb) for h in range(H_MAMBA)]
    st = []
    for j, g, sl in units:
        bg = bc_ref[j:j + 1, g * M_STATE:(g + 1) * M_STATE]
        z = s_ref[j, sl, :] * dat_ref[sl, j:j + 1] + xdtt_ref[sl, j:j + 1] * bg
        so_ref[j, sl, :] = z
        st.append(z)
    ys = [jnp.sum(z * bc_ref[j:j + 1, (M_GROUPS + g) * M_STATE:(M_GROUPS + g + 1) * M_STATE], axis=-1, keepdims=True)
          for z, (j, g, sl) in zip(st, units)]
    for y, (j, g, sl) in zip(ys, units):
        yt_ref[sl, j:j + 1] = y


def _mamba_step(bc, xdtt, dat, state, layer, bb=8):
    n = bc.shape[0]
    rowb = pl.BlockSpec((bb, BRANCH), lambda i: (i, 0))
    colb = pl.BlockSpec((None, BRANCH, bb), lambda i: (i, 0, 0))
    stb = pl.BlockSpec((bb, BRANCH, M_STATE), lambda i: (i, 0, 0))
    return pl.pallas_call(
        functools.partial(_mamba_step_kernel, bb=bb),
        grid=(n // bb,),
        in_specs=[rowb, colb, colb, pl.BlockSpec((None, bb, BRANCH, M_STATE), lambda i: (layer, i, 0, 0))],
        out_specs=[colb, stb],
        out_shape=[jax.ShapeDtypeStruct((n // bb, BRANCH, bb), F32),
                   jax.ShapeDtypeStruct((n, BRANCH, M_STATE), F32)],
        compiler_params=_cparams(("parallel",)),
        name="mamba_step",
    )(bc, xdtt, dat, state)


def _mamba_dec_post_kernel(y_ref, x_ref, z_ref, dskip_ref, gn_ref, o_ref):
    ym = y_ref[...] + dskip_ref[...] * x_ref[...]
    o_ref[...] = _gated_norm(ym, z_ref[...], gn_ref[...]).astype(o_ref.dtype)


def _mamba_dec_post(y, x, p, lw):
    n = y.shape[0]
    blk = pl.BlockSpec((n, BRANCH), lambda i: (0, 0))
    vec = pl.BlockSpec((1, BRANCH), lambda i: (0, 0))
    return pl.pallas_call(
        _mamba_dec_post_kernel,
        grid=(1,),
        in_specs=[blk, blk, pl.BlockSpec((n, BRANCH), lambda i: (0, C_MZ)), vec, vec],
        out_specs=blk,
        out_shape=jax.ShapeDtypeStruct((n, BRANCH), BF16),
        compiler_params=_cparams(("arbitrary",)),
        name="mamba_dec_post",
    )(y, x, p, lw["d_skip"], lw["gnorm_g"])


def _xattn_prompt_kernel(q_ref, g_ref, mk_ref, mv_ref, o_ref):
    q = q_ref[...].astype(BF16)
    mk = mk_ref[...].astype(BF16)
    mv = mv_ref[...].astype(BF16)
    for h in range(H_X):
        sl = slice(h * D_X, (h + 1) * D_X)
        s = _dot_t(q[:, sl], mk[:, sl]) * (D_X ** -0.5)
        e = jnp.exp(s - jnp.max(s, axis=-1, keepdims=True))
        o = jnp.dot(e.astype(BF16), mv[:, sl], preferred_element_type=F32) / jnp.sum(e, axis=-1, keepdims=True)
        o_ref[:, sl] = (o * _silu(g_ref[:, sl])).astype(o_ref.dtype)


def _xattn_prompt(p, mkv, nb, t):
    tq = min(512, t)
    nq = t // tq
    return pl.pallas_call(
        _xattn_prompt_kernel,
        grid=(nb, nq),
        in_specs=[pl.BlockSpec((tq, BRANCH), lambda b, i: (b * nq + i, C_XQ)),
                  pl.BlockSpec((tq, BRANCH), lambda b, i: (b * nq + i, C_XG)),
                  pl.BlockSpec((N_MEM, BRANCH), lambda b, i: (b, 0)),
                  pl.BlockSpec((N_MEM, BRANCH), lambda b, i: (b, 1))],
        out_specs=pl.BlockSpec((tq, BRANCH), lambda b, i: (b * nq + i, 0)),
        out_shape=jax.ShapeDtypeStruct((nb * t, BRANCH), BF16),
        compiler_params=_cparams(("parallel", "parallel")),
        name="xattn_prompt",
    )(p, p, mkv, mkv)


def _xattn_decode_kernel(q_ref, g_ref, mk_ref, mv_ref, o_ref, *, bb):
    nrow = 8
    units = [(j, h) for j in range(bb) for h in range(H_X)]
    q = [q_ref[j].astype(BF16) for j in range(bb)]
    s = [_dot_t(jnp.broadcast_to(q[j][:, h * D_X:(h + 1) * D_X], (nrow, D_X)),
                mk_ref[j, pl.ds(h, N_MEM, stride=H_X), :].astype(BF16)) * (D_X ** -0.5) for j, h in units]
    e = [jnp.exp(z - jnp.max(z, axis=-1, keepdims=True)) for z in s]
    r = [jnp.dot(z.astype(BF16), mv_ref[j, pl.ds(h, N_MEM, stride=H_X), :].astype(BF16),
                 preferred_element_type=F32) for z, (j, h) in zip(e, units)]
    outs = [(x / jnp.sum(z, axis=-1, keepdims=True))[0:1, :] for x, z in zip(r, e)]
    for j in range(bb):
        o = jnp.concatenate(outs[j * H_X:(j + 1) * H_X], axis=-1)
        o_ref[j] = (o * _silu(g_ref[j])).astype(o_ref.dtype)


def _xattn_decode(p3, layer, mem_k, mem_v, bb=4):
    n = p3.shape[0]
    mem = pl.BlockSpec((None, bb, N_MEM * H_X, D_X), lambda b: (layer, b, 0, 0))
    return pl.pallas_call(
        functools.partial(_xattn_decode_kernel, bb=bb),
        grid=(n // bb,),
        in_specs=[pl.BlockSpec((bb, 1, BRANCH), lambda b: (b, 0, C_XQ)),
                  pl.BlockSpec((bb, 1, BRANCH), lambda b: (b, 0, C_XG)), mem, mem],
        out_specs=pl.BlockSpec((bb, 1, BRANCH), lambda b: (b, 0, 0)),
        out_shape=jax.ShapeDtypeStruct((n, 1, BRANCH), BF16),
        compiler_params=_cparams(("parallel",)),
        name="xattn_decode",
    )(p3, p3, mem_k, mem_v)


def _layer_weights(l, prm):
    w = prm["w_in"][l].astype(BF16)
    o_rs, o_rg, o_mz, o_mxbc, o_dt, o_xq, o_xg = 2048, 3712, 4224, 4736, 5760, 5768, 6280
    w_perm = jnp.concatenate([
        w[:, :o_rs + 3 * BRANCH], w[:, o_rg:o_mz], w[:, o_mz:o_mxbc], w[:, o_xq:o_xg], w[:, o_xg:],
        w[:, o_mxbc:o_dt], w[:, o_rs + 3 * BRANCH:o_rg], w[:, o_dt:o_xq],
        jnp.zeros((D_MODEL, 128 - H_MAMBA), w.dtype)], axis=1).astype(BF16)
    row = lambda v: v.reshape(1, -1).astype(F32)
    mu = prm["shift_mu"][l].astype(F32)
    seg = jnp.arange(BRANCH) // R_HEAD
    zl = jnp.zeros((N_LORA // 2, BRANCH), F32)
    cw = prm["conv_w"][l].astype(F32)
    cb = prm["conv_b"][l].astype(F32)
    pad8 = lambda v: jnp.pad(v.astype(F32), (0, 128 - H_MAMBA)).reshape(1, 128)
    lw = {
        "w_in": w_perm, "norm_g": prm["norm_g"][l], "w_out": prm["w_out"][l].astype(BF16),
        "lam": [row(prm[n][l]) for n in ("lam_q1", "lam_k1", "lam_q2", "lam_k2")],
        "subln_g": prm["subln_g"][l].astype(F32),
        "mu": [row(mu[:BRANCH]), row(mu[BRANCH:2 * BRANCH]), row(mu[2 * BRANCH:3 * BRANCH]), row(mu[3 * BRANCH:])],
        "w0": row(prm["w0"][l]), "a0": row(prm["a0"][l]),
        "w_up": jnp.concatenate([prm["w_up"][l].astype(F32), zl], axis=0).astype(BF16),
        "a_up": jnp.concatenate([zl, prm["a_up"][l].astype(F32)], axis=0).astype(BF16),
        "k_k": row(prm["k_k"][l]), "k_a": row(prm["k_a"][l]), "r_k": row(prm["r_k"][l]),
        "lnx_g": row(prm["lnx_g"][l]), "lnx_b": row(prm["lnx_b"][l]),
        "bd": (seg[:, None] == seg[None, :]).astype(BF16),
        "cw_x": cw[:, :BRANCH], "cw_bc": cw[:, BRANCH:], "cb_x": row(cb[:BRANCH]), "cb_bc": row(cb[BRANCH:]),
        "dt_bias": pad8(prm["dt_bias"][l]), "a_log": pad8(prm["a_log"][l]),
        "d_skip": row(jnp.repeat(prm["d_skip"][l].astype(F32), M_HEADDIM)),
        "gnorm_g": row(prm["gnorm_g"][l]),
        "w_mkv": jnp.concatenate([prm["w_mk"][l], prm["w_mv"][l]], axis=1).astype(BF16),
        "mem_norm_g": prm["mem_norm_g"][l],
    }
    if l > 0:
        lw["vres_0"] = row(prm["vres_0"][l - 1])
        lw["vres_a"] = prm["vres_a"][l - 1].astype(BF16)
        lw["vres_b"] = prm["vres_b"][l - 1].astype(BF16)
    return lw


def _lam_init(l):
    return 0.8 - 0.6 * math.exp(-0.3 * l)


def _prompt_layer(x, l, lw, mem, vfirst, nb, t, fg, final, tm_proj, tm_row):
    p = _rms_matmul(x, lw["norm_g"], lw["w_in"], tm_proj, 1152)
    mkv = _rms_matmul(mem, lw["mem_norm_g"], lw["w_mkv"], mem.shape[0], 2 * BRANCH)
    out_a = _diff_prompt(p, lw["lam"], lw["subln_g"], nb, t, _lam_init(l))
    r, w, k, v, na, b, bonus = _rwkv_pre(p, None, lw, vfirst, nb, t, tm_row)
    seqs = [z.reshape(nb, t, BRANCH) for z in (r, w, k, v, na, b)]
    y, rw_state = _rwkv_scan(seqs, nb, t)
    out_b = _rwkv_post(y.reshape(nb * t, BRANCH), bonus, p, lw, tm_row)
    out_m, ssm_state = _mamba_prompt(p, lw, nb, t)
    out_x = _xattn_prompt(p, mkv, nb, t)
    x = _outproj((out_a, out_b, out_m, out_x), lw["w_out"], x, fg, tm_row, final)
    return x, p, mkv, rw_state, ssm_state, (v if vfirst is None else vfirst)


def _decode_layer(x, l, lw, caches, vfirst, fg, final):
    cache_k, cache_v, mem_k, mem_v, page_table, rw_prev, rw_state, conv_state, ssm_state = caches
    n = x.shape[0]
    bb = 8
    p = _rms_matmul(x, lw["norm_g"], lw["w_in"], n, 1152)
    p3 = p.reshape(n, 1, N_PROJ)
    out_a = _diff_decode(p3, l, cache_k, cache_v, page_table, lw["lam"], lw["subln_g"], _lam_init(l)).reshape(n, BRANCH)

    r, w, k, v, na, b, bonus = _rwkv_pre(p, rw_prev, lw, vfirst, 1, n, n)
    to_cols = lambda z: z.reshape(n // bb, bb, BRANCH).transpose(0, 2, 1)
    from_cols = lambda z: z.transpose(0, 2, 1).reshape(n, BRANCH)
    yt, rw_new = _rwkv_step(r, w, k, na, b, to_cols(v), rw_state.reshape(-1, n, BRANCH, R_HEAD), l, bb)
    out_b = _rwkv_post(from_cols(yt), bonus, p, lw, n)

    cs = conv_state.transpose(1, 0, 2)
    xm, bc, xdt, da = _mamba_dec_pre(p, cs[:, :, :BRANCH], cs[:, :, BRANCH:], lw)
    ymt, ssm_new = _mamba_step(bc, to_cols(xdt), to_cols(da), ssm_state.reshape(-1, n, BRANCH, M_STATE), l, bb)
    out_m = _mamba_dec_post(from_cols(ymt), xm, p, lw)

    out_x = _xattn_decode(p3, l, mem_k, mem_v).reshape(n, BRANCH)
    x = _outproj((out_a, out_b, out_m, out_x), lw["w_out"], x, fg, n, final)
    return x, p, rw_new, ssm_new, (v if vfirst is None else vfirst)


def _cols(p, c, width=BRANCH):
    return p[..., c * width:(c + 1) * width]


def kernel(x_prompt, x_sample, cache_diff_k, cache_diff_v, cache_mem_k, cache_mem_v, state_rwkv_shift, state_rwkv, state_conv, state_ssm, page_table, mem_prompt, norm_g, w_in, w_out, lam_q1, lam_k1, lam_q2, lam_k2, subln_g, shift_mu, w0, w_up, a0, a_up, k_k, k_a, r_k, lnx_g, lnx_b, vres_0, vres_a, vres_b, conv_w, conv_b, dt_bias, a_log, d_skip, gnorm_g, mem_norm_g, w_mk, w_mv, final_g):
    prm = dict(norm_g=norm_g, w_in=w_in, w_out=w_out, lam_q1=lam_q1, lam_k1=lam_k1, lam_q2=lam_q2, lam_k2=lam_k2,
               subln_g=subln_g, shift_mu=shift_mu, w0=w0, w_up=w_up, a0=a0, a_up=a_up, k_k=k_k, k_a=k_a, r_k=r_k,
               lnx_g=lnx_g, lnx_b=lnx_b, vres_0=vres_0, vres_a=vres_a, vres_b=vres_b, conv_w=conv_w, conv_b=conv_b,
               dt_bias=dt_bias, a_log=a_log, d_skip=d_skip, gnorm_g=gnorm_g, mem_norm_g=mem_norm_g,
               w_mk=w_mk, w_mv=w_mv)
    depth = w_in.shape[0]
    nb, t, _ = x_prompt.shape
    n = x_sample.shape[0]
    n_pool = cache_diff_k.shape[1]
    ck = jnp.transpose(cache_diff_k, (0, 1, 3, 4, 5, 2)).reshape(depth, n_pool, BRANCH, PAGE_SIZE)
    cv = cache_diff_v.reshape(depth, n_pool, PAGE_SIZE * H_DIFF, DV)
    cmk = cache_mem_k.reshape(depth, n, N_MEM * H_X, D_X)
    cmv = cache_mem_v.reshape(depth, n, N_MEM * H_X, D_X)
    mem = mem_prompt.reshape(nb * N_MEM, D_MODEL)

    xp = x_prompt.reshape(nb * t, D_MODEL)
    xs = x_sample.reshape(n, D_MODEL)
    vf_p = vf_s = None
    outs = [[] for _ in range(14)]
    for l in range(depth):
        lw = _layer_weights(l, prm)
        final = l == depth - 1
        xp, p, mkv, rw_p, ssm_p, vf_p = _prompt_layer(xp, l, lw, mem, vf_p, nb, t, final_g, final,
                                                      min(1024, nb * t), min(512, t))
        p3 = p.reshape(nb, t, N_PROJ)
        last = p3[:, t - 1]
        outs[0].append(_cols(p3, C_DK).reshape(nb, t, H_DIFF, 2, DQK))
        outs[1].append(_cols(p3, C_DV).reshape(nb, t, H_DIFF, DV))
        outs[2].append(mkv[:, :BRANCH].reshape(nb, N_MEM, H_X, D_X))
        outs[3].append(mkv[:, BRANCH:].reshape(nb, N_MEM, H_X, D_X))
        outs[4].append(jnp.concatenate([_cols(last, C_RR), _cols(last, C_RK), _cols(last, C_RV),
                                        _cols(last, C_LORA128, 128)], axis=-1))
        outs[5].append(rw_p.reshape(nb, H_RWKV, R_HEAD, R_HEAD))
        tail = p3[:, t - (M_CONV - 1):]
        outs[6].append(jnp.concatenate([_cols(tail, C_MX), _cols(tail, C_MBC)], axis=-1))
        outs[7].append(ssm_p)

        caches = (ck, cv, cmk, cmv, page_table, state_rwkv_shift[l], state_rwkv, state_conv[l], state_ssm)
        xs, ps, rw_s, ssm_s, vf_s = _decode_layer(xs, l, lw, caches, vf_s, final_g, final)
        outs[8].append(_cols(ps, C_DK).reshape(n, 1, H_DIFF, 2, DQK))
        outs[9].append(_cols(ps, C_DV).reshape(n, 1, H_DIFF, DV))
        outs[10].append(jnp.concatenate([_cols(ps, C_RR), _cols(ps, C_RK), _cols(ps, C_RV),
                                         _cols(ps, C_LORA128, 128)], axis=-1))
        outs[11].append(rw_s.reshape(n, H_RWKV, R_HEAD, R_HEAD))
        outs[12].append(jnp.concatenate(
            [state_conv[l][:, 1:], jnp.concatenate([_cols(ps, C_MX), _cols(ps, C_MBC)], axis=-1)[:, None]], axis=1))
        outs[13].append(ssm_s.reshape(n, H_MAMBA, M_HEADDIM, M_STATE))

    return (xp.reshape(nb, t, D_MODEL), xs.reshape(n, 1, D_MODEL)) + tuple(jnp.stack(o) for o in outs)
```

```python
import functools
import math

import jax
import jax.numpy as jnp
from jax import lax
from jax.experimental import pallas as pl
from jax.experimental.pallas import tpu as pltpu

F32 = jnp.float32
BF16 = jnp.bfloat16
HIGHEST = lax.Precision.HIGHEST

D_MODEL = 2048
BRANCH = 512
H_DIFF = 4
DQK = 64
DV = 128
R_HEAD = 64
H_RWKV = 8
N_LORA = 128
N_SHIFT = 3 * BRANCH + N_LORA
DECAY_SCALE = 0.6065306597
LNX_EPS = 64e-5
M_HEADDIM = 64
H_MAMBA = 8
M_GROUPS = 2
M_STATE = 128
M_CONV = 4
SSD_CHUNK = 128
H_X = 4
D_X = 128
N_MEM = 256
RMS_EPS = 1e-5
PAGE_SIZE = 128

C_DQ, C_DK, C_DV, C_DG, C_RR, C_RK, C_RV, C_RG, C_MZ, C_XQ, C_XG, C_MX, C_MBC = range(13)
C_LORA128 = 13 * BRANCH // 128
C_DT128 = C_LORA128 + 1
N_PROJ = 13 * BRANCH + 2 * 128

VMEM_LIMIT = 56 * 1024 * 1024


def _cparams(sem):
    return pltpu.CompilerParams(dimension_semantics=sem, vmem_limit_bytes=VMEM_LIMIT)


def _silu(x):
    return x * jax.nn.sigmoid(x)


def _dot_t(a, b):
    return lax.dot_general(a, b, (((1,), (1,)), ((), ())), preferred_element_type=F32)


def _rms_matmul_kernel(x_ref, g_ref, w_ref, o_ref, h_ref):
    @pl.when(pl.program_id(1) == 0)
    def _():
        x = x_ref[...]
        ms = jnp.mean(x * x, axis=-1, keepdims=True)
        h_ref[...] = (x * lax.rsqrt(ms + RMS_EPS) * g_ref[...]).astype(BF16)

    o_ref[...] = jnp.dot(h_ref[...], w_ref[...], preferred_element_type=F32)


def _rms_matmul(x, g, w, tm, tn, layer=None):
    m, k = x.shape
    n = w.shape[-1]
    if layer is None:
        w_spec = pl.BlockSpec((k, tn), lambda i, j: (0, j))
    else:
        w_spec = pl.BlockSpec((None, k, tn), lambda i, j: (layer, 0, j))
    return pl.pallas_call(
        _rms_matmul_kernel,
        grid=(m // tm, n // tn),
        in_specs=[pl.BlockSpec((tm, k), lambda i, j: (i, 0)),
                  pl.BlockSpec((1, k), lambda i, j: (0, 0)),
                  w_spec],
        out_specs=pl.BlockSpec((tm, tn), lambda i, j: (i, j)),
        out_shape=jax.ShapeDtypeStruct((m, n), F32),
        scratch_shapes=[pltpu.VMEM((tm, k), BF16)],
        compiler_params=_cparams(("parallel", "arbitrary")),
        name="rms_matmul",
    )(x, g.reshape(1, k), w)


def _outproj_kernel(a_ref, b_ref, m_ref, c_ref, w_ref, x_ref, fg_ref, o_ref, *, final):
    acc = x_ref[...]
    for g, r in enumerate((a_ref, b_ref, m_ref, c_ref)):
        acc = acc + jnp.dot(r[...], w_ref[g * BRANCH:(g + 1) * BRANCH, :], preferred_element_type=F32)
    if final:
        ms = jnp.mean(acc * acc, axis=-1, keepdims=True)
        acc = acc * lax.rsqrt(ms + RMS_EPS) * fg_ref[...]
    o_ref[...] = acc


def _outproj(parts, w, x, fg, tm, final):
    m = x.shape[0]
    part_spec = pl.BlockSpec((tm, BRANCH), lambda i: (i, 0))
    return pl.pallas_call(
        functools.partial(_outproj_kernel, final=final),
        grid=(m // tm,),
        in_specs=[part_spec] * 4 + [pl.BlockSpec((D_MODEL, D_MODEL), lambda i: (0, 0)),
                                    pl.BlockSpec((tm, D_MODEL), lambda i: (i, 0)),
                                    pl.BlockSpec((1, D_MODEL), lambda i: (0, 0))],
        out_specs=pl.BlockSpec((tm, D_MODEL), lambda i: (i, 0)),
        out_shape=jax.ShapeDtypeStruct((m, D_MODEL), F32),
        compiler_params=_cparams(("parallel",)),
        name="outproj",
    )(*parts, w, x, fg.reshape(1, D_MODEL))


def _alibi_slope(h):
    return 2.0 ** (-8.0 * (h + 1) / H_DIFF)


def _lambda(lq1, lk1, lq2, lk2, lam_init):
    s1 = jnp.sum(lq1[...] * lk1[...], axis=-1, keepdims=True)
    s2 = jnp.sum(lq2[...] * lk2[...], axis=-1, keepdims=True)
    return jnp.exp(s1) - jnp.exp(s2) + lam_init


def _subln_gate(o, subg, gate, lam_init):
    ms = jnp.mean(o * o, axis=-1, keepdims=True)
    o = (o * lax.rsqrt(ms + RMS_EPS) * subg) * (1.0 - lam_init)
    return o * _silu(gate)


def _diff_prompt_kernel(lq1, lk1, lq2, lk2, subg_ref, q_ref, k_ref, v_ref, g_ref, o_ref,
                        m_ref, l_ref, acc_ref, *, tq, lam_init):
    qi = pl.program_id(1)
    ki = pl.program_id(2)

    @pl.when(ki == 0)
    def _():
        m_ref[...] = jnp.full(m_ref.shape, -1e30, F32)
        l_ref[...] = jnp.zeros(l_ref.shape, F32)
        acc_ref[...] = jnp.zeros(acc_ref.shape, F32)

    def block(masked):
        lane = lax.broadcasted_iota(jnp.int32, (tq, 2 * DQK), 1)
        slot = lane % DQK
        rows = lax.broadcasted_iota(jnp.int32, (tq, 2 * DQK), 0)
        lo = (rows % 256).astype(F32)
        hi = (rows // 256).astype(F32)
        off = ((qi - ki) * tq).astype(F32)
        zero = jnp.zeros((tq, 2 * DQK), F32)
        fq = jnp.where(slot == 0, -lo, jnp.where(slot == 1, -256.0 * hi, jnp.where(
            slot == 2, 1.0, jnp.where(slot == 3, 256.0, jnp.where(slot == 4, -off, zero)))))
        fk = jnp.where(slot == 0, 1.0, jnp.where(slot == 1, 1.0, jnp.where(
            slot == 2, lo, jnp.where(slot == 3, hi, jnp.where(slot == 4, 1.0, zero))))).astype(BF16)
        first = lane < DQK
        if masked:
            causal = (lax.broadcasted_iota(jnp.int32, (tq, tq), 0) >= lax.broadcasted_iota(jnp.int32, (tq, tq), 1))
        q = (q_ref[...] * (DQK ** -0.5)).astype(BF16)
        k = k_ref[...].astype(BF16)
        v = v_ref[...].astype(BF16)
        for h in range(H_DIFF):
            hs = slice(h * DV, (h + 1) * DV)
            fqh = (fq * _alibi_slope(h)).astype(BF16)
            qh, kh, vh = q[:, hs], k[:, hs], v[:, hs]
            for c in range(2):
                j = 2 * h + c
                keep = first if c == 0 else jnp.logical_not(first)
                s = _dot_t(jnp.where(keep, qh, fqh), jnp.where(keep, kh, fk))
                if masked:
                    s = jnp.where(causal, s, -jnp.inf)
                m_prev = m_ref[j]
                m_new = jnp.maximum(m_prev, jnp.max(s, axis=-1, keepdims=True))
                alpha = jnp.exp(m_prev - m_new)
                p = jnp.exp(s - jnp.concatenate([m_new] * (tq // DV), axis=1))
                l_ref[j] = alpha * l_ref[j] + jnp.sum(p, axis=-1, keepdims=True)
                acc_ref[j] = alpha * acc_ref[j] + jnp.dot(p.astype(BF16), vh, preferred_element_type=F32)
                m_ref[j] = m_new

    @pl.when(ki < qi)
    def _():
        block(False)

    @pl.when(ki == qi)
    def _():
        block(True)
        lam = _lambda(lq1, lk1, lq2, lk2, lam_init)
        for h in range(H_DIFF):
            o = acc_ref[2 * h] / l_ref[2 * h] - lam * (acc_ref[2 * h + 1] / l_ref[2 * h + 1])
            sl = slice(h * DV, (h + 1) * DV)
            o_ref[:, sl] = _subln_gate(o, subg_ref[...], g_ref[:, sl], lam_init).astype(o_ref.dtype)


def _diff_prompt(p, lam_params, subg, nb, t, lam_init):
    tq = min(512, t)
    nq = t // tq
    vec = pl.BlockSpec((1, DQK), lambda b, i, j: (0, 0))

    def qmap(c):
        return lambda b, i, j: (b * nq + i, c)

    def kmap(c):
        return lambda b, i, j: (b * nq + jnp.minimum(i, j), c)

    return pl.pallas_call(
        functools.partial(_diff_prompt_kernel, tq=tq, lam_init=lam_init),
        grid=(nb, nq, nq),
        in_specs=[vec, vec, vec, vec, pl.BlockSpec((1, DV), lambda b, i, j: (0, 0)),
                  pl.BlockSpec((tq, BRANCH), qmap(C_DQ)),
                  pl.BlockSpec((tq, BRANCH), kmap(C_DK)),
                  pl.BlockSpec((tq, BRANCH), kmap(C_DV)),
                  pl.BlockSpec((tq, BRANCH), qmap(C_DG))],
        out_specs=pl.BlockSpec((tq, BRANCH), lambda b, i, j: (b * nq + i, 0)),
        out_shape=jax.ShapeDtypeStruct((nb * t, BRANCH), BF16),
        scratch_shapes=[pltpu.VMEM((2 * H_DIFF, tq, DV), F32)] * 3,
        compiler_params=_cparams(("parallel", "parallel", "arbitrary")),
        name="diff_attn_prompt",
    )(*lam_params, subg.reshape(1, DV), p, p, p, p)


def _diff_decode_kernel(pt_ref, lq1, lk1, lq2, lk2, subg_ref, q_ref, kc_ref, vc_ref, g_ref, *rest,
                        n_pages, lam_init):
    kp = rest[:n_pages]
    vp = rest[n_pages:2 * n_pages]
    o_ref = rest[2 * n_pages]
    nrow = 2 * H_DIFF
    row = lax.broadcasted_iota(jnp.int32, (nrow, BRANCH), 0)
    lane = lax.broadcasted_iota(jnp.int32, (nrow, BRANCH), 1)
    q = q_ref[...] * (DQK ** -0.5)
    qm = jnp.where(lane // DQK == row, q, 0.0).astype(BF16)

    rowc = lax.broadcasted_iota(jnp.int32, (nrow, 1), 0)
    slope = jnp.zeros((nrow, 1), F32)
    for h in range(H_DIFF):
        slope = jnp.where(rowc // 2 == h, _alibi_slope(h), slope)

    past = n_pages * PAGE_SIZE
    s = jnp.concatenate([jnp.dot(qm, kp[j][...].astype(BF16), preferred_element_type=F32)
                         for j in range(n_pages)], axis=-1)
    dist = (past - lax.broadcasted_iota(jnp.int32, (nrow, past), 1)).astype(F32)
    s = s + (-slope) * dist
    kc = kc_ref[...].astype(BF16).astype(F32)
    vc = vc_ref[...].astype(BF16).astype(F32)
    sc = jnp.sum(qm.astype(F32) * kc, axis=-1, keepdims=True)
    m = jnp.maximum(jnp.max(s, axis=-1, keepdims=True), sc)
    p = jnp.exp(s - m)
    pc = jnp.exp(sc - m)
    l_fin = jnp.sum(p, axis=-1, keepdims=True) + pc
    pb = p.astype(BF16)
    pcb = pc.astype(BF16).astype(F32)
    lam = _lambda(lq1, lk1, lq2, lk2, lam_init)
    sign = jnp.where(rowc % 2 == 0, 1.0, -lam) / l_fin
    g = g_ref[...]
    outs = []
    for h in range(H_DIFF):
        hs = slice(h * DV, (h + 1) * DV)
        acc = pcb * vc[:, hs]
        for j in range(n_pages):
            vh = vp[j][pl.ds(h, PAGE_SIZE, stride=H_DIFF), :].astype(BF16)
            acc = acc + jnp.dot(pb[:, j * PAGE_SIZE:(j + 1) * PAGE_SIZE], vh, preferred_element_type=F32)
        o = jnp.sum(jnp.where(rowc // 2 == h, acc * sign, 0.0), axis=0, keepdims=True)
        outs.append(_subln_gate(o, subg_ref[...], g[:, hs], lam_init))
    o_ref[...] = jnp.concatenate(outs, axis=-1).astype(o_ref.dtype)


def _diff_decode(p3, layer, cache_k, cache_v, page_table, lam_params, subg, lam_init):
    nb, n_pages = page_table.shape
    vec = pl.BlockSpec((1, DQK), lambda b, pt: (0, 0))

    def row(c):
        return pl.BlockSpec((None, 1, BRANCH), lambda b, pt: (b, 0, c))

    def page(j, shape):
        return pl.BlockSpec((None, None) + shape, lambda b, pt: (layer, pt[b * n_pages + j], 0, 0))

    kpages = [page(j, (BRANCH, PAGE_SIZE)) for j in range(n_pages)]
    vpages = [page(j, (PAGE_SIZE * H_DIFF, DV)) for j in range(n_pages)]
    return pl.pallas_call(
        functools.partial(_diff_decode_kernel, n_pages=n_pages, lam_init=lam_init),
        grid_spec=pltpu.PrefetchScalarGridSpec(
            num_scalar_prefetch=1,
            grid=(nb,),
            in_specs=[vec, vec, vec, vec, pl.BlockSpec((1, DV), lambda b, pt: (0, 0)),
                      row(C_DQ), row(C_DK), row(C_DV), row(C_DG)] + kpages + vpages,
            out_specs=pl.BlockSpec((None, 1, BRANCH), lambda b, pt: (b, 0, 0))),
        out_shape=jax.ShapeDtypeStruct((nb, 1, BRANCH), BF16),
        compiler_params=_cparams(("parallel",)),
        name="diff_attn_decode",
    )(page_table.reshape(-1), *lam_params, subg.reshape(1, DV), p3, p3, p3, p3,
      *([cache_k] * n_pages), *([cache_v] * n_pages))


def _lane_sum3(x, ones):
    x1 = x.astype(BF16)
    rem = x - x1.astype(F32)
    x2 = rem.astype(BF16)
    x3 = (rem - x2.astype(F32)).astype(BF16)
    d = lambda z: jnp.dot(z, ones, preferred_element_type=F32)
    return d(x1) + (d(x2) + d(x3))


def _seg_sum(x, bd_ref):
    return _lane_sum3(x, bd_ref[...])


def _rwkv_pre_kernel(*refs, shift, has_vres, tm):
    it = iter(refs)
    cur = [next(it) for _ in range(4)]
    prev = None if shift else [next(it) for _ in range(4)]
    mu = [next(it) for _ in range(4)]
    w0, wup, a0, aup, kk_ref, ka_ref, rk_ref, bd_ref = [next(it) for _ in range(8)]
    if has_vres:
        vf_ref, vr0, vra, vrb = [next(it) for _ in range(4)]
    outs = [next(it) for _ in range(7)]
    carry = [next(it) for _ in range(4)] if shift else None

    xs = []
    for n in range(4):
        c = cur[n][...]
        if shift:
            @pl.when(pl.program_id(1) == 0)
            def _():
                carry[n][...] = jnp.zeros(carry[n].shape, F32)
            rolled = pltpu.roll(c, 1, 0)
            rows = lax.broadcasted_iota(jnp.int32, c.shape, 0)
            pv = jnp.where(rows == 0, carry[n][...], rolled)
            carry[n][...] = c[tm - 1:tm, :]
        else:
            pv = prev[n][...]
        xs.append(c + (pv - c) * mu[n][...])
    xr, xk, xv, xl = xs

    hw = jnp.dot(jnp.tanh(xl).astype(BF16), wup[...], preferred_element_type=F32)
    w = -DECAY_SCALE * jax.nn.sigmoid(w0[...] + hw)
    if not shift:
        w = jnp.exp(w)
    a = jax.nn.sigmoid(a0[...] + jnp.dot(xl.astype(BF16), aup[...], preferred_element_type=F32))
    if has_vres:
        t1 = jnp.dot(xv.astype(BF16), vra[...], preferred_element_type=F32)
        t2 = jnp.dot(t1.astype(BF16), vrb[...], preferred_element_type=F32)
        xv = xv + (vf_ref[...] - xv) * jax.nn.sigmoid(vr0[...] + t2)
    kk = xk * kk_ref[...]
    kk = kk / jnp.maximum(jnp.sqrt(_seg_sum(kk * kk, bd_ref)), 1e-12)
    k2 = xk * (1.0 + (a - 1.0) * ka_ref[...])
    bonus = _seg_sum(xr * k2 * rk_ref[...], bd_ref) * xv
    for o, val in zip(outs, (xr, w, k2, xv, -kk, kk * a, bonus)):
        o[...] = val


def _rwkv_pre(p, prev, lw, vfirst, nb, t, tm):
    shift = prev is None
    has_vres = vfirst is not None
    nt = t // tm
    if shift:
        grid = (nb, nt)
        rmap = lambda c: (lambda b, i: (b * nt + i, c))
        cmap = lambda b, i: (0, 0)
        sem = ("parallel", "arbitrary")
    else:
        grid = (nt,)
        rmap = lambda c: (lambda i: (i, c))
        cmap = lambda i: (0, 0)
        sem = ("parallel",)
    wide = lambda c: pl.BlockSpec((tm, BRANCH), rmap(c))
    vec = lambda n: pl.BlockSpec((1, n), cmap)
    args, specs = [], []
    args += [p, p, p, p]
    specs += [wide(C_RR), wide(C_RK), wide(C_RV), pl.BlockSpec((tm, N_LORA), rmap(C_LORA128))]
    if not shift:
        args += [prev, prev, prev, prev]
        specs += [wide(0), wide(1), wide(2), pl.BlockSpec((tm, N_LORA), rmap(3 * BRANCH // N_LORA))]
    args += lw["mu"]
    specs += [vec(BRANCH)] * 3 + [vec(N_LORA)]
    args += [lw["w0"], lw["w_up"], lw["a0"], lw["a_up"], lw["k_k"], lw["k_a"], lw["r_k"], lw["bd"]]
    specs += [vec(BRANCH), pl.BlockSpec((N_LORA, BRANCH), cmap), vec(BRANCH), pl.BlockSpec((N_LORA, BRANCH), cmap),
              vec(BRANCH), vec(BRANCH), vec(BRANCH), pl.BlockSpec((BRANCH, BRANCH), cmap)]
    if has_vres:
        args += [vfirst, lw["vres_0"], lw["vres_a"], lw["vres_b"]]
        specs += [pl.BlockSpec((tm, BRANCH), rmap(0)), vec(BRANCH),
                  pl.BlockSpec(lw["vres_a"].shape, cmap), pl.BlockSpec(lw["vres_b"].shape, cmap)]
    out_spec = pl.BlockSpec((tm, BRANCH), rmap(0))
    scratch = [pltpu.VMEM((1, BRANCH), F32)] * 3 + [pltpu.VMEM((1, N_LORA), F32)] if shift else []
    return pl.pallas_call(
        functools.partial(_rwkv_pre_kernel, shift=shift, has_vres=has_vres, tm=tm),
        grid=grid, in_specs=specs, out_specs=[out_spec] * 7,
        out_shape=[jax.ShapeDtypeStruct((nb * t, BRANCH), F32)] * 7,
        scratch_shapes=scratch,
        compiler_params=_cparams(sem),
        name="rwkv_pre",
    )(*args)


RWKV_CHUNK = 64

_NN = ((1,), (0,))
_NT = ((1,), (1,))
_TN = ((0,), (0,))


def _split(x):
    hi = x.astype(BF16)
    return hi, (x - hi.astype(F32)).astype(BF16)


def _mm3(a, b, dims):
    dg = lambda x, y: lax.dot_general(x, y, (dims, ((), ())), preferred_element_type=F32)
    return dg(a[0], b[0]) + (dg(a[0], b[1]) + dg(a[1], b[0]))


def _rwkv_scan_kernel(r_ref, lw_ref, k_ref, v_ref, a_ref, b_ref, y_ref, s_ref):
    C = RWKV_CHUNK
    W2 = 2 * R_HEAD

    @pl.when(pl.program_id(1) == 0)
    def _():
        s_ref[...] = jnp.zeros(s_ref.shape, F32)

    lw = lw_ref[...]
    tri = (lax.broadcasted_iota(jnp.int32, (C, C), 0) >= lax.broadcasted_iota(jnp.int32, (C, C), 1)).astype(BF16)
    l1 = lw.astype(BF16)
    rem = lw - l1.astype(F32)
    l2 = rem.astype(BF16)
    l3 = (rem - l2.astype(F32)).astype(BF16)
    cs = (jnp.dot(tri, l1, preferred_element_type=F32) + jnp.dot(tri, l2, preferred_element_type=F32)
          + jnp.dot(tri, l3, preferred_element_type=F32))
    p_in = jnp.exp(cs)
    p_inv = jnp.exp(-cs)
    rt = r_ref[...] * p_in
    at = a_ref[...] * jnp.exp(cs - lw)
    bt = b_ref[...] * p_inv
    kt = k_ref[...] * p_inv
    v = v_ref[...]
    p_last = p_in[C - 1:C, :]

    heads = range(H_RWKV)
    pairs = range(H_RWKV // 2)
    psl = [slice(p * W2, (p + 1) * W2) for p in pairs]
    lane2 = lax.broadcasted_iota(jnp.int32, (2 * C, W2), 1)
    keep = [lane2 < R_HEAD, lane2 >= R_HEAD]
    zb = jnp.zeros((2 * C, W2), BF16)
    gr = lax.broadcasted_iota(jnp.int32, (2 * C, 2 * C), 0)
    gc = lax.broadcasted_iota(jnp.int32, (2 * C, 2 * C), 1) % C
    gmask = jnp.where(gr < C, gr - 1, gr - C) >= gc
    second = lax.broadcasted_iota(jnp.int32, (C, W2), 1) >= R_HEAD

    ar_s = [_split(jnp.concatenate([at[:, s], rt[:, s]], axis=0)) for s in psl]
    bk_s = [_split(jnp.concatenate([bt[:, s], kt[:, s]], axis=0)) for s in psl]
    s0 = [s_ref[p] for p in pairs]
    s0_s = [_split(z) for z in s0]
    mask2 = lambda z, h: tuple(jnp.where(keep[h % 2], x, zb) for x in z[h // 2])
    arm = [mask2(ar_s, h) for h in heads]
    vh = [v[:, h * R_HEAD:(h + 1) * R_HEAD] for h in heads]
    zv_s = [_split(jnp.concatenate([jnp.zeros((C, R_HEAD), F32), z], axis=0)) for z in vh]
    g = [jnp.where(gmask, _mm3(arm[h], bk_s[h // 2], _NT), 0.0) for h in heads]
    x = [_mm3(arm[h], s0_s[h // 2], _NT) for h in heads]
    rhs = [x[h][:C] + _mm3(_split(g[h][:C]), zv_s[h], _NN) for h in heads]
    w = [jnp.concatenate([g[h][:C, :C], rhs[h]], axis=1) for h in heads]
    for _ in range(int(math.log2(C))):
        w_s = [_split(z) for z in w]
        w = [_mm3((w_s[h][0][:, :R_HEAD], w_s[h][1][:, :R_HEAD]), w_s[h], _NN) + jnp.where(second, w[h], 0.0)
             for h in heads]
    uv_s = [_split(jnp.concatenate([w[h][:, R_HEAD:], vh[h]], axis=0)) for h in heads]
    for h in heads:
        y_ref[:, h * R_HEAD:(h + 1) * R_HEAD] = x[h][C:] + _mm3(_split(g[h][C:]), uv_s[h], _NN)
    ds = [_mm3(uv_s[h], mask2(bk_s, h), _TN) for h in heads]
    for p in pairs:
        s_ref[p] = (s0[p] + ds[2 * p] + ds[2 * p + 1]) * p_last[:, psl[p]]


def _rwkv_scan(seqs, nb, t):
    tc = RWKV_CHUNK
    hp = H_RWKV // 2
    blk = pl.BlockSpec((None, tc, BRANCH), lambda b, c: (b, c, 0))
    y, s = pl.pallas_call(
        _rwkv_scan_kernel,
        grid=(nb, t // tc),
        in_specs=[blk] * 6,
        out_specs=[blk, pl.BlockSpec((hp, R_HEAD, 2 * R_HEAD), lambda b, c: (b, 0, 0))],
        out_shape=[jax.ShapeDtypeStruct((nb, t, BRANCH), F32),
                   jax.ShapeDtypeStruct((nb * hp, R_HEAD, 2 * R_HEAD), F32)],
        compiler_params=_cparams(("parallel", "arbitrary")),
        name="rwkv_scan",
    )(*seqs)
    s = s.reshape(nb, hp, R_HEAD, 2, R_HEAD).transpose(0, 1, 3, 2, 4).reshape(nb, H_RWKV, R_HEAD, R_HEAD)
    return y, s


def _rwkv_step_kernel(r_ref, w_ref, k_ref, a_ref, b_ref, vt_ref, s_ref, yt_ref, so_ref, *, bb):
    ones = jnp.ones((R_HEAD, R_HEAD), BF16)

    def expand(ref):
        return jnp.concatenate([jnp.broadcast_to(ref[j:j + 1, h * R_HEAD:(h + 1) * R_HEAD], (R_HEAD, R_HEAD))
                                for j in range(bb) for h in range(H_RWKV)], axis=0)

    st = s_ref[...].reshape(bb * BRANCH, R_HEAD)
    sa = _lane_sum3(st * expand(a_ref), ones)
    vcol = jnp.concatenate([jnp.broadcast_to(vt_ref[:, j:j + 1], (BRANCH, R_HEAD)) for j in range(bb)], axis=0)
    st = st * expand(w_ref) + sa * expand(b_ref) + vcol * expand(k_ref)
    so_ref[...] = st.reshape(bb, BRANCH, R_HEAD)
    y = _lane_sum3(st * expand(r_ref), ones)
    for j in range(bb):
        yt_ref[:, j:j + 1] = y[j * BRANCH:(j + 1) * BRANCH, 0:1]


def _rwkv_step(r, w, k, na, b, vt, state, layer, bb=8):
    n = r.shape[0]
    rowb = pl.BlockSpec((bb, BRANCH), lambda i: (i, 0))
    colb = pl.BlockSpec((None, BRANCH, bb), lambda i: (i, 0, 0))
    stb = pl.BlockSpec((bb, BRANCH, R_HEAD), lambda i: (i, 0, 0))
    return pl.pallas_call(
        functools.partial(_rwkv_step_kernel, bb=bb),
        grid=(n // bb,),
        in_specs=[rowb] * 5 + [colb, pl.BlockSpec((None, bb, BRANCH, R_HEAD), lambda i: (layer, i, 0, 0))],
        out_specs=[colb, stb],
        out_shape=[jax.ShapeDtypeStruct((n // bb, BRANCH, bb), F32),
                   jax.ShapeDtypeStruct((n, BRANCH, R_HEAD), F32)],
        compiler_params=_cparams(("parallel",)),
        name="rwkv_step",
    )(r, w, k, na, b, vt, state)


def _rwkv_post_kernel(y_ref, bonus_ref, g_ref, lg_ref, lb_ref, bd_ref, o_ref):
    y = y_ref[...]
    mu = _seg_sum(y, bd_ref) * (1.0 / R_HEAD)
    d = y - mu
    var = _seg_sum(d * d, bd_ref) * (1.0 / R_HEAD)
    o = d * lax.rsqrt(var + LNX_EPS) * lg_ref[...] + lb_ref[...] + bonus_ref[...]
    o_ref[...] = (o * _silu(g_ref[...])).astype(o_ref.dtype)


def _rwkv_post(y, bonus, p, lw, tm):
    m = y.shape[0]
    blk = pl.BlockSpec((tm, BRANCH), lambda i: (i, 0))
    vec = pl.BlockSpec((1, BRANCH), lambda i: (0, 0))
    return pl.pallas_call(
        _rwkv_post_kernel,
        grid=(m // tm,),
        in_specs=[blk, blk, pl.BlockSpec((tm, BRANCH), lambda i: (i, C_RG)), vec, vec,
                  pl.BlockSpec((BRANCH, BRANCH), lambda i: (0, 0))],
        out_specs=blk,
        out_shape=jax.ShapeDtypeStruct((m, BRANCH), BF16),
        compiler_params=_cparams(("parallel",)),
        name="rwkv_post",
    )(y, bonus, p, lw["lnx_g"], lw["lnx_b"], lw["bd"])


def _softplus(x):
    return jnp.maximum(x, 0.0) + jnp.log1p(jnp.exp(-jnp.abs(x)))


def _neg_exp_alog(alog_ref):
    lanes = lax.broadcasted_iota(jnp.int32, alog_ref.shape, 1)
    return jnp.where(lanes < H_MAMBA, -jnp.exp(alog_ref[...]), 0.0)


def _gated_norm(ym, z, g):
    u = ym * _silu(z)
    ms = jnp.mean(u * u, axis=-1, keepdims=True)
    return u * lax.rsqrt(ms + RMS_EPS) * g


def _mamba_prompt_kernel(mx_ref, mbc_ref, mdt_ref, mz_ref, cwx_ref, cwb_ref, cbx_ref, cbb_ref, dtb_ref, alog_ref,
                         dskip_ref, gn_ref, o_ref, s_ref, extx_ref, extb_ref, ym_ref):
    L = SSD_CHUNK
    pad = 8

    @pl.when(pl.program_id(1) == 0)
    def _():
        extx_ref[0:pad, :] = jnp.zeros((pad, BRANCH), F32)
        extb_ref[0:pad, :] = jnp.zeros((pad, BRANCH), F32)
        s_ref[...] = jnp.zeros(s_ref.shape, F32)

    def conv(ext_ref, cur_ref, cw_ref, cb_ref):
        ext_ref[pad:pad + L, :] = cur_ref[...]
        acc = cb_ref[...]
        for j in range(M_CONV):
            lo = pad - (M_CONV - 1) + j
            acc = acc + ext_ref[lo:lo + L, :] * cw_ref[j:j + 1, :]
        ext_ref[0:pad, :] = ext_ref[L:L + pad, :]
        return _silu(acc)

    x = conv(extx_ref, mx_ref, cwx_ref, cbx_ref)
    bc = conv(extb_ref, mbc_ref, cwb_ref, cbb_ref)
    dt = _softplus(mdt_ref[...] + dtb_ref[...])
    adt = dt * _neg_exp_alog(alog_ref)
    rowi = lax.broadcasted_iota(jnp.int32, (L, L), 0)
    coli = lax.broadcasted_iota(jnp.int32, (L, L), 1)
    lower = rowi >= coli
    acs = jnp.dot(lower.astype(F32), adt, preferred_element_type=F32, precision=HIGHEST)
    acs_t = acs.T
    last = acs[L - 1:L, :]
    dstate = jnp.exp(last - acs)
    eacs = jnp.exp(acs)
    elast = jnp.exp(last)
    bcb = bc.astype(BF16)
    grp = []
    for g in range(M_GROUPS):
        bg = bcb[:, g * M_STATE:(g + 1) * M_STATE]
        cg = bcb[:, (M_GROUPS + g) * M_STATE:(M_GROUPS + g + 1) * M_STATE]
        grp.append((bg, cg, _dot_t(cg, bg)))
    for h in range(H_MAMBA):
        bg, cg, cb = grp[h // (H_MAMBA // M_GROUPS)]
        sl = slice(h * M_HEADDIM, (h + 1) * M_HEADDIM)
        decay = jnp.where(lower, jnp.exp(acs[:, h:h + 1] - acs_t[h:h + 1, :]), 0.0)
        xh = x[:, sl]
        xdt = xh * dt[:, h:h + 1]
        yd = jnp.dot((cb * decay).astype(BF16), xdt.astype(BF16), preferred_element_type=F32)
        st = s_ref[h]
        yo = _dot_t(cg, st.astype(BF16)) * eacs[:, h:h + 1]
        upd = lax.dot_general((xdt * dstate[:, h:h + 1]).astype(BF16), bg, (((0,), (0,)), ((), ())),
                              preferred_element_type=F32)
        s_ref[h] = st * elast[:, h:h + 1] + upd
        ym_ref[:, sl] = yd + yo + dskip_ref[:, sl] * xh
    o_ref[...] = _gated_norm(ym_ref[...], mz_ref[...], gn_ref[...]).astype(o_ref.dtype)


def _mamba_prompt(p, lw, nb, t):
    L = SSD_CHUNK
    nt = t // L
    rmap = lambda c: (lambda b, i: (b * nt + i, c))
    cmap = lambda b, i: (0, 0)
    wide = lambda c: pl.BlockSpec((L, BRANCH), rmap(c))
    vec = lambda n: pl.BlockSpec((1, n), cmap)
    cw = pl.BlockSpec((M_CONV, BRANCH), cmap)
    return pl.pallas_call(
        _mamba_prompt_kernel,
        grid=(nb, nt),
        in_specs=[wide(C_MX), wide(C_MBC), pl.BlockSpec((L, 128), rmap(C_DT128)), wide(C_MZ),
                  cw, cw, vec(BRANCH), vec(BRANCH), vec(128), vec(128), vec(BRANCH), vec(BRANCH)],
        out_specs=[pl.BlockSpec((L, BRANCH), rmap(0)),
                   pl.BlockSpec((None, H_MAMBA, M_HEADDIM, M_STATE), lambda b, i: (b, 0, 0, 0))],
        out_shape=[jax.ShapeDtypeStruct((nb * t, BRANCH), BF16),
                   jax.ShapeDtypeStruct((nb, H_MAMBA, M_HEADDIM, M_STATE), F32)],
        scratch_shapes=[pltpu.VMEM((L + 8, BRANCH), F32), pltpu.VMEM((L + 8, BRANCH), F32),
                        pltpu.VMEM((L, BRANCH), F32)],
        compiler_params=_cparams(("parallel", "arbitrary")),
        name="mamba_prompt",
    )(p, p, p, p, lw["cw_x"], lw["cw_bc"], lw["cb_x"], lw["cb_bc"], lw["dt_bias"], lw["a_log"],
      lw["d_skip"], lw["gnorm_g"])


def _mamba_dec_pre_kernel(mx_ref, mbc_ref, mdt_ref, sx_ref, sb_ref, cwx_ref, cwb_ref, cbx_ref, cbb_ref,
                          dtb_ref, alog_ref, x_ref, bc_ref, xdt_ref, da_ref):
    def conv(st_ref, cur_ref, cw_ref, cb_ref):
        acc = cb_ref[...]
        for j in range(M_CONV - 1):
            acc = acc + st_ref[j] * cw_ref[j:j + 1, :]
        acc = acc + cur_ref[...] * cw_ref[M_CONV - 1:M_CONV, :]
        return _silu(acc)

    x = conv(sx_ref, mx_ref, cwx_ref, cbx_ref)
    x_ref[...] = x
    bc_ref[...] = conv(sb_ref, mbc_ref, cwb_ref, cbb_ref)
    dt = _softplus(mdt_ref[...] + dtb_ref[...])
    da = jnp.exp(dt * _neg_exp_alog(alog_ref))
    n = x.shape[0]
    for h in range(H_MAMBA):
        sl = slice(h * M_HEADDIM, (h + 1) * M_HEADDIM)
        xdt_ref[:, sl] = x[:, sl] * dt[:, h:h + 1]
        da_ref[:, sl] = jnp.broadcast_to(da[:, h:h + 1], (n, M_HEADDIM))


def _mamba_dec_pre(p, conv_x, conv_bc, lw):
    n = p.shape[0]
    cmap = lambda i: (0, 0)
    wide = lambda c: pl.BlockSpec((n, BRANCH), lambda i: (0, c))
    vec = lambda m: pl.BlockSpec((1, m), cmap)
    cw = pl.BlockSpec((M_CONV, BRANCH), cmap)
    st = pl.BlockSpec((M_CONV - 1, n, BRANCH), lambda i: (0, 0, 0))
    out = pl.BlockSpec((n, BRANCH), cmap)
    return pl.pallas_call(
        _mamba_dec_pre_kernel,
        grid=(1,),
        in_specs=[wide(C_MX), wide(C_MBC), pl.BlockSpec((n, 128), lambda i: (0, C_DT128)), st, st,
                  cw, cw, vec(BRANCH), vec(BRANCH), vec(128), vec(128)],
        out_specs=[out] * 4,
        out_shape=[jax.ShapeDtypeStruct((n, BRANCH), F32)] * 4,
        compiler_params=_cparams(("arbitrary",)),
        name="mamba_dec_pre",
    )(p, p, p, conv_x, conv_bc, lw["cw_x"], lw["cw_bc"], lw["cb_x"], lw["cb_bc"], lw["dt_bias"], lw["a_log"])


def _mamba_step_kernel(bc_ref, xdtt_ref, dat_ref, s_ref, yt_ref, so_ref, *, bb):
    units = [(j, h // (H_MAMBA // M_GROUPS), slice(h * M_HEADDIM, (h + 1) * M_HEADDIM))
             for j in range(bb) for h in range(H_MAMBA)]
    st = []
    for j, g, sl in units:
        bg = bc_ref[j:j + 1, g * M_STATE:(g + 1) * M_STATE]
        z = s_ref[j, sl, :] * dat_ref[sl, j:j + 1] + xdtt_ref[sl, j:j + 1] * bg
        so_ref[j, sl, :] = z
        st.append(z)
    ys = [jnp.sum(z * bc_ref[j:j + 1, (M_GROUPS + g) * M_STATE:(M_GROUPS + g + 1) * M_STATE], axis=-1, keepdims=True)
          for z, (j, g, sl) in zip(st, units)]
    for y, (j, g, sl) in zip(ys, units):
        yt_ref[sl, j:j + 1] = y


def _mamba_step(bc, xdtt, dat, state, layer, bb=8):
    n = bc.shape[0]
    rowb = pl.BlockSpec((bb, BRANCH), lambda i: (i, 0))
    colb = pl.BlockSpec((None, BRANCH, bb), lambda i: (i, 0, 0))
    stb = pl.BlockSpec((bb, BRANCH, M_STATE), lambda i: (i, 0, 0))
    return pl.pallas_call(
        functools.partial(_mamba_step_kernel, bb=bb),
        grid=(n // bb,),
        in_specs=[rowb, colb, colb, pl.BlockSpec((None, bb, BRANCH, M_STATE), lambda i: (layer, i, 0, 0))],
        out_specs=[colb, stb],
        out_shape=[jax.ShapeDtypeStruct((n // bb, BRANCH, bb), F32),
                   jax.ShapeDtypeStruct((n, BRANCH, M_STATE), F32)],
        compiler_params=_cparams(("parallel",)),
        name="mamba_step",
    )(bc, xdtt, dat, state)


def _mamba_dec_post_kernel(y_ref, x_ref, z_ref, dskip_ref, gn_ref, o_ref):
    ym = y_ref[...] + dskip_ref[...] * x_ref[...]
    o_ref[...] = _gated_norm(ym, z_ref[...], gn_ref[...]).astype(o_ref.dtype)


def _mamba_dec_post(y, x, p, lw):
    n = y.shape[0]
    blk = pl.BlockSpec((n, BRANCH), lambda i: (0, 0))
    vec = pl.BlockSpec((1, BRANCH), lambda i: (0, 0))
    return pl.pallas_call(
        _mamba_dec_post_kernel,
        grid=(1,),
        in_specs=[blk, blk, pl.BlockSpec((n, BRANCH), lambda i: (0, C_MZ)), vec, vec],
        out_specs=blk,
        out_shape=jax.ShapeDtypeStruct((n, BRANCH), BF16),
        compiler_params=_cparams(("arbitrary",)),
        name="mamba_dec_post",
    )(y, x, p, lw["d_skip"], lw["gnorm_g"])


def _xattn_prompt_kernel(q_ref, g_ref, mk_ref, mv_ref, o_ref):
    q = q_ref[...].astype(BF16)
    mk = mk_ref[...].astype(BF16)
    mv = mv_ref[...].astype(BF16)
    for h in range(H_X):
        sl = slice(h * D_X, (h + 1) * D_X)
        s = _dot_t(q[:, sl], mk[:, sl]) * (D_X ** -0.5)
        e = jnp.exp(s - jnp.max(s, axis=-1, keepdims=True))
        o = jnp.dot(e.astype(BF16), mv[:, sl], preferred_element_type=F32) / jnp.sum(e, axis=-1, keepdims=True)
        o_ref[:, sl] = (o * _silu(g_ref[:, sl])).astype(o_ref.dtype)


def _xattn_prompt(p, mkv, nb, t):
    tq = min(512, t)
    nq = t // tq
    return pl.pallas_call(
        _xattn_prompt_kernel,
        grid=(nb, nq),
        in_specs=[pl.BlockSpec((tq, BRANCH), lambda b, i: (b * nq + i, C_XQ)),
                  pl.BlockSpec((tq, BRANCH), lambda b, i: (b * nq + i, C_XG)),
                  pl.BlockSpec((N_MEM, BRANCH), lambda b, i: (b, 0)),
                  pl.BlockSpec((N_MEM, BRANCH), lambda b, i: (b, 1))],
        out_specs=pl.BlockSpec((tq, BRANCH), lambda b, i: (b * nq + i, 0)),
        out_shape=jax.ShapeDtypeStruct((nb * t, BRANCH), BF16),
        compiler_params=_cparams(("parallel", "parallel")),
        name="xattn_prompt",
    )(p, p, mkv, mkv)


def _xattn_decode_kernel(q_ref, g_ref, mk_ref, mv_ref, o_ref, *, bb):
    nrow = 8
    units = [(j, h) for j in range(bb) for h in range(H_X)]
    q = [q_ref[j].astype(BF16) for j in range(bb)]
    s = [_dot_t(jnp.broadcast_to(q[j][:, h * D_X:(h + 1) * D_X], (nrow, D_X)),
                mk_ref[j, pl.ds(h, N_MEM, stride=H_X), :].astype(BF16)) * (D_X ** -0.5) for j, h in units]
    e = [jnp.exp(z - jnp.max(z, axis=-1, keepdims=True)) for z in s]
    r = [jnp.dot(z.astype(BF16), mv_ref[j, pl.ds(h, N_MEM, stride=H_X), :].astype(BF16),
                 preferred_element_type=F32) for z, (j, h) in zip(e, units)]
    outs = [(x / jnp.sum(z, axis=-1, keepdims=True))[0:1, :] for x, z in zip(r, e)]
    for j in range(bb):
        o = jnp.concatenate(outs[j * H_X:(j + 1) * H_X], axis=-1)
        o_ref[j] = (o * _silu(g_ref[j])).astype(o_ref.dtype)


def _xattn_decode(p3, layer, mem_k, mem_v, bb=4):
    n = p3.shape[0]
    mem = pl.BlockSpec((None, bb, N_MEM * H_X, D_X), lambda b: (layer, b, 0, 0))
    return pl.pallas_call(
        functools.partial(_xattn_decode_kernel, bb=bb),
        grid=(n // bb,),
        in_specs=[pl.BlockSpec((bb, 1, BRANCH), lambda b: (b, 0, C_XQ)),
                  pl.BlockSpec((bb, 1, BRANCH), lambda b: (b, 0, C_XG)), mem, mem],
        out_specs=pl.BlockSpec((bb, 1, BRANCH), lambda b: (b, 0, 0)),
        out_shape=jax.ShapeDtypeStruct((n, 1, BRANCH), BF16),
        compiler_params=_cparams(("parallel",)),
        name="xattn_decode",
    )(p3, p3, mem_k, mem_v)


N_IN = 8 * BRANCH + N_SHIFT + 2 * BRANCH + H_MAMBA


def _reorder_w_in_kernel(w_ref, o_ref):
    o_rs, o_rg, o_mz, o_mxbc, o_dt, o_xq = 2048, 3712, 4224, 4736, 5760, 5768
    o_lora = o_rs + 3 * BRANCH

    def put(dst, val):
        o_ref[:, dst:dst + val.shape[1]] = val.astype(o_ref.dtype)

    put(0, w_ref[:, :o_lora])
    put(C_RG * BRANCH, w_ref[:, o_rg:o_mxbc])
    tail = w_ref[:, o_dt:]
    put(C_XQ * BRANCH, tail[:, o_xq - o_dt:])
    put(C_MX * BRANCH, w_ref[:, o_mxbc:o_dt])
    put(C_LORA128 * 128, w_ref[:, o_lora:o_rg])
    lanes = lax.broadcasted_iota(jnp.int32, (w_ref.shape[0], 128), 1)
    put(C_DT128 * 128, jnp.where(lanes < H_MAMBA, tail[:, :128], 0.0))


def _reorder_w_in(w_in, tm=256):
    depth = w_in.shape[0]
    return pl.pallas_call(
        _reorder_w_in_kernel,
        grid=(depth, D_MODEL // tm),
        in_specs=[pl.BlockSpec((None, tm, N_IN), lambda l, i: (l, i, 0))],
        out_specs=pl.BlockSpec((None, tm, N_PROJ), lambda l, i: (l, i, 0)),
        out_shape=jax.ShapeDtypeStruct((depth, D_MODEL, N_PROJ), BF16),
        compiler_params=_cparams(("parallel", "parallel")),
        name="reorder_w_in",
    )(w_in)


def _layer_weights(l, prm):
    row = lambda v: v.reshape(1, -1).astype(F32)
    mu = prm["shift_mu"][l].astype(F32)
    seg = jnp.arange(BRANCH) // R_HEAD
    zl = jnp.zeros((N_LORA // 2, BRANCH), F32)
    cw = prm["conv_w"][l].astype(F32)
    cb = prm["conv_b"][l].astype(F32)
    pad8 = lambda v: jnp.pad(v.astype(F32), (0, 128 - H_MAMBA)).reshape(1, 128)
    lw = {
        "w_in": prm["w_in_reordered"], "norm_g": prm["norm_g"][l], "w_out": prm["w_out"][l].astype(BF16),
        "lam": [row(prm[n][l]) for n in ("lam_q1", "lam_k1", "lam_q2", "lam_k2")],
        "subln_g": prm["subln_g"][l].astype(F32),
        "mu": [row(mu[:BRANCH]), row(mu[BRANCH:2 * BRANCH]), row(mu[2 * BRANCH:3 * BRANCH]), row(mu[3 * BRANCH:])],
        "w0": row(prm["w0"][l]), "a0": row(prm["a0"][l]),
        "w_up": jnp.concatenate([prm["w_up"][l].astype(F32), zl], axis=0).astype(BF16),
        "a_up": jnp.concatenate([zl, prm["a_up"][l].astype(F32)], axis=0).astype(BF16),
        "k_k": row(prm["k_k"][l]), "k_a": row(prm["k_a"][l]), "r_k": row(prm["r_k"][l]),
        "lnx_g": row(prm["lnx_g"][l]), "lnx_b": row(prm["lnx_b"][l]),
        "bd": (seg[:, None] == seg[None, :]).astype(BF16),
        "cw_x": cw[:, :BRANCH], "cw_bc": cw[:, BRANCH:], "cb_x": row(cb[:BRANCH]), "cb_bc": row(cb[BRANCH:]),
        "dt_bias": pad8(prm["dt_bias"][l]), "a_log": pad8(prm["a_log"][l]),
        "d_skip": row(jnp.repeat(prm["d_skip"][l].astype(F32), M_HEADDIM)),
        "gnorm_g": row(prm["gnorm_g"][l]),
        "w_mkv": jnp.concatenate([prm["w_mk"][l], prm["w_mv"][l]], axis=1).astype(BF16),
        "mem_norm_g": prm["mem_norm_g"][l],
    }
    if l > 0:
        lw["vres_0"] = row(prm["vres_0"][l - 1])
        lw["vres_a"] = prm["vres_a"][l - 1].astype(BF16)
        lw["vres_b"] = prm["vres_b"][l - 1].astype(BF16)
    return lw


def _lam_init(l):
    return 0.8 - 0.6 * math.exp(-0.3 * l)


def _prompt_layer(x, l, lw, mem, vfirst, nb, t, fg, final, tm_proj, tm_row):
    p = _rms_matmul(x, lw["norm_g"], lw["w_in"], tm_proj, 1152, layer=l)
    mkv = _rms_matmul(mem, lw["mem_norm_g"], lw["w_mkv"], mem.shape[0], 2 * BRANCH)
    out_a = _diff_prompt(p, lw["lam"], lw["subln_g"], nb, t, _lam_init(l))
    r, w, k, v, na, b, bonus = _rwkv_pre(p, None, lw, vfirst, nb, t, tm_row)
    seqs = [z.reshape(nb, t, BRANCH) for z in (r, w, k, v, na, b)]
    y, rw_state = _rwkv_scan(seqs, nb, t)
    out_b = _rwkv_post(y.reshape(nb * t, BRANCH), bonus, p, lw, tm_row)
    out_m, ssm_state = _mamba_prompt(p, lw, nb, t)
    out_x = _xattn_prompt(p, mkv, nb, t)
    x = _outproj((out_a, out_b, out_m, out_x), lw["w_out"], x, fg, tm_row, final)
    return x, p, mkv, rw_state, ssm_state, (v if vfirst is None else vfirst)


def _decode_layer(x, l, lw, caches, vfirst, fg, final):
    cache_k, cache_v, mem_k, mem_v, page_table, rw_prev, rw_state, conv_state, ssm_state = caches
    n = x.shape[0]
    bb = 8
    p = _rms_matmul(x, lw["norm_g"], lw["w_in"], n, 1152, layer=l)
    p3 = p.reshape(n, 1, N_PROJ)
    out_a = _diff_decode(p3, l, cache_k, cache_v, page_table, lw["lam"], lw["subln_g"], _lam_init(l)).reshape(n, BRANCH)

    r, w, k, v, na, b, bonus = _rwkv_pre(p, rw_prev, lw, vfirst, 1, n, n)
    to_cols = lambda z: z.reshape(n // bb, bb, BRANCH).transpose(0, 2, 1)
    from_cols = lambda z: z.transpose(0, 2, 1).reshape(n, BRANCH)
    yt, rw_new = _rwkv_step(r, w, k, na, b, to_cols(v), rw_state.reshape(-1, n, BRANCH, R_HEAD), l, bb)
    out_b = _rwkv_post(from_cols(yt), bonus, p, lw, n)

    cs = conv_state.transpose(1, 0, 2)
    xm, bc, xdt, da = _mamba_dec_pre(p, cs[:, :, :BRANCH], cs[:, :, BRANCH:], lw)
    ymt, ssm_new = _mamba_step(bc, to_cols(xdt), to_cols(da), ssm_state.reshape(-1, n, BRANCH, M_STATE), l, bb)
    out_m = _mamba_dec_post(from_cols(ymt), xm, p, lw)

    out_x = _xattn_decode(p3, l, mem_k, mem_v).reshape(n, BRANCH)
    x = _outproj((out_a, out_b, out_m, out_x), lw["w_out"], x, fg, n, final)
    return x, p, rw_new, ssm_new, (v if vfirst is None else vfirst)


def _cols(p, c, width=BRANCH):
    return p[..., c * width:(c + 1) * width]


def kernel(x_prompt, x_sample, cache_diff_k, cache_diff_v, cache_mem_k, cache_mem_v, state_rwkv_shift, state_rwkv, state_conv, state_ssm, page_table, mem_prompt, norm_g, w_in, w_out, lam_q1, lam_k1, lam_q2, lam_k2, subln_g, shift_mu, w0, w_up, a0, a_up, k_k, k_a, r_k, lnx_g, lnx_b, vres_0, vres_a, vres_b, conv_w, conv_b, dt_bias, a_log, d_skip, gnorm_g, mem_norm_g, w_mk, w_mv, final_g):
    prm = dict(norm_g=norm_g, w_in=w_in, w_out=w_out, lam_q1=lam_q1, lam_k1=lam_k1, lam_q2=lam_q2, lam_k2=lam_k2,
               subln_g=subln_g, shift_mu=shift_mu, w0=w0, w_up=w_up, a0=a0, a_up=a_up, k_k=k_k, k_a=k_a, r_k=r_k,
               lnx_g=lnx_g, lnx_b=lnx_b, vres_0=vres_0, vres_a=vres_a, vres_b=vres_b, conv_w=conv_w, conv_b=conv_b,
               dt_bias=dt_bias, a_log=a_log, d_skip=d_skip, gnorm_g=gnorm_g, mem_norm_g=mem_norm_g,
               w_mk=w_mk, w_mv=w_mv, w_in_reordered=_reorder_w_in(w_in))
    depth = w_in.shape[0]
    nb, t, _ = x_prompt.shape
    n = x_sample.shape[0]
    n_pool = cache_diff_k.shape[1]
    ck = jnp.transpose(cache_diff_k, (0, 1, 3, 4, 5, 2)).reshape(depth, n_pool, BRANCH, PAGE_SIZE)
    cv = cache_diff_v.reshape(depth, n_pool, PAGE_SIZE * H_DIFF, DV)
    cmk = cache_mem_k.reshape(depth, n, N_MEM * H_X, D_X)
    cmv = cache_mem_v.reshape(depth, n, N_MEM * H_X, D_X)
    mem = mem_prompt.reshape(nb * N_MEM, D_MODEL)

    xp = x_prompt.reshape(nb * t, D_MODEL)
    xs = x_sample.reshape(n, D_MODEL)
    vf_p = vf_s = None
    outs = [[] for _ in range(14)]
    for l in range(depth):
        lw = _layer_weights(l, prm)
        final = l == depth - 1
        xp, p, mkv, rw_p, ssm_p, vf_p = _prompt_layer(xp, l, lw, mem, vf_p, nb, t, final_g, final,
                                                      min(1024, nb * t), min(512, t))
        p3 = p.reshape(nb, t, N_PROJ)
        last = p3[:, t - 1]
        outs[0].append(_cols(p3, C_DK).reshape(nb, t, H_DIFF, 2, DQK))
        outs[1].append(_cols(p3, C_DV).reshape(nb, t, H_DIFF, DV))
        outs[2].append(mkv[:, :BRANCH].reshape(nb, N_MEM, H_X, D_X))
        outs[3].append(mkv[:, BRANCH:].reshape(nb, N_MEM, H_X, D_X))
        outs[4].append(jnp.concatenate([_cols(last, C_RR), _cols(last, C_RK), _cols(last, C_RV),
                                        _cols(last, C_LORA128, 128)], axis=-1))
        outs[5].append(rw_p.reshape(nb, H_RWKV, R_HEAD, R_HEAD))
        tail = p3[:, t - (M_CONV - 1):]
        outs[6].append(jnp.concatenate([_cols(tail, C_MX), _cols(tail, C_MBC)], axis=-1))
        outs[7].append(ssm_p)

        caches = (ck, cv, cmk, cmv, page_table, state_rwkv_shift[l], state_rwkv, state_conv[l], state_ssm)
        xs, ps, rw_s, ssm_s, vf_s = _decode_layer(xs, l, lw, caches, vf_s, final_g, final)
        outs[8].append(_cols(ps, C_DK).reshape(n, 1, H_DIFF, 2, DQK))
        outs[9].append(_cols(ps, C_DV).reshape(n, 1, H_DIFF, DV))
        outs[10].append(jnp.concatenate([_cols(ps, C_RR), _cols(ps, C_RK), _cols(ps, C_RV),
                                         _cols(ps, C_LORA128, 128)], axis=-1))
        outs[11].append(rw_s.reshape(n, H_RWKV, R_HEAD, R_HEAD))
        outs[12].append(jnp.concatenate(
            [state_conv[l][:, 1:], jnp.concatenate([_cols(ps, C_MX), _cols(ps, C_MBC)], axis=-1)[:, None]], axis=1))
        outs[13].append(ssm_s.reshape(n, H_MAMBA, M_HEADDIM, M_STATE))

    return (xp.reshape(nb, t, D_MODEL), xs.reshape(n, 1, D_MODEL)) + tuple(jnp.stack(o) for o in outs)
```

```python
import functools
import math

import jax
import jax.numpy as jnp
from jax import lax
from jax.experimental import pallas as pl
from jax.experimental.pallas import tpu as pltpu

F32 = jnp.float32
BF16 = jnp.bfloat16
HIGHEST = lax.Precision.HIGHEST

D_MODEL = 2048
BRANCH = 512
H_DIFF = 4
DQK = 64
DV = 128
R_HEAD = 64
H_RWKV = 8
N_LORA = 128
N_SHIFT = 3 * BRANCH + N_LORA
DECAY_SCALE = 0.6065306597
LNX_EPS = 64e-5
M_HEADDIM = 64
H_MAMBA = 8
M_GROUPS = 2
M_STATE = 128
M_CONV = 4
SSD_CHUNK = 128
H_X = 4
D_X = 128
N_MEM = 256
RMS_EPS = 1e-5
PAGE_SIZE = 128

C_DQ, C_DK, C_DV, C_DG, C_RR, C_RK, C_RV, C_RG, C_MZ, C_XQ, C_XG, C_MX, C_MBC = range(13)
C_LORA128 = 13 * BRANCH // 128
C_DT128 = C_LORA128 + 1
N_PROJ = 13 * BRANCH + 2 * 128

VMEM_LIMIT = 56 * 1024 * 1024


def _cparams(sem):
    return pltpu.CompilerParams(dimension_semantics=sem, vmem_limit_bytes=VMEM_LIMIT)


def _silu(x):
    return x * jax.nn.sigmoid(x)


def _dot_t(a, b):
    return lax.dot_general(a, b, (((1,), (1,)), ((), ())), preferred_element_type=F32)


def _rms_matmul_kernel(x_ref, g_ref, w_ref, o_ref, h_ref):
    @pl.when(pl.program_id(1) == 0)
    def _():
        x = x_ref[...]
        ms = jnp.mean(x * x, axis=-1, keepdims=True)
        h_ref[...] = (x * lax.rsqrt(ms + RMS_EPS) * g_ref[...]).astype(BF16)

    o_ref[...] = jnp.dot(h_ref[...], w_ref[...], preferred_element_type=F32)


def _rms_matmul(x, g, w, tm, tn, layer=None):
    m, k = x.shape
    n = w.shape[-1]
    if layer is None:
        w_spec = pl.BlockSpec((k, tn), lambda i, j: (0, j))
    else:
        w_spec = pl.BlockSpec((None, k, tn), lambda i, j: (layer, 0, j))
    return pl.pallas_call(
        _rms_matmul_kernel,
        grid=(m // tm, n // tn),
        in_specs=[pl.BlockSpec((tm, k), lambda i, j: (i, 0)),
                  pl.BlockSpec((1, k), lambda i, j: (0, 0)),
                  w_spec],
        out_specs=pl.BlockSpec((tm, tn), lambda i, j: (i, j)),
        out_shape=jax.ShapeDtypeStruct((m, n), F32),
        scratch_shapes=[pltpu.VMEM((tm, k), BF16)],
        compiler_params=_cparams(("parallel", "arbitrary")),
        name="rms_matmul",
    )(x, g.reshape(1, k), w)


def _outproj_kernel(a_ref, b_ref, m_ref, c_ref, w_ref, x_ref, fg_ref, o_ref, *, final):
    acc = x_ref[...]
    for g, r in enumerate((a_ref, b_ref, m_ref, c_ref)):
        acc = acc + jnp.dot(r[...], w_ref[g * BRANCH:(g + 1) * BRANCH, :], preferred_element_type=F32)
    if final:
        ms = jnp.mean(acc * acc, axis=-1, keepdims=True)
        acc = acc * lax.rsqrt(ms + RMS_EPS) * fg_ref[...]
    o_ref[...] = acc


def _outproj(parts, w, x, fg, tm, final):
    m = x.shape[0]
    part_spec = pl.BlockSpec((tm, BRANCH), lambda i: (i, 0))
    return pl.pallas_call(
        functools.partial(_outproj_kernel, final=final),
        grid=(m // tm,),
        in_specs=[part_spec] * 4 + [pl.BlockSpec((D_MODEL, D_MODEL), lambda i: (0, 0)),
                                    pl.BlockSpec((tm, D_MODEL), lambda i: (i, 0)),
                                    pl.BlockSpec((1, D_MODEL), lambda i: (0, 0))],
        out_specs=pl.BlockSpec((tm, D_MODEL), lambda i: (i, 0)),
        out_shape=jax.ShapeDtypeStruct((m, D_MODEL), F32),
        compiler_params=_cparams(("parallel",)),
        name="outproj",
    )(*parts, w, x, fg.reshape(1, D_MODEL))


def _alibi_slope(h):
    return 2.0 ** (-8.0 * (h + 1) / H_DIFF)


def _lambda(lq1, lk1, lq2, lk2, lam_init):
    s1 = jnp.sum(lq1[...] * lk1[...], axis=-1, keepdims=True)
    s2 = jnp.sum(lq2[...] * lk2[...], axis=-1, keepdims=True)
    return jnp.exp(s1) - jnp.exp(s2) + lam_init


def _subln_gate(o, subg, gate, lam_init):
    ms = jnp.mean(o * o, axis=-1, keepdims=True)
    o = (o * lax.rsqrt(ms + RMS_EPS) * subg) * (1.0 - lam_init)
    return o * _silu(gate)


def _diff_prompt_kernel(lq1, lk1, lq2, lk2, subg_ref, q_ref, k_ref, v_ref, g_ref, o_ref,
                        m_ref, l_ref, acc_ref, *, tq, lam_init):
    qi = pl.program_id(1)
    ki = pl.program_id(2)

    @pl.when(ki == 0)
    def _():
        m_ref[...] = jnp.full(m_ref.shape, -1e30, F32)
        l_ref[...] = jnp.zeros(l_ref.shape, F32)
        acc_ref[...] = jnp.zeros(acc_ref.shape, F32)

    def block(masked):
        lane = lax.broadcasted_iota(jnp.int32, (tq, 2 * DQK), 1)
        slot = lane % DQK
        rows = lax.broadcasted_iota(jnp.int32, (tq, 2 * DQK), 0)
        lo = (rows % 256).astype(F32)
        hi = (rows // 256).astype(F32)
        off = ((qi - ki) * tq).astype(F32)
        zero = jnp.zeros((tq, 2 * DQK), F32)
        fq = jnp.where(slot == 0, -lo, jnp.where(slot == 1, -256.0 * hi, jnp.where(
            slot == 2, 1.0, jnp.where(slot == 3, 256.0, jnp.where(slot == 4, -off, zero)))))
        fk = jnp.where(slot == 0, 1.0, jnp.where(slot == 1, 1.0, jnp.where(
            slot == 2, lo, jnp.where(slot == 3, hi, jnp.where(slot == 4, 1.0, zero))))).astype(BF16)
        first = lane < DQK
        if masked:
            causal = (lax.broadcasted_iota(jnp.int32, (tq, tq), 0) >= lax.broadcasted_iota(jnp.int32, (tq, tq), 1))
        q = (q_ref[...] * (DQK ** -0.5)).astype(BF16)
        k = k_ref[...].astype(BF16)
        v = v_ref[...].astype(BF16)
        for h in range(H_DIFF):
            hs = slice(h * DV, (h + 1) * DV)
            fqh = (fq * _alibi_slope(h)).astype(BF16)
            qh, kh, vh = q[:, hs], k[:, hs], v[:, hs]
            for c in range(2):
                j = 2 * h + c
                keep = first if c == 0 else jnp.logical_not(first)
                s = _dot_t(jnp.where(keep, qh, fqh), jnp.where(keep, kh, fk))
                if masked:
                    s = jnp.where(causal, s, -jnp.inf)
                m_prev = m_ref[j]
                m_new = jnp.maximum(m_prev, jnp.max(s, axis=-1, keepdims=True))
                alpha = jnp.exp(m_prev - m_new)
                p = jnp.exp(s - jnp.concatenate([m_new] * (tq // DV), axis=1))
                l_ref[j] = alpha * l_ref[j] + jnp.sum(p, axis=-1, keepdims=True)
                acc_ref[j] = alpha * acc_ref[j] + jnp.dot(p.astype(BF16), vh, preferred_element_type=F32)
                m_ref[j] = m_new

    @pl.when(ki < qi)
    def _():
        block(False)

    @pl.when(ki == qi)
    def _():
        block(True)
        lam = _lambda(lq1, lk1, lq2, lk2, lam_init)
        for h in range(H_DIFF):
            o = acc_ref[2 * h] / l_ref[2 * h] - lam * (acc_ref[2 * h + 1] / l_ref[2 * h + 1])
            sl = slice(h * DV, (h + 1) * DV)
            o_ref[:, sl] = _subln_gate(o, subg_ref[...], g_ref[:, sl], lam_init).astype(o_ref.dtype)


def _diff_prompt(p, lam_params, subg, nb, t, lam_init):
    tq = min(512, t)
    nq = t // tq
    vec = pl.BlockSpec((1, DQK), lambda b, i, j: (0, 0))

    def qmap(c):
        return lambda b, i, j: (b * nq + i, c)

    def kmap(c):
        return lambda b, i, j: (b * nq + jnp.minimum(i, j), c)

    return pl.pallas_call(
        functools.partial(_diff_prompt_kernel, tq=tq, lam_init=lam_init),
        grid=(nb, nq, nq),
        in_specs=[vec, vec, vec, vec, pl.BlockSpec((1, DV), lambda b, i, j: (0, 0)),
                  pl.BlockSpec((tq, BRANCH), qmap(C_DQ)),
                  pl.BlockSpec((tq, BRANCH), kmap(C_DK)),
                  pl.BlockSpec((tq, BRANCH), kmap(C_DV)),
                  pl.BlockSpec((tq, BRANCH), qmap(C_DG))],
        out_specs=pl.BlockSpec((tq, BRANCH), lambda b, i, j: (b * nq + i, 0)),
        out_shape=jax.ShapeDtypeStruct((nb * t, BRANCH), BF16),
        scratch_shapes=[pltpu.VMEM((2 * H_DIFF, tq, DV), F32)] * 3,
        compiler_params=_cparams(("parallel", "parallel", "arbitrary")),
        name="diff_attn_prompt",
    )(*lam_params, subg.reshape(1, DV), p, p, p, p)


def _diff_decode_kernel(pt_ref, lq1, lk1, lq2, lk2, subg_ref, q_ref, kc_ref, vc_ref, g_ref, *rest,
                        n_pages, lam_init):
    kp = rest[:n_pages]
    vp = rest[n_pages:2 * n_pages]
    o_ref = rest[2 * n_pages]
    nrow = 2 * H_DIFF
    row = lax.broadcasted_iota(jnp.int32, (nrow, BRANCH), 0)
    lane = lax.broadcasted_iota(jnp.int32, (nrow, BRANCH), 1)
    q = q_ref[...] * (DQK ** -0.5)
    qm = jnp.where(lane // DQK == row, q, 0.0).astype(BF16)

    rowc = lax.broadcasted_iota(jnp.int32, (nrow, 1), 0)
    slope = jnp.zeros((nrow, 1), F32)
    for h in range(H_DIFF):
        slope = jnp.where(rowc // 2 == h, _alibi_slope(h), slope)

    past = n_pages * PAGE_SIZE
    s = jnp.concatenate([jnp.dot(qm, kp[j][...].astype(BF16), preferred_element_type=F32)
                         for j in range(n_pages)], axis=-1)
    dist = (past - lax.broadcasted_iota(jnp.int32, (nrow, past), 1)).astype(F32)
    s = s + (-slope) * dist
    kc = kc_ref[...].astype(BF16).astype(F32)
    vc = vc_ref[...].astype(BF16).astype(F32)
    sc = jnp.sum(qm.astype(F32) * kc, axis=-1, keepdims=True)
    m = jnp.maximum(jnp.max(s, axis=-1, keepdims=True), sc)
    p = jnp.exp(s - m)
    pc = jnp.exp(sc - m)
    l_fin = jnp.sum(p, axis=-1, keepdims=True) + pc
    pb = p.astype(BF16)
    pcb = pc.astype(BF16).astype(F32)
    lam = _lambda(lq1, lk1, lq2, lk2, lam_init)
    sign = jnp.where(rowc % 2 == 0, 1.0, -lam) / l_fin
    g = g_ref[...]
    outs = []
    for h in range(H_DIFF):
        hs = slice(h * DV, (h + 1) * DV)
        acc = pcb * vc[:, hs]
        for j in range(n_pages):
            vh = vp[j][pl.ds(h, PAGE_SIZE, stride=H_DIFF), :].astype(BF16)
            acc = acc + jnp.dot(pb[:, j * PAGE_SIZE:(j + 1) * PAGE_SIZE], vh, preferred_element_type=F32)
        o = jnp.sum(jnp.where(rowc // 2 == h, acc * sign, 0.0), axis=0, keepdims=True)
        outs.append(_subln_gate(o, subg_ref[...], g[:, hs], lam_init))
    o_ref[...] = jnp.concatenate(outs, axis=-1).astype(o_ref.dtype)


def _diff_decode(p3, layer, cache_k, cache_v, page_table, lam_params, subg, lam_init):
    nb, n_pages = page_table.shape
    vec = pl.BlockSpec((1, DQK), lambda b, pt: (0, 0))

    def row(c):
        return pl.BlockSpec((None, 1, BRANCH), lambda b, pt: (b, 0, c))

    def page(j, shape):
        return pl.BlockSpec((None, None) + shape, lambda b, pt: (layer, pt[b * n_pages + j], 0, 0))

    kpages = [page(j, (BRANCH, PAGE_SIZE)) for j in range(n_pages)]
    vpages = [page(j, (PAGE_SIZE * H_DIFF, DV)) for j in range(n_pages)]
    return pl.pallas_call(
        functools.partial(_diff_decode_kernel, n_pages=n_pages, lam_init=lam_init),
        grid_spec=pltpu.PrefetchScalarGridSpec(
            num_scalar_prefetch=1,
            grid=(nb,),
            in_specs=[vec, vec, vec, vec, pl.BlockSpec((1, DV), lambda b, pt: (0, 0)),
                      row(C_DQ), row(C_DK), row(C_DV), row(C_DG)] + kpages + vpages,
            out_specs=pl.BlockSpec((None, 1, BRANCH), lambda b, pt: (b, 0, 0))),
        out_shape=jax.ShapeDtypeStruct((nb, 1, BRANCH), BF16),
        compiler_params=_cparams(("parallel",)),
        name="diff_attn_decode",
    )(page_table.reshape(-1), *lam_params, subg.reshape(1, DV), p3, p3, p3, p3,
      *([cache_k] * n_pages), *([cache_v] * n_pages))


def _lane_sum3(x, ones):
    x1 = x.astype(BF16)
    rem = x - x1.astype(F32)
    x2 = rem.astype(BF16)
    x3 = (rem - x2.astype(F32)).astype(BF16)
    d = lambda z: jnp.dot(z, ones, preferred_element_type=F32)
    return d(x1) + (d(x2) + d(x3))


def _seg_sum(x, bd_ref):
    return _lane_sum3(x, bd_ref[...])


def _rwkv_pre_kernel(*refs, shift, has_vres, tm):
    it = iter(refs)
    cur = [next(it) for _ in range(4)]
    prev = None if shift else [next(it) for _ in range(4)]
    mu = [next(it) for _ in range(4)]
    w0, wup, a0, aup, kk_ref, ka_ref, rk_ref, bd_ref = [next(it) for _ in range(8)]
    if has_vres:
        vf_ref, vr0, vra, vrb = [next(it) for _ in range(4)]
    outs = [next(it) for _ in range(7 if shift else 8)]
    carry = [next(it) for _ in range(4)] if shift else None

    xs = []
    for n in range(4):
        c = cur[n][...]
        if shift:
            @pl.when(pl.program_id(1) == 0)
            def _():
                carry[n][...] = jnp.zeros(carry[n].shape, F32)
            rolled = pltpu.roll(c, 1, 0)
            rows = lax.broadcasted_iota(jnp.int32, c.shape, 0)
            pv = jnp.where(rows == 0, carry[n][...], rolled)
            carry[n][...] = c[tm - 1:tm, :]
        else:
            pv = prev[n][...]
        xs.append(c + (pv - c) * mu[n][...])
    xr, xk, xv, xl = xs

    hw = jnp.dot(jnp.tanh(xl).astype(BF16), wup[...], preferred_element_type=F32)
    w = -DECAY_SCALE * jax.nn.sigmoid(w0[...] + hw)
    if not shift:
        w = jnp.exp(w)
    a = jax.nn.sigmoid(a0[...] + jnp.dot(xl.astype(BF16), aup[...], preferred_element_type=F32))
    if has_vres:
        t1 = jnp.dot(xv.astype(BF16), vra[...], preferred_element_type=F32)
        t2 = jnp.dot(t1.astype(BF16), vrb[...], preferred_element_type=F32)
        xv = xv + (vf_ref[...] - xv) * jax.nn.sigmoid(vr0[...] + t2)
    kk = xk * kk_ref[...]
    kk = kk / jnp.maximum(jnp.sqrt(_seg_sum(kk * kk, bd_ref)), 1e-12)
    k2 = xk * (1.0 + (a - 1.0) * ka_ref[...])
    bonus = _seg_sum(xr * k2 * rk_ref[...], bd_ref) * xv
    vals = (xr, w, k2, xv, -kk, kk * a)
    if shift:
        for o, val in zip(outs, vals + (bonus,)):
            o[...] = val
    else:
        for o, val in zip(outs, vals):
            o[...] = val.T
        outs[6][...] = bonus
        outs[7][...] = xv


def _rwkv_pre(p, prev, lw, vfirst, nb, t, tm):
    shift = prev is None
    has_vres = vfirst is not None
    nt = t // tm
    if shift:
        grid = (nb, nt)
        rmap = lambda c: (lambda b, i: (b * nt + i, c))
        cmap = lambda b, i: (0, 0)
        sem = ("parallel", "arbitrary")
    else:
        grid = (nt,)
        rmap = lambda c: (lambda i: (i, c))
        cmap = lambda i: (0, 0)
        sem = ("parallel",)
    wide = lambda c: pl.BlockSpec((tm, BRANCH), rmap(c))
    vec = lambda n: pl.BlockSpec((1, n), cmap)
    args, specs = [], []
    args += [p, p, p, p]
    specs += [wide(C_RR), wide(C_RK), wide(C_RV), pl.BlockSpec((tm, N_LORA), rmap(C_LORA128))]
    if not shift:
        args += [prev, prev, prev, prev]
        specs += [wide(0), wide(1), wide(2), pl.BlockSpec((tm, N_LORA), rmap(3 * BRANCH // N_LORA))]
    args += lw["mu"]
    specs += [vec(BRANCH)] * 3 + [vec(N_LORA)]
    args += [lw["w0"], lw["w_up"], lw["a0"], lw["a_up"], lw["k_k"], lw["k_a"], lw["r_k"], lw["bd"]]
    specs += [vec(BRANCH), pl.BlockSpec((N_LORA, BRANCH), cmap), vec(BRANCH), pl.BlockSpec((N_LORA, BRANCH), cmap),
              vec(BRANCH), vec(BRANCH), vec(BRANCH), pl.BlockSpec((BRANCH, BRANCH), cmap)]
    if has_vres:
        args += [vfirst, lw["vres_0"], lw["vres_a"], lw["vres_b"]]
        specs += [pl.BlockSpec((tm, BRANCH), rmap(0)), vec(BRANCH),
                  pl.BlockSpec(lw["vres_a"].shape, cmap), pl.BlockSpec(lw["vres_b"].shape, cmap)]
    out_spec = pl.BlockSpec((tm, BRANCH), rmap(0))
    row_shape = jax.ShapeDtypeStruct((nb * t, BRANCH), F32)
    if shift:
        out_specs, out_shape = [out_spec] * 7, [row_shape] * 7
    else:
        col_spec = pl.BlockSpec((BRANCH, tm), lambda i: (0, i))
        out_specs = [col_spec] * 6 + [out_spec] * 2
        out_shape = [jax.ShapeDtypeStruct((BRANCH, nb * t), F32)] * 6 + [row_shape] * 2
    scratch = [pltpu.VMEM((1, BRANCH), F32)] * 3 + [pltpu.VMEM((1, N_LORA), F32)] if shift else []
    return pl.pallas_call(
        functools.partial(_rwkv_pre_kernel, shift=shift, has_vres=has_vres, tm=tm),
        grid=grid, in_specs=specs, out_specs=out_specs,
        out_shape=out_shape,
        scratch_shapes=scratch,
        compiler_params=_cparams(sem),
        name="rwkv_pre",
    )(*args)


RWKV_CHUNK = 64

_NN = ((1,), (0,))
_NT = ((1,), (1,))
_TN = ((0,), (0,))


def _split(x):
    hi = x.astype(BF16)
    return hi, (x - hi.astype(F32)).astype(BF16)


def _mm3(a, b, dims):
    dg = lambda x, y: lax.dot_general(x, y, (dims, ((), ())), preferred_element_type=F32)
    return dg(a[0], b[0]) + (dg(a[0], b[1]) + dg(a[1], b[0]))


def _rwkv_scan_kernel(r_ref, lw_ref, k_ref, v_ref, a_ref, b_ref, y_ref, s_ref):
    C = RWKV_CHUNK
    W2 = 2 * R_HEAD

    @pl.when(pl.program_id(1) == 0)
    def _():
        s_ref[...] = jnp.zeros(s_ref.shape, F32)

    lw = lw_ref[...]
    tri = (lax.broadcasted_iota(jnp.int32, (C, C), 0) >= lax.broadcasted_iota(jnp.int32, (C, C), 1)).astype(BF16)
    l1 = lw.astype(BF16)
    rem = lw - l1.astype(F32)
    l2 = rem.astype(BF16)
    l3 = (rem - l2.astype(F32)).astype(BF16)
    cs = (jnp.dot(tri, l1, preferred_element_type=F32) + jnp.dot(tri, l2, preferred_element_type=F32)
          + jnp.dot(tri, l3, preferred_element_type=F32))
    p_in = jnp.exp(cs)
    p_inv = jnp.exp(-cs)
    rt = r_ref[...] * p_in
    at = a_ref[...] * jnp.exp(cs - lw)
    bt = b_ref[...] * p_inv
    kt = k_ref[...] * p_inv
    v = v_ref[...]
    p_last = p_in[C - 1:C, :]

    heads = range(H_RWKV)
    pairs = range(H_RWKV // 2)
    psl = [slice(p * W2, (p + 1) * W2) for p in pairs]
    lane2 = lax.broadcasted_iota(jnp.int32, (2 * C, W2), 1)
    keep = [lane2 < R_HEAD, lane2 >= R_HEAD]
    zb = jnp.zeros((2 * C, W2), BF16)
    gr = lax.broadcasted_iota(jnp.int32, (2 * C, 2 * C), 0)
    gc = lax.broadcasted_iota(jnp.int32, (2 * C, 2 * C), 1) % C
    gmask = jnp.where(gr < C, gr - 1, gr - C) >= gc
    second = lax.broadcasted_iota(jnp.int32, (C, W2), 1) >= R_HEAD

    ar_s = [_split(jnp.concatenate([at[:, s], rt[:, s]], axis=0)) for s in psl]
    bk_s = [_split(jnp.concatenate([bt[:, s], kt[:, s]], axis=0)) for s in psl]
    s0 = [s_ref[p] for p in pairs]
    s0_s = [_split(z) for z in s0]
    mask2 = lambda z, h: tuple(jnp.where(keep[h % 2], x, zb) for x in z[h // 2])
    arm = [mask2(ar_s, h) for h in heads]
    vh = [v[:, h * R_HEAD:(h + 1) * R_HEAD] for h in heads]
    zv_s = [_split(jnp.concatenate([jnp.zeros((C, R_HEAD), F32), z], axis=0)) for z in vh]
    g = [jnp.where(gmask, _mm3(arm[h], bk_s[h // 2], _NT), 0.0) for h in heads]
    x = [_mm3(arm[h], s0_s[h // 2], _NT) for h in heads]
    rhs = [x[h][:C] + _mm3(_split(g[h][:C]), zv_s[h], _NN) for h in heads]
    w = [jnp.concatenate([g[h][:C, :C], rhs[h]], axis=1) for h in heads]
    for _ in range(int(math.log2(C))):
        w_s = [_split(z) for z in w]
        w = [_mm3((w_s[h][0][:, :R_HEAD], w_s[h][1][:, :R_HEAD]), w_s[h], _NN) + jnp.where(second, w[h], 0.0)
             for h in heads]
    uv_s = [_split(jnp.concatenate([w[h][:, R_HEAD:], vh[h]], axis=0)) for h in heads]
    for h in heads:
        y_ref[:, h * R_HEAD:(h + 1) * R_HEAD] = x[h][C:] + _mm3(_split(g[h][C:]), uv_s[h], _NN)
    ds = [_mm3(uv_s[h], mask2(bk_s, h), _TN) for h in heads]
    for p in pairs:
        s_ref[p] = (s0[p] + ds[2 * p] + ds[2 * p + 1]) * p_last[:, psl[p]]


def _rwkv_scan(seqs, nb, t):
    tc = RWKV_CHUNK
    hp = H_RWKV // 2
    blk = pl.BlockSpec((None, tc, BRANCH), lambda b, c: (b, c, 0))
    y, s = pl.pallas_call(
        _rwkv_scan_kernel,
        grid=(nb, t // tc),
        in_specs=[blk] * 6,
        out_specs=[blk, pl.BlockSpec((hp, R_HEAD, 2 * R_HEAD), lambda b, c: (b, 0, 0))],
        out_shape=[jax.ShapeDtypeStruct((nb, t, BRANCH), F32),
                   jax.ShapeDtypeStruct((nb * hp, R_HEAD, 2 * R_HEAD), F32)],
        compiler_params=_cparams(("parallel", "arbitrary")),
        name="rwkv_scan",
    )(*seqs)
    s = s.reshape(nb, hp, R_HEAD, 2, R_HEAD).transpose(0, 1, 3, 2, 4).reshape(nb, H_RWKV, R_HEAD, R_HEAD)
    return y, s


def _rwkv_step_kernel(r_ref, w_ref, k_ref, a_ref, b_ref, v_ref, s_ref, y_ref, so_ref):
    r, w, k, a, b = [ref[...] for ref in (r_ref, w_ref, k_ref, a_ref, b_ref)]
    vals = range(R_HEAD)
    st = [s_ref[v] for v in vals]
    sa = [jnp.sum(z * a, axis=0, keepdims=True) for z in st]
    st = [z * w + x * b + v_ref[v:v + 1, :] * k for z, x, v in zip(st, sa, vals)]
    for v in vals:
        so_ref[v] = st[v]
    y_ref[...] = jnp.concatenate([jnp.sum(z * r, axis=0, keepdims=True) for z in st], axis=0)


def _rwkv_step(seqs, state, layer):
    n = seqs[0].shape[1]
    vec = pl.BlockSpec((R_HEAD, n), lambda h: (h, 0))
    stb = pl.BlockSpec((R_HEAD, R_HEAD, n), lambda h: (h, 0, 0))
    return pl.pallas_call(
        _rwkv_step_kernel,
        grid=(H_RWKV,),
        in_specs=[vec] * 6 + [pl.BlockSpec((None, R_HEAD, R_HEAD, n), lambda h: (layer, h, 0, 0))],
        out_specs=[vec, stb],
        out_shape=[jax.ShapeDtypeStruct((BRANCH, n), F32),
                   jax.ShapeDtypeStruct((BRANCH, R_HEAD, n), F32)],
        compiler_params=_cparams(("parallel",)),
        name="rwkv_step",
    )(*seqs, state)


def _rwkv_post_kernel(y_ref, bonus_ref, g_ref, lg_ref, lb_ref, bd_ref, o_ref):
    y = y_ref[...]
    mu = _seg_sum(y, bd_ref) * (1.0 / R_HEAD)
    d = y - mu
    var = _seg_sum(d * d, bd_ref) * (1.0 / R_HEAD)
    o = d * lax.rsqrt(var + LNX_EPS) * lg_ref[...] + lb_ref[...] + bonus_ref[...]
    o_ref[...] = (o * _silu(g_ref[...])).astype(o_ref.dtype)


def _rwkv_post(y, bonus, p, lw, tm):
    m = y.shape[0]
    blk = pl.BlockSpec((tm, BRANCH), lambda i: (i, 0))
    vec = pl.BlockSpec((1, BRANCH), lambda i: (0, 0))
    return pl.pallas_call(
        _rwkv_post_kernel,
        grid=(m // tm,),
        in_specs=[blk, blk, pl.BlockSpec((tm, BRANCH), lambda i: (i, C_RG)), vec, vec,
                  pl.BlockSpec((BRANCH, BRANCH), lambda i: (0, 0))],
        out_specs=blk,
        out_shape=jax.ShapeDtypeStruct((m, BRANCH), BF16),
        compiler_params=_cparams(("parallel",)),
        name="rwkv_post",
    )(y, bonus, p, lw["lnx_g"], lw["lnx_b"], lw["bd"])


def _softplus(x):
    return jnp.maximum(x, 0.0) + jnp.log1p(jnp.exp(-jnp.abs(x)))


def _neg_exp_alog(alog_ref):
    lanes = lax.broadcasted_iota(jnp.int32, alog_ref.shape, 1)
    return jnp.where(lanes < H_MAMBA, -jnp.exp(alog_ref[...]), 0.0)


def _gated_norm(ym, z, g):
    u = ym * _silu(z)
    ms = jnp.mean(u * u, axis=-1, keepdims=True)
    return u * lax.rsqrt(ms + RMS_EPS) * g


def _mamba_prompt_kernel(mx_ref, mbc_ref, mdt_ref, mz_ref, cwx_ref, cwb_ref, cbx_ref, cbb_ref, dtb_ref, alog_ref,
                         dskip_ref, gn_ref, o_ref, s_ref, extx_ref, extb_ref, ym_ref):
    L = SSD_CHUNK
    pad = 8

    @pl.when(pl.program_id(1) == 0)
    def _():
        extx_ref[0:pad, :] = jnp.zeros((pad, BRANCH), F32)
        extb_ref[0:pad, :] = jnp.zeros((pad, BRANCH), F32)
        s_ref[...] = jnp.zeros(s_ref.shape, F32)

    def conv(ext_ref, cur_ref, cw_ref, cb_ref):
        ext_ref[pad:pad + L, :] = cur_ref[...]
        acc = cb_ref[...]
        for j in range(M_CONV):
            lo = pad - (M_CONV - 1) + j
            acc = acc + ext_ref[lo:lo + L, :] * cw_ref[j:j + 1, :]
        ext_ref[0:pad, :] = ext_ref[L:L + pad, :]
        return _silu(acc)

    x = conv(extx_ref, mx_ref, cwx_ref, cbx_ref)
    bc = conv(extb_ref, mbc_ref, cwb_ref, cbb_ref)
    dt = _softplus(mdt_ref[...] + dtb_ref[...])
    adt = dt * _neg_exp_alog(alog_ref)
    rowi = lax.broadcasted_iota(jnp.int32, (L, L), 0)
    coli = lax.broadcasted_iota(jnp.int32, (L, L), 1)
    lower = rowi >= coli
    acs = jnp.dot(lower.astype(F32), adt, preferred_element_type=F32, precision=HIGHEST)
    acs_t = acs.T
    last = acs[L - 1:L, :]
    dstate = jnp.exp(last - acs)
    eacs = jnp.exp(acs)
    elast = jnp.exp(last)
    bcb = bc.astype(BF16)
    grp = []
    for g in range(M_GROUPS):
        bg = bcb[:, g * M_STATE:(g + 1) * M_STATE]
        cg = bcb[:, (M_GROUPS + g) * M_STATE:(M_GROUPS + g + 1) * M_STATE]
        grp.append((bg, cg, _dot_t(cg, bg)))
    for h in range(H_MAMBA):
        bg, cg, cb = grp[h // (H_MAMBA // M_GROUPS)]
        sl = slice(h * M_HEADDIM, (h + 1) * M_HEADDIM)
        decay = jnp.where(lower, jnp.exp(acs[:, h:h + 1] - acs_t[h:h + 1, :]), 0.0)
        xh = x[:, sl]
        xdt = xh * dt[:, h:h + 1]
        yd = jnp.dot((cb * decay).astype(BF16), xdt.astype(BF16), preferred_element_type=F32)
        st = s_ref[h]
        yo = _dot_t(cg, st.astype(BF16)) * eacs[:, h:h + 1]
        upd = lax.dot_general((xdt * dstate[:, h:h + 1]).astype(BF16), bg, (((0,), (0,)), ((), ())),
                              preferred_element_type=F32)
        s_ref[h] = st * elast[:, h:h + 1] + upd
        ym_ref[:, sl] = yd + yo + dskip_ref[:, sl] * xh
    o_ref[...] = _gated_norm(ym_ref[...], mz_ref[...], gn_ref[...]).astype(o_ref.dtype)


def _mamba_prompt(p, lw, nb, t):
    L = SSD_CHUNK
    nt = t // L
    rmap = lambda c: (lambda b, i: (b * nt + i, c))
    cmap = lambda b, i: (0, 0)
    wide = lambda c: pl.BlockSpec((L, BRANCH), rmap(c))
    vec = lambda n: pl.BlockSpec((1, n), cmap)
    cw = pl.BlockSpec((M_CONV, BRANCH), cmap)
    return pl.pallas_call(
        _mamba_prompt_kernel,
        grid=(nb, nt),
        in_specs=[wide(C_MX), wide(C_MBC), pl.BlockSpec((L, 128), rmap(C_DT128)), wide(C_MZ),
                  cw, cw, vec(BRANCH), vec(BRANCH), vec(128), vec(128), vec(BRANCH), vec(BRANCH)],
        out_specs=[pl.BlockSpec((L, BRANCH), rmap(0)),
                   pl.BlockSpec((None, H_MAMBA, M_HEADDIM, M_STATE), lambda b, i: (b, 0, 0, 0))],
        out_shape=[jax.ShapeDtypeStruct((nb * t, BRANCH), BF16),
                   jax.ShapeDtypeStruct((nb, H_MAMBA, M_HEADDIM, M_STATE), F32)],
        scratch_shapes=[pltpu.VMEM((L + 8, BRANCH), F32), pltpu.VMEM((L + 8, BRANCH), F32),
                        pltpu.VMEM((L, BRANCH), F32)],
        compiler_params=_cparams(("parallel", "arbitrary")),
        name="mamba_prompt",
    )(p, p, p, p, lw["cw_x"], lw["cw_bc"], lw["cb_x"], lw["cb_bc"], lw["dt_bias"], lw["a_log"],
      lw["d_skip"], lw["gnorm_g"])


def _mamba_dec_pre_kernel(mx_ref, mbc_ref, mdt_ref, sx_ref, sb_ref, cwx_ref, cwb_ref, cbx_ref, cbb_ref,
                          dtb_ref, alog_ref, x_ref, bc_ref, xdt_ref, da_ref):
    def conv(st_ref, cur_ref, cw_ref, cb_ref):
        acc = cb_ref[...]
        for j in range(M_CONV - 1):
            acc = acc + st_ref[j] * cw_ref[j:j + 1, :]
        acc = acc + cur_ref[...] * cw_ref[M_CONV - 1:M_CONV, :]
        return _silu(acc)

    x = conv(sx_ref, mx_ref, cwx_ref, cbx_ref)
    x_ref[...] = x
    bc_ref[...] = conv(sb_ref, mbc_ref, cwb_ref, cbb_ref)
    dt = _softplus(mdt_ref[...] + dtb_ref[...])
    da = jnp.exp(dt * _neg_exp_alog(alog_ref))
    n = x.shape[0]
    for h in range(H_MAMBA):
        sl = slice(h * M_HEADDIM, (h + 1) * M_HEADDIM)
        xdt_ref[:, sl] = x[:, sl] * dt[:, h:h + 1]
        da_ref[:, sl] = jnp.broadcast_to(da[:, h:h + 1], (n, M_HEADDIM))


def _mamba_dec_pre(p, conv_x, conv_bc, lw):
    n = p.shape[0]
    cmap = lambda i: (0, 0)
    wide = lambda c: pl.BlockSpec((n, BRANCH), lambda i: (0, c))
    vec = lambda m: pl.BlockSpec((1, m), cmap)
    cw = pl.BlockSpec((M_CONV, BRANCH), cmap)
    st = pl.BlockSpec((M_CONV - 1, n, BRANCH), lambda i: (0, 0, 0))
    out = pl.BlockSpec((n, BRANCH), cmap)
    return pl.pallas_call(
        _mamba_dec_pre_kernel,
        grid=(1,),
        in_specs=[wide(C_MX), wide(C_MBC), pl.BlockSpec((n, 128), lambda i: (0, C_DT128)), st, st,
                  cw, cw, vec(BRANCH), vec(BRANCH), vec(128), vec(128)],
        out_specs=[out] * 4,
        out_shape=[jax.ShapeDtypeStruct((n, BRANCH), F32)] * 4,
        compiler_params=_cparams(("arbitrary",)),
        name="mamba_dec_pre",
    )(p, p, p, conv_x, conv_bc, lw["cw_x"], lw["cw_bc"], lw["cb_x"], lw["cb_bc"], lw["dt_bias"], lw["a_log"])


def _mamba_step_kernel(bc_ref, xdtt_ref, dat_ref, s_ref, yt_ref, so_ref, *, bb):
    units = [(j, h // (H_MAMBA // M_GROUPS), slice(h * M_HEADDIM, (h + 1) * M_HEADDIM))
             for j in range(bb) for h in range(H_MAMBA)]
    st = []
    for j, g, sl in units:
        bg = bc_ref[j:j + 1, g * M_STATE:(g + 1) * M_STATE]
        z = s_ref[j, sl, :] * dat_ref[sl, j:j + 1] + xdtt_ref[sl, j:j + 1] * bg
        so_ref[j, sl, :] = z
        st.append(z)
    ys = [jnp.sum(z * bc_ref[j:j + 1, (M_GROUPS + g) * M_STATE:(M_GROUPS + g + 1) * M_STATE], axis=-1, keepdims=True)
          for z, (j, g, sl) in zip(st, units)]
    for y, (j, g, sl) in zip(ys, units):
        yt_ref[sl, j:j + 1] = y


def _mamba_step(bc, xdtt, dat, state, layer, bb=8):
    n = bc.shape[0]
    rowb = pl.BlockSpec((bb, BRANCH), lambda i: (i, 0))
    colb = pl.BlockSpec((None, BRANCH, bb), lambda i: (i, 0, 0))
    stb = pl.BlockSpec((bb, BRANCH, M_STATE), lambda i: (i, 0, 0))
    return pl.pallas_call(
        functools.partial(_mamba_step_kernel, bb=bb),
        grid=(n // bb,),
        in_specs=[rowb, colb, colb, pl.BlockSpec((None, bb, BRANCH, M_STATE), lambda i: (layer, i, 0, 0))],
        out_specs=[colb, stb],
        out_shape=[jax.ShapeDtypeStruct((n // bb, BRANCH, bb), F32),
                   jax.ShapeDtypeStruct((n, BRANCH, M_STATE), F32)],
        compiler_params=_cparams(("parallel",)),
        name="mamba_step",
    )(bc, xdtt, dat, state)


def _mamba_dec_post_kernel(y_ref, x_ref, z_ref, dskip_ref, gn_ref, o_ref):
    ym = y_ref[...] + dskip_ref[...] * x_ref[...]
    o_ref[...] = _gated_norm(ym, z_ref[...], gn_ref[...]).astype(o_ref.dtype)


def _mamba_dec_post(y, x, p, lw):
    n = y.shape[0]
    blk = pl.BlockSpec((n, BRANCH), lambda i: (0, 0))
    vec = pl.BlockSpec((1, BRANCH), lambda i: (0, 0))
    return pl.pallas_call(
        _mamba_dec_post_kernel,
        grid=(1,),
        in_specs=[blk, blk, pl.BlockSpec((n, BRANCH), lambda i: (0, C_MZ)), vec, vec],
        out_specs=blk,
        out_shape=jax.ShapeDtypeStruct((n, BRANCH), BF16),
        compiler_params=_cparams(("arbitrary",)),
        name="mamba_dec_post",
    )(y, x, p, lw["d_skip"], lw["gnorm_g"])


def _xattn_prompt_kernel(q_ref, g_ref, mk_ref, mv_ref, o_ref):
    q = q_ref[...].astype(BF16)
    mk = mk_ref[...].astype(BF16)
    mv = mv_ref[...].astype(BF16)
    for h in range(H_X):
        sl = slice(h * D_X, (h + 1) * D_X)
        s = _dot_t(q[:, sl], mk[:, sl]) * (D_X ** -0.5)
        e = jnp.exp(s - jnp.max(s, axis=-1, keepdims=True))
        o = jnp.dot(e.astype(BF16), mv[:, sl], preferred_element_type=F32) / jnp.sum(e, axis=-1, keepdims=True)
        o_ref[:, sl] = (o * _silu(g_ref[:, sl])).astype(o_ref.dtype)


def _xattn_prompt(p, mkv, nb, t):
    tq = min(512, t)
    nq = t // tq
    return pl.pallas_call(
        _xattn_prompt_kernel,
        grid=(nb, nq),
        in_specs=[pl.BlockSpec((tq, BRANCH), lambda b, i: (b * nq + i, C_XQ)),
                  pl.BlockSpec((tq, BRANCH), lambda b, i: (b * nq + i, C_XG)),
                  pl.BlockSpec((N_MEM, BRANCH), lambda b, i: (b, 0)),
                  pl.BlockSpec((N_MEM, BRANCH), lambda b, i: (b, 1))],
        out_specs=pl.BlockSpec((tq, BRANCH), lambda b, i: (b * nq + i, 0)),
        out_shape=jax.ShapeDtypeStruct((nb * t, BRANCH), BF16),
        compiler_params=_cparams(("parallel", "parallel")),
        name="xattn_prompt",
    )(p, p, mkv, mkv)


def _xattn_decode_kernel(q_ref, g_ref, mk_ref, mv_ref, o_ref, *, bb):
    nrow = 8
    units = [(j, h) for j in range(bb) for h in range(H_X)]
    q = [q_ref[j].astype(BF16) for j in range(bb)]
    s = [_dot_t(jnp.broadcast_to(q[j][:, h * D_X:(h + 1) * D_X], (nrow, D_X)),
                mk_ref[j, pl.ds(h, N_MEM, stride=H_X), :].astype(BF16)) * (D_X ** -0.5) for j, h in units]
    e = [jnp.exp(z - jnp.max(z, axis=-1, keepdims=True)) for z in s]
    r = [jnp.dot(z.astype(BF16), mv_ref[j, pl.ds(h, N_MEM, stride=H_X), :].astype(BF16),
                 preferred_element_type=F32) for z, (j, h) in zip(e, units)]
    outs = [(x / jnp.sum(z, axis=-1, keepdims=True))[0:1, :] for x, z in zip(r, e)]
    for j in range(bb):
        o = jnp.concatenate(outs[j * H_X:(j + 1) * H_X], axis=-1)
        o_ref[j] = (o * _silu(g_ref[j])).astype(o_ref.dtype)


def _xattn_decode(p3, layer, mem_k, mem_v, bb=4):
    n = p3.shape[0]
    mem = pl.BlockSpec((None, bb, N_MEM * H_X, D_X), lambda b: (layer, b, 0, 0))
    return pl.pallas_call(
        functools.partial(_xattn_decode_kernel, bb=bb),
        grid=(n // bb,),
        in_specs=[pl.BlockSpec((bb, 1, BRANCH), lambda b: (b, 0, C_XQ)),
                  pl.BlockSpec((bb, 1, BRANCH), lambda b: (b, 0, C_XG)), mem, mem],
        out_specs=pl.BlockSpec((bb, 1, BRANCH), lambda b: (b, 0, 0)),
        out_shape=jax.ShapeDtypeStruct((n, 1, BRANCH), BF16),
        compiler_params=_cparams(("parallel",)),
        name="xattn_decode",
    )(p3, p3, mem_k, mem_v)


N_IN = 8 * BRANCH + N_SHIFT + 2 * BRANCH + H_MAMBA


def _reorder_w_in_kernel(w_ref, o_ref):
    o_rs, o_rg, o_mz, o_mxbc, o_dt, o_xq = 2048, 3712, 4224, 4736, 5760, 5768
    o_lora = o_rs + 3 * BRANCH

    def put(dst, val):
        o_ref[:, dst:dst + val.shape[1]] = val.astype(o_ref.dtype)

    put(0, w_ref[:, :o_lora])
    put(C_RG * BRANCH, w_ref[:, o_rg:o_mxbc])
    tail = w_ref[:, o_dt:]
    put(C_XQ * BRANCH, tail[:, o_xq - o_dt:])
    put(C_MX * BRANCH, w_ref[:, o_mxbc:o_dt])
    put(C_LORA128 * 128, w_ref[:, o_lora:o_rg])
    lanes = lax.broadcasted_iota(jnp.int32, (w_ref.shape[0], 128), 1)
    put(C_DT128 * 128, jnp.where(lanes < H_MAMBA, tail[:, :128], 0.0))


def _reorder_w_in(w_in, tm=256):
    depth = w_in.shape[0]
    return pl.pallas_call(
        _reorder_w_in_kernel,
        grid=(depth, D_MODEL // tm),
        in_specs=[pl.BlockSpec((None, tm, N_IN), lambda l, i: (l, i, 0))],
        out_specs=pl.BlockSpec((None, tm, N_PROJ), lambda l, i: (l, i, 0)),
        out_shape=jax.ShapeDtypeStruct((depth, D_MODEL, N_PROJ), BF16),
        compiler_params=_cparams(("parallel", "parallel")),
        name="reorder_w_in",
    )(w_in)


def _layer_weights(l, prm):
    row = lambda v: v.reshape(1, -1).astype(F32)
    mu = prm["shift_mu"][l].astype(F32)
    seg = jnp.arange(BRANCH) // R_HEAD
    zl = jnp.zeros((N_LORA // 2, BRANCH), F32)
    cw = prm["conv_w"][l].astype(F32)
    cb = prm["conv_b"][l].astype(F32)
    pad8 = lambda v: jnp.pad(v.astype(F32), (0, 128 - H_MAMBA)).reshape(1, 128)
    lw = {
        "w_in": prm["w_in_reordered"], "norm_g": prm["norm_g"][l], "w_out": prm["w_out"][l].astype(BF16),
        "lam": [row(prm[n][l]) for n in ("lam_q1", "lam_k1", "lam_q2", "lam_k2")],
        "subln_g": prm["subln_g"][l].astype(F32),
        "mu": [row(mu[:BRANCH]), row(mu[BRANCH:2 * BRANCH]), row(mu[2 * BRANCH:3 * BRANCH]), row(mu[3 * BRANCH:])],
        "w0": row(prm["w0"][l]), "a0": row(prm["a0"][l]),
        "w_up": jnp.concatenate([prm["w_up"][l].astype(F32), zl], axis=0).astype(BF16),
        "a_up": jnp.concatenate([zl, prm["a_up"][l].astype(F32)], axis=0).astype(BF16),
        "k_k": row(prm["k_k"][l]), "k_a": row(prm["k_a"][l]), "r_k": row(prm["r_k"][l]),
        "lnx_g": row(prm["lnx_g"][l]), "lnx_b": row(prm["lnx_b"][l]),
        "bd": (seg[:, None] == seg[None, :]).astype(BF16),
        "cw_x": cw[:, :BRANCH], "cw_bc": cw[:, BRANCH:], "cb_x": row(cb[:BRANCH]), "cb_bc": row(cb[BRANCH:]),
        "dt_bias": pad8(prm["dt_bias"][l]), "a_log": pad8(prm["a_log"][l]),
        "d_skip": row(jnp.repeat(prm["d_skip"][l].astype(F32), M_HEADDIM)),
        "gnorm_g": row(prm["gnorm_g"][l]),
        "w_mkv": jnp.concatenate([prm["w_mk"][l], prm["w_mv"][l]], axis=1).astype(BF16),
        "mem_norm_g": prm["mem_norm_g"][l],
    }
    if l > 0:
        lw["vres_0"] = row(prm["vres_0"][l - 1])
        lw["vres_a"] = prm["vres_a"][l - 1].astype(BF16)
        lw["vres_b"] = prm["vres_b"][l - 1].astype(BF16)
    return lw


def _lam_init(l):
    return 0.8 - 0.6 * math.exp(-0.3 * l)


def _prompt_layer(x, l, lw, mem, vfirst, nb, t, fg, final, tm_proj, tm_row):
    p = _rms_matmul(x, lw["norm_g"], lw["w_in"], tm_proj, 1152, layer=l)
    mkv = _rms_matmul(mem, lw["mem_norm_g"], lw["w_mkv"], mem.shape[0], 2 * BRANCH)
    out_a = _diff_prompt(p, lw["lam"], lw["subln_g"], nb, t, _lam_init(l))
    r, w, k, v, na, b, bonus = _rwkv_pre(p, None, lw, vfirst, nb, t, tm_row)
    seqs = [z.reshape(nb, t, BRANCH) for z in (r, w, k, v, na, b)]
    y, rw_state = _rwkv_scan(seqs, nb, t)
    out_b = _rwkv_post(y.reshape(nb * t, BRANCH), bonus, p, lw, tm_row)
    out_m, ssm_state = _mamba_prompt(p, lw, nb, t)
    out_x = _xattn_prompt(p, mkv, nb, t)
    x = _outproj((out_a, out_b, out_m, out_x), lw["w_out"], x, fg, tm_row, final)
    return x, p, mkv, rw_state, ssm_state, (v if vfirst is None else vfirst)


def _decode_layer(x, l, lw, caches, vfirst, fg, final):
    cache_k, cache_v, mem_k, mem_v, page_table, rw_prev, rw_state, conv_state, ssm_state = caches
    n = x.shape[0]
    bb = 8
    p = _rms_matmul(x, lw["norm_g"], lw["w_in"], n, 1152, layer=l)
    p3 = p.reshape(n, 1, N_PROJ)
    out_a = _diff_decode(p3, l, cache_k, cache_v, page_table, lw["lam"], lw["subln_g"], _lam_init(l)).reshape(n, BRANCH)

    rt, wt, kt, vt, nat, bt, bonus, v = _rwkv_pre(p, rw_prev, lw, vfirst, 1, n, n)
    to_cols = lambda z: z.reshape(n // bb, bb, BRANCH).transpose(0, 2, 1)
    from_cols = lambda z: z.transpose(0, 2, 1).reshape(n, BRANCH)
    yt, rw_new = _rwkv_step((rt, wt, kt, nat, bt, vt), rw_state, l)
    out_b = _rwkv_post(yt.T, bonus, p, lw, n)

    cs = conv_state.transpose(1, 0, 2)
    xm, bc, xdt, da = _mamba_dec_pre(p, cs[:, :, :BRANCH], cs[:, :, BRANCH:], lw)
    ymt, ssm_new = _mamba_step(bc, to_cols(xdt), to_cols(da), ssm_state.reshape(-1, n, BRANCH, M_STATE), l, bb)
    out_m = _mamba_dec_post(from_cols(ymt), xm, p, lw)

    out_x = _xattn_decode(p3, l, mem_k, mem_v).reshape(n, BRANCH)
    x = _outproj((out_a, out_b, out_m, out_x), lw["w_out"], x, fg, n, final)
    return x, p, rw_new, ssm_new, (v if vfirst is None else vfirst)


def _cols(p, c, width=BRANCH):
    return p[..., c * width:(c + 1) * width]


def kernel(x_prompt, x_sample, cache_diff_k, cache_diff_v, cache_mem_k, cache_mem_v, state_rwkv_shift, state_rwkv, state_conv, state_ssm, page_table, mem_prompt, norm_g, w_in, w_out, lam_q1, lam_k1, lam_q2, lam_k2, subln_g, shift_mu, w0, w_up, a0, a_up, k_k, k_a, r_k, lnx_g, lnx_b, vres_0, vres_a, vres_b, conv_w, conv_b, dt_bias, a_log, d_skip, gnorm_g, mem_norm_g, w_mk, w_mv, final_g):
    prm = dict(norm_g=norm_g, w_in=w_in, w_out=w_out, lam_q1=lam_q1, lam_k1=lam_k1, lam_q2=lam_q2, lam_k2=lam_k2,
               subln_g=subln_g, shift_mu=shift_mu, w0=w0, w_up=w_up, a0=a0, a_up=a_up, k_k=k_k, k_a=k_a, r_k=r_k,
               lnx_g=lnx_g, lnx_b=lnx_b, vres_0=vres_0, vres_a=vres_a, vres_b=vres_b, conv_w=conv_w, conv_b=conv_b,
               dt_bias=dt_bias, a_log=a_log, d_skip=d_skip, gnorm_g=gnorm_g, mem_norm_g=mem_norm_g,
               w_mk=w_mk, w_mv=w_mv, w_in_reordered=_reorder_w_in(w_in))
    depth = w_in.shape[0]
    nb, t, _ = x_prompt.shape
    n = x_sample.shape[0]
    n_pool = cache_diff_k.shape[1]
    ck = jnp.transpose(cache_diff_k, (0, 1, 3, 4, 5, 2)).reshape(depth, n_pool, BRANCH, PAGE_SIZE)
    cv = cache_diff_v.reshape(depth, n_pool, PAGE_SIZE * H_DIFF, DV)
    cmk = cache_mem_k.reshape(depth, n, N_MEM * H_X, D_X)
    cmv = cache_mem_v.reshape(depth, n, N_MEM * H_X, D_X)
    rws = jnp.transpose(state_rwkv, (0, 2, 3, 4, 1)).reshape(depth, BRANCH, R_HEAD, n)
    mem = mem_prompt.reshape(nb * N_MEM, D_MODEL)

    xp = x_prompt.reshape(nb * t, D_MODEL)
    xs = x_sample.reshape(n, D_MODEL)
    vf_p = vf_s = None
    outs = [[] for _ in range(14)]
    for l in range(depth):
        lw = _layer_weights(l, prm)
        final = l == depth - 1
        xp, p, mkv, rw_p, ssm_p, vf_p = _prompt_layer(xp, l, lw, mem, vf_p, nb, t, final_g, final,
                                                      min(1024, nb * t), min(512, t))
        p3 = p.reshape(nb, t, N_PROJ)
        last = p3[:, t - 1]
        outs[0].append(_cols(p3, C_DK).reshape(nb, t, H_DIFF, 2, DQK))
        outs[1].append(_cols(p3, C_DV).reshape(nb, t, H_DIFF, DV))
        outs[2].append(mkv[:, :BRANCH].reshape(nb, N_MEM, H_X, D_X))
        outs[3].append(mkv[:, BRANCH:].reshape(nb, N_MEM, H_X, D_X))
        outs[4].append(jnp.concatenate([_cols(last, C_RR), _cols(last, C_RK), _cols(last, C_RV),
                                        _cols(last, C_LORA128, 128)], axis=-1))
        outs[5].append(rw_p.reshape(nb, H_RWKV, R_HEAD, R_HEAD))
        tail = p3[:, t - (M_CONV - 1):]
        outs[6].append(jnp.concatenate([_cols(tail, C_MX), _cols(tail, C_MBC)], axis=-1))
        outs[7].append(ssm_p)

        caches = (ck, cv, cmk, cmv, page_table, state_rwkv_shift[l], rws, state_conv[l], state_ssm)
        xs, ps, rw_s, ssm_s, vf_s = _decode_layer(xs, l, lw, caches, vf_s, final_g, final)
        outs[8].append(_cols(ps, C_DK).reshape(n, 1, H_DIFF, 2, DQK))
        outs[9].append(_cols(ps, C_DV).reshape(n, 1, H_DIFF, DV))
        outs[10].append(jnp.concatenate([_cols(ps, C_RR), _cols(ps, C_RK), _cols(ps, C_RV),
                                         _cols(ps, C_LORA128, 128)], axis=-1))
        outs[11].append(rw_s.reshape(H_RWKV, R_HEAD, R_HEAD, n).transpose(3, 0, 1, 2))
        outs[12].append(jnp.concatenate(
            [state_conv[l][:, 1:], jnp.concatenate([_cols(ps, C_MX), _cols(ps, C_MBC)], axis=-1)[:, None]], axis=1))
        outs[13].append(ssm_s.reshape(n, H_MAMBA, M_HEADDIM, M_STATE))

    return (xp.reshape(nb, t, D_MODEL), xs.reshape(n, 1, D_MODEL)) + tuple(jnp.stack(o) for o in outs)
```

```python
import functools
import math

import jax
import jax.numpy as jnp
from jax import lax
from jax.experimental import pallas as pl
from jax.experimental.pallas import tpu as pltpu

F32 = jnp.float32
BF16 = jnp.bfloat16
HIGHEST = lax.Precision.HIGHEST

D_MODEL = 2048
BRANCH = 512
H_DIFF = 4
DQK = 64
DV = 128
R_HEAD = 64
H_RWKV = 8
N_LORA = 128
N_SHIFT = 3 * BRANCH + N_LORA
DECAY_SCALE = 0.6065306597
LNX_EPS = 64e-5
M_HEADDIM = 64
H_MAMBA = 8
M_GROUPS = 2
M_STATE = 128
M_CONV = 4
SSD_CHUNK = 128
H_X = 4
D_X = 128
N_MEM = 256
RMS_EPS = 1e-5
PAGE_SIZE = 128

C_DQ, C_DK, C_DV, C_DG, C_RR, C_RK, C_RV, C_RG, C_MZ, C_XQ, C_XG, C_MX, C_MBC = range(13)
C_LORA128 = 13 * BRANCH // 128
C_DT128 = C_LORA128 + 1
N_PROJ = 13 * BRANCH + 2 * 128

VMEM_LIMIT = 56 * 1024 * 1024


def _cparams(sem):
    return pltpu.CompilerParams(dimension_semantics=sem, vmem_limit_bytes=VMEM_LIMIT)


def _silu(x):
    return x * jax.nn.sigmoid(x)


def _dot_t(a, b):
    return lax.dot_general(a, b, (((1,), (1,)), ((), ())), preferred_element_type=F32)


def _rms_matmul_kernel(x_ref, g_ref, w_ref, o_ref, h_ref, *, w_rows_are_outputs):
    @pl.when(pl.program_id(1) == 0)
    def _():
        x = x_ref[...]
        ms = jnp.mean(x * x, axis=-1, keepdims=True)
        h_ref[...] = (x * lax.rsqrt(ms + RMS_EPS) * g_ref[...]).astype(BF16)

    if w_rows_are_outputs:
        o_ref[...] = _dot_t(h_ref[...], w_ref[...])
    else:
        o_ref[...] = jnp.dot(h_ref[...], w_ref[...], preferred_element_type=F32)


def _rms_matmul(x, g, w, tm, tn, layer=None):
    m, k = x.shape
    if layer is None:
        n = w.shape[1]
        w_spec = pl.BlockSpec((k, tn), lambda i, j: (0, j))
    else:
        n = w.shape[1]
        w_spec = pl.BlockSpec((None, tn, k), lambda i, j: (layer, j, 0))
    return pl.pallas_call(
        functools.partial(_rms_matmul_kernel, w_rows_are_outputs=layer is not None),
        grid=(m // tm, n // tn),
        in_specs=[pl.BlockSpec((tm, k), lambda i, j: (i, 0)),
                  pl.BlockSpec((1, k), lambda i, j: (0, 0)),
                  w_spec],
        out_specs=pl.BlockSpec((tm, tn), lambda i, j: (i, j)),
        out_shape=jax.ShapeDtypeStruct((m, n), F32),
        scratch_shapes=[pltpu.VMEM((tm, k), BF16)],
        compiler_params=_cparams(("parallel", "arbitrary")),
        name="rms_matmul",
    )(x, g.reshape(1, k), w)


def _outproj_kernel(a_ref, b_ref, m_ref, c_ref, w_ref, x_ref, fg_ref, o_ref, *, final):
    acc = x_ref[...]
    for g, r in enumerate((a_ref, b_ref, m_ref, c_ref)):
        acc = acc + jnp.dot(r[...], w_ref[g * BRANCH:(g + 1) * BRANCH, :], preferred_element_type=F32)
    if final:
        ms = jnp.mean(acc * acc, axis=-1, keepdims=True)
        acc = acc * lax.rsqrt(ms + RMS_EPS) * fg_ref[...]
    o_ref[...] = acc


def _outproj(parts, w, x, fg, tm, final):
    m = x.shape[0]
    part_spec = pl.BlockSpec((tm, BRANCH), lambda i: (i, 0))
    return pl.pallas_call(
        functools.partial(_outproj_kernel, final=final),
        grid=(m // tm,),
        in_specs=[part_spec] * 4 + [pl.BlockSpec((D_MODEL, D_MODEL), lambda i: (0, 0)),
                                    pl.BlockSpec((tm, D_MODEL), lambda i: (i, 0)),
                                    pl.BlockSpec((1, D_MODEL), lambda i: (0, 0))],
        out_specs=pl.BlockSpec((tm, D_MODEL), lambda i: (i, 0)),
        out_shape=jax.ShapeDtypeStruct((m, D_MODEL), F32),
        compiler_params=_cparams(("parallel",)),
        name="outproj",
    )(*parts, w, x, fg.reshape(1, D_MODEL))


def _alibi_slope(h):
    return 2.0 ** (-8.0 * (h + 1) / H_DIFF)


def _lambda(lq1, lk1, lq2, lk2, lam_init):
    s1 = jnp.sum(lq1[...] * lk1[...], axis=-1, keepdims=True)
    s2 = jnp.sum(lq2[...] * lk2[...], axis=-1, keepdims=True)
    return jnp.exp(s1) - jnp.exp(s2) + lam_init


def _subln_gate(o, subg, gate, lam_init):
    ms = jnp.mean(o * o, axis=-1, keepdims=True)
    o = (o * lax.rsqrt(ms + RMS_EPS) * subg) * (1.0 - lam_init)
    return o * _silu(gate)


def _diff_prompt_kernel(lq1, lk1, lq2, lk2, subg_ref, q_ref, k_ref, v_ref, g_ref, o_ref,
                        m_ref, l_ref, acc_ref, *, tq, lam_init):
    qi = pl.program_id(1)
    ki = pl.program_id(2)

    @pl.when(ki == 0)
    def _():
        m_ref[...] = jnp.full(m_ref.shape, -1e30, F32)
        l_ref[...] = jnp.zeros(l_ref.shape, F32)
        acc_ref[...] = jnp.zeros(acc_ref.shape, F32)

    def block(masked):
        lane = lax.broadcasted_iota(jnp.int32, (tq, 2 * DQK), 1)
        slot = lane % DQK
        rows = lax.broadcasted_iota(jnp.int32, (tq, 2 * DQK), 0)
        lo = (rows % 256).astype(F32)
        hi = (rows // 256).astype(F32)
        off = ((qi - ki) * tq).astype(F32)
        zero = jnp.zeros((tq, 2 * DQK), F32)
        fq = jnp.where(slot == 0, -lo, jnp.where(slot == 1, -256.0 * hi, jnp.where(
            slot == 2, 1.0, jnp.where(slot == 3, 256.0, jnp.where(slot == 4, -off, zero)))))
        fk = jnp.where(slot == 0, 1.0, jnp.where(slot == 1, 1.0, jnp.where(
            slot == 2, lo, jnp.where(slot == 3, hi, jnp.where(slot == 4, 1.0, zero))))).astype(BF16)
        first = lane < DQK
        if masked:
            causal = (lax.broadcasted_iota(jnp.int32, (tq, tq), 0) >= lax.broadcasted_iota(jnp.int32, (tq, tq), 1))
        q = (q_ref[...] * (DQK ** -0.5)).astype(BF16)
        k = k_ref[...].astype(BF16)
        v = v_ref[...].astype(BF16)
        for h in range(H_DIFF):
            hs = slice(h * DV, (h + 1) * DV)
            fqh = (fq * _alibi_slope(h)).astype(BF16)
            qh, kh, vh = q[:, hs], k[:, hs], v[:, hs]
            for c in range(2):
                j = 2 * h + c
                keep = first if c == 0 else jnp.logical_not(first)
                s = _dot_t(jnp.where(keep, qh, fqh), jnp.where(keep, kh, fk))
                if masked:
                    s = jnp.where(causal, s, -jnp.inf)
                m_prev = m_ref[j]
                m_new = jnp.maximum(m_prev, jnp.max(s, axis=-1, keepdims=True))
                alpha = jnp.exp(m_prev - m_new)
                p = jnp.exp(s - jnp.concatenate([m_new] * (tq // DV), axis=1))
                l_ref[j] = alpha * l_ref[j] + jnp.sum(p, axis=-1, keepdims=True)
                acc_ref[j] = alpha * acc_ref[j] + jnp.dot(p.astype(BF16), vh, preferred_element_type=F32)
                m_ref[j] = m_new

    @pl.when(ki < qi)
    def _():
        block(False)

    @pl.when(ki == qi)
    def _():
        block(True)
        lam = _lambda(lq1, lk1, lq2, lk2, lam_init)
        for h in range(H_DIFF):
            o = acc_ref[2 * h] / l_ref[2 * h] - lam * (acc_ref[2 * h + 1] / l_ref[2 * h + 1])
            sl = slice(h * DV, (h + 1) * DV)
            o_ref[:, sl] = _subln_gate(o, subg_ref[...], g_ref[:, sl], lam_init).astype(o_ref.dtype)


def _diff_prompt(p, lam_params, subg, nb, t, lam_init):
    tq = min(512, t)
    nq = t // tq
    vec = pl.BlockSpec((1, DQK), lambda b, i, j: (0, 0))

    def qmap(c):
        return lambda b, i, j: (b * nq + i, c)

    def kmap(c):
        return lambda b, i, j: (b * nq + jnp.minimum(i, j), c)

    return pl.pallas_call(
        functools.partial(_diff_prompt_kernel, tq=tq, lam_init=lam_init),
        grid=(nb, nq, nq),
        in_specs=[vec, vec, vec, vec, pl.BlockSpec((1, DV), lambda b, i, j: (0, 0)),
                  pl.BlockSpec((tq, BRANCH), qmap(C_DQ)),
                  pl.BlockSpec((tq, BRANCH), kmap(C_DK)),
                  pl.BlockSpec((tq, BRANCH), kmap(C_DV)),
                  pl.BlockSpec((tq, BRANCH), qmap(C_DG))],
        out_specs=pl.BlockSpec((tq, BRANCH), lambda b, i, j: (b * nq + i, 0)),
        out_shape=jax.ShapeDtypeStruct((nb * t, BRANCH), BF16),
        scratch_shapes=[pltpu.VMEM((2 * H_DIFF, tq, DV), F32)] * 3,
        compiler_params=_cparams(("parallel", "parallel", "arbitrary")),
        name="diff_attn_prompt",
    )(*lam_params, subg.reshape(1, DV), p, p, p, p)


def _diff_decode_kernel(pt_ref, lq1, lk1, lq2, lk2, subg_ref, q_ref, kc_ref, vc_ref, g_ref, *rest,
                        n_pages, lam_init):
    kp = rest[:n_pages]
    vp = rest[n_pages:2 * n_pages]
    o_ref = rest[2 * n_pages]
    nrow = 2 * H_DIFF
    row = lax.broadcasted_iota(jnp.int32, (nrow, BRANCH), 0)
    lane = lax.broadcasted_iota(jnp.int32, (nrow, BRANCH), 1)
    q = q_ref[...] * (DQK ** -0.5)
    qm = jnp.where(lane // DQK == row, q, 0.0).astype(BF16)

    rowc = lax.broadcasted_iota(jnp.int32, (nrow, 1), 0)
    slope = jnp.zeros((nrow, 1), F32)
    for h in range(H_DIFF):
        slope = jnp.where(rowc // 2 == h, _alibi_slope(h), slope)

    past = n_pages * PAGE_SIZE
    s = jnp.concatenate([jnp.dot(qm, kp[j][...].astype(BF16), preferred_element_type=F32)
                         for j in range(n_pages)], axis=-1)
    dist = (past - lax.broadcasted_iota(jnp.int32, (nrow, past), 1)).astype(F32)
    s = s + (-slope) * dist
    kc = kc_ref[...].astype(BF16).astype(F32)
    vc = vc_ref[...].astype(BF16).astype(F32)
    sc = jnp.sum(qm.astype(F32) * kc, axis=-1, keepdims=True)
    m = jnp.maximum(jnp.max(s, axis=-1, keepdims=True), sc)
    p = jnp.exp(s - m)
    pc = jnp.exp(sc - m)
    l_fin = jnp.sum(p, axis=-1, keepdims=True) + pc
    pb = p.astype(BF16)
    pcb = pc.astype(BF16).astype(F32)
    lam = _lambda(lq1, lk1, lq2, lk2, lam_init)
    sign = jnp.where(rowc % 2 == 0, 1.0, -lam) / l_fin
    g = g_ref[...]
    outs = []
    for h in range(H_DIFF):
        hs = slice(h * DV, (h + 1) * DV)
        acc = pcb * vc[:, hs]
        for j in range(n_pages):
            vh = vp[j][pl.ds(h, PAGE_SIZE, stride=H_DIFF), :].astype(BF16)
            acc = acc + jnp.dot(pb[:, j * PAGE_SIZE:(j + 1) * PAGE_SIZE], vh, preferred_element_type=F32)
        o = jnp.sum(jnp.where(rowc // 2 == h, acc * sign, 0.0), axis=0, keepdims=True)
        outs.append(_subln_gate(o, subg_ref[...], g[:, hs], lam_init))
    o_ref[...] = jnp.concatenate(outs, axis=-1).astype(o_ref.dtype)


def _diff_decode(p3, layer, cache_k, cache_v, page_table, lam_params, subg, lam_init):
    nb, n_pages = page_table.shape
    vec = pl.BlockSpec((1, DQK), lambda b, pt: (0, 0))

    def row(c):
        return pl.BlockSpec((None, 1, BRANCH), lambda b, pt: (b, 0, c))

    def page(j, shape):
        return pl.BlockSpec((None, None) + shape, lambda b, pt: (layer, pt[b * n_pages + j], 0, 0))

    kpages = [page(j, (BRANCH, PAGE_SIZE)) for j in range(n_pages)]
    vpages = [page(j, (PAGE_SIZE * H_DIFF, DV)) for j in range(n_pages)]
    return pl.pallas_call(
        functools.partial(_diff_decode_kernel, n_pages=n_pages, lam_init=lam_init),
        grid_spec=pltpu.PrefetchScalarGridSpec(
            num_scalar_prefetch=1,
            grid=(nb,),
            in_specs=[vec, vec, vec, vec, pl.BlockSpec((1, DV), lambda b, pt: (0, 0)),
                      row(C_DQ), row(C_DK), row(C_DV), row(C_DG)] + kpages + vpages,
            out_specs=pl.BlockSpec((None, 1, BRANCH), lambda b, pt: (b, 0, 0))),
        out_shape=jax.ShapeDtypeStruct((nb, 1, BRANCH), BF16),
        compiler_params=_cparams(("parallel",)),
        name="diff_attn_decode",
    )(page_table.reshape(-1), *lam_params, subg.reshape(1, DV), p3, p3, p3, p3,
      *([cache_k] * n_pages), *([cache_v] * n_pages))


def _lane_sum3(x, ones):
    x1 = x.astype(BF16)
    rem = x - x1.astype(F32)
    x2 = rem.astype(BF16)
    x3 = (rem - x2.astype(F32)).astype(BF16)
    d = lambda z: jnp.dot(z, ones, preferred_element_type=F32)
    return d(x1) + (d(x2) + d(x3))


def _seg_sum(x, bd_ref):
    return _lane_sum3(x, bd_ref[...])


def _rwkv_pre_kernel(*refs, shift, has_vres, tm):
    it = iter(refs)
    cur = [next(it) for _ in range(4)]
    prev = None if shift else [next(it) for _ in range(4)]
    mu = [next(it) for _ in range(4)]
    w0, wup, a0, aup, kk_ref, ka_ref, rk_ref, bd_ref = [next(it) for _ in range(8)]
    if has_vres:
        vf_ref, vr0, vra, vrb = [next(it) for _ in range(4)]
    outs = [next(it) for _ in range(7 if shift else 8)]
    carry = [next(it) for _ in range(4)] if shift else None

    xs = []
    for n in range(4):
        c = cur[n][...]
        if shift:
            @pl.when(pl.program_id(1) == 0)
            def _():
                carry[n][...] = jnp.zeros(carry[n].shape, F32)
            rolled = pltpu.roll(c, 1, 0)
            rows = lax.broadcasted_iota(jnp.int32, c.shape, 0)
            pv = jnp.where(rows == 0, carry[n][...], rolled)
            carry[n][...] = c[tm - 1:tm, :]
        else:
            pv = prev[n][...]
        xs.append(c + (pv - c) * mu[n][...])
    xr, xk, xv, xl = xs

    hw = jnp.dot(jnp.tanh(xl).astype(BF16), wup[...], preferred_element_type=F32)
    w = -DECAY_SCALE * jax.nn.sigmoid(w0[...] + hw)
    if not shift:
        w = jnp.exp(w)
    a = jax.nn.sigmoid(a0[...] + jnp.dot(xl.astype(BF16), aup[...], preferred_element_type=F32))
    if has_vres:
        t1 = jnp.dot(xv.astype(BF16), vra[...], preferred_element_type=F32)
        t2 = jnp.dot(t1.astype(BF16), vrb[...], preferred_element_type=F32)
        xv = xv + (vf_ref[...] - xv) * jax.nn.sigmoid(vr0[...] + t2)
    kk = xk * kk_ref[...]
    kk = kk / jnp.maximum(jnp.sqrt(_seg_sum(kk * kk, bd_ref)), 1e-12)
    k2 = xk * (1.0 + (a - 1.0) * ka_ref[...])
    bonus = _seg_sum(xr * k2 * rk_ref[...], bd_ref) * xv
    vals = (xr, w, k2, xv, -kk, kk * a)
    if shift:
        for o, val in zip(outs, vals + (bonus,)):
            o[...] = val
    else:
        for o, val in zip(outs, vals):
            o[...] = val.T
        outs[6][...] = bonus
        outs[7][...] = xv


def _rwkv_pre(p, prev, lw, vfirst, nb, t, tm):
    shift = prev is None
    has_vres = vfirst is not None
    nt = t // tm
    if shift:
        grid = (nb, nt)
        rmap = lambda c: (lambda b, i: (b * nt + i, c))
        cmap = lambda b, i: (0, 0)
        sem = ("parallel", "arbitrary")
    else:
        grid = (nt,)
        rmap = lambda c: (lambda i: (i, c))
        cmap = lambda i: (0, 0)
        sem = ("parallel",)
    wide = lambda c: pl.BlockSpec((tm, BRANCH), rmap(c))
    vec = lambda n: pl.BlockSpec((1, n), cmap)
    args, specs = [], []
    args += [p, p, p, p]
    specs += [wide(C_RR), wide(C_RK), wide(C_RV), pl.BlockSpec((tm, N_LORA), rmap(C_LORA128))]
    if not shift:
        args += [prev, prev, prev, prev]
        specs += [wide(0), wide(1), wide(2), pl.BlockSpec((tm, N_LORA), rmap(3 * BRANCH // N_LORA))]
    args += lw["mu"]
    specs += [vec(BRANCH)] * 3 + [vec(N_LORA)]
    args += [lw["w0"], lw["w_up"], lw["a0"], lw["a_up"], lw["k_k"], lw["k_a"], lw["r_k"], lw["bd"]]
    specs += [vec(BRANCH), pl.BlockSpec((N_LORA, BRANCH), cmap), vec(BRANCH), pl.BlockSpec((N_LORA, BRANCH), cmap),
              vec(BRANCH), vec(BRANCH), vec(BRANCH), pl.BlockSpec((BRANCH, BRANCH), cmap)]
    if has_vres:
        args += [vfirst, lw["vres_0"], lw["vres_a"], lw["vres_b"]]
        specs += [pl.BlockSpec((tm, BRANCH), rmap(0)), vec(BRANCH),
                  pl.BlockSpec(lw["vres_a"].shape, cmap), pl.BlockSpec(lw["vres_b"].shape, cmap)]
    out_spec = pl.BlockSpec((tm, BRANCH), rmap(0))
    row_shape = jax.ShapeDtypeStruct((nb * t, BRANCH), F32)
    if shift:
        out_specs, out_shape = [out_spec] * 7, [row_shape] * 7
    else:
        col_spec = pl.BlockSpec((BRANCH, tm), lambda i: (0, i))
        out_specs = [col_spec] * 6 + [out_spec] * 2
        out_shape = [jax.ShapeDtypeStruct((BRANCH, nb * t), F32)] * 6 + [row_shape] * 2
    scratch = [pltpu.VMEM((1, BRANCH), F32)] * 3 + [pltpu.VMEM((1, N_LORA), F32)] if shift else []
    return pl.pallas_call(
        functools.partial(_rwkv_pre_kernel, shift=shift, has_vres=has_vres, tm=tm),
        grid=grid, in_specs=specs, out_specs=out_specs,
        out_shape=out_shape,
        scratch_shapes=scratch,
        compiler_params=_cparams(sem),
        name="rwkv_pre",
    )(*args)


RWKV_CHUNK = 64

_NN = ((1,), (0,))
_NT = ((1,), (1,))
_TN = ((0,), (0,))


def _split(x):
    hi = x.astype(BF16)
    return hi, (x - hi.astype(F32)).astype(BF16)


def _mm3(a, b, dims):
    dg = lambda x, y: lax.dot_general(x, y, (dims, ((), ())), preferred_element_type=F32)
    return dg(a[0], b[0]) + (dg(a[0], b[1]) + dg(a[1], b[0]))


def _rwkv_scan_kernel(r_ref, lw_ref, k_ref, v_ref, a_ref, b_ref, y_ref, s_ref):
    C = RWKV_CHUNK
    W2 = 2 * R_HEAD

    @pl.when(pl.program_id(1) == 0)
    def _():
        s_ref[...] = jnp.zeros(s_ref.shape, F32)

    lw = lw_ref[...]
    tri = (lax.broadcasted_iota(jnp.int32, (C, C), 0) >= lax.broadcasted_iota(jnp.int32, (C, C), 1)).astype(BF16)
    l1 = lw.astype(BF16)
    rem = lw - l1.astype(F32)
    l2 = rem.astype(BF16)
    l3 = (rem - l2.astype(F32)).astype(BF16)
    cs = (jnp.dot(tri, l1, preferred_element_type=F32) + jnp.dot(tri, l2, preferred_element_type=F32)
          + jnp.dot(tri, l3, preferred_element_type=F32))
    p_in = jnp.exp(cs)
    p_inv = jnp.exp(-cs)
    rt = r_ref[...] * p_in
    at = a_ref[...] * jnp.exp(cs - lw)
    bt = b_ref[...] * p_inv
    kt = k_ref[...] * p_inv
    v = v_ref[...]
    p_last = p_in[C - 1:C, :]

    heads = range(H_RWKV)
    pairs = range(H_RWKV // 2)
    psl = [slice(p * W2, (p + 1) * W2) for p in pairs]
    lane2 = lax.broadcasted_iota(jnp.int32, (2 * C, W2), 1)
    keep = [lane2 < R_HEAD, lane2 >= R_HEAD]
    zb = jnp.zeros((2 * C, W2), BF16)
    gr = lax.broadcasted_iota(jnp.int32, (2 * C, 2 * C), 0)
    gc = lax.broadcasted_iota(jnp.int32, (2 * C, 2 * C), 1) % C
    gmask = jnp.where(gr < C, gr - 1, gr - C) >= gc
    second = lax.broadcasted_iota(jnp.int32, (C, W2), 1) >= R_HEAD

    ar_s = [_split(jnp.concatenate([at[:, s], rt[:, s]], axis=0)) for s in psl]
    bk_s = [_split(jnp.concatenate([bt[:, s], kt[:, s]], axis=0)) for s in psl]
    s0 = [s_ref[p] for p in pairs]
    s0_s = [_split(z) for z in s0]
    mask2 = lambda z, h: tuple(jnp.where(keep[h % 2], x, zb) for x in z[h // 2])
    arm = [mask2(ar_s, h) for h in heads]
    vh = [v[:, h * R_HEAD:(h + 1) * R_HEAD] for h in heads]
    zv_s = [_split(jnp.concatenate([jnp.zeros((C, R_HEAD), F32), z], axis=0)) for z in vh]
    g = [jnp.where(gmask, _mm3(arm[h], bk_s[h // 2], _NT), 0.0) for h in heads]
    x = [_mm3(arm[h], s0_s[h // 2], _NT) for h in heads]
    rhs = [x[h][:C] + _mm3(_split(g[h][:C]), zv_s[h], _NN) for h in heads]
    w = [jnp.concatenate([g[h][:C, :C], rhs[h]], axis=1) for h in heads]
    for _ in range(int(math.log2(C))):
        w_s = [_split(z) for z in w]
        w = [_mm3((w_s[h][0][:, :R_HEAD], w_s[h][1][:, :R_HEAD]), w_s[h], _NN) + jnp.where(second, w[h], 0.0)
             for h in heads]
    uv_s = [_split(jnp.concatenate([w[h][:, R_HEAD:], vh[h]], axis=0)) for h in heads]
    for h in heads:
        y_ref[:, h * R_HEAD:(h + 1) * R_HEAD] = x[h][C:] + _mm3(_split(g[h][C:]), uv_s[h], _NN)
    ds = [_mm3(uv_s[h], mask2(bk_s, h), _TN) for h in heads]
    for p in pairs:
        s_ref[p] = (s0[p] + ds[2 * p] + ds[2 * p + 1]) * p_last[:, psl[p]]


def _rwkv_scan(seqs, nb, t):
    tc = RWKV_CHUNK
    hp = H_RWKV // 2
    blk = pl.BlockSpec((None, tc, BRANCH), lambda b, c: (b, c, 0))
    y, s = pl.pallas_call(
        _rwkv_scan_kernel,
        grid=(nb, t // tc),
        in_specs=[blk] * 6,
        out_specs=[blk, pl.BlockSpec((hp, R_HEAD, 2 * R_HEAD), lambda b, c: (b, 0, 0))],
        out_shape=[jax.ShapeDtypeStruct((nb, t, BRANCH), F32),
                   jax.ShapeDtypeStruct((nb * hp, R_HEAD, 2 * R_HEAD), F32)],
        compiler_params=_cparams(("parallel", "arbitrary")),
        name="rwkv_scan",
    )(*seqs)
    s = s.reshape(nb, hp, R_HEAD, 2, R_HEAD).transpose(0, 1, 3, 2, 4).reshape(nb, H_RWKV, R_HEAD, R_HEAD)
    return y, s


def _rwkv_step_kernel(r_ref, w_ref, k_ref, a_ref, b_ref, v_ref, s_ref, y_ref, so_ref):
    r, w, k, a, b = [ref[...] for ref in (r_ref, w_ref, k_ref, a_ref, b_ref)]
    vals = range(R_HEAD)
    st = [s_ref[v] for v in vals]
    sa = [jnp.sum(z * a, axis=0, keepdims=True) for z in st]
    st = [z * w + x * b + v_ref[v:v + 1, :] * k for z, x, v in zip(st, sa, vals)]
    for v in vals:
        so_ref[v] = st[v]
    y_ref[...] = jnp.concatenate([jnp.sum(z * r, axis=0, keepdims=True) for z in st], axis=0)


def _rwkv_step(seqs, state, layer):
    n = seqs[0].shape[1]
    vec = pl.BlockSpec((R_HEAD, n), lambda h: (h, 0))
    stb = pl.BlockSpec((R_HEAD, R_HEAD, n), lambda h: (h, 0, 0))
    return pl.pallas_call(
        _rwkv_step_kernel,
        grid=(H_RWKV,),
        in_specs=[vec] * 6 + [pl.BlockSpec((None, R_HEAD, R_HEAD, n), lambda h: (layer, h, 0, 0))],
        out_specs=[vec, stb],
        out_shape=[jax.ShapeDtypeStruct((BRANCH, n), F32),
                   jax.ShapeDtypeStruct((BRANCH, R_HEAD, n), F32)],
        compiler_params=_cparams(("parallel",)),
        name="rwkv_step",
    )(*seqs, state)


def _rwkv_post_kernel(y_ref, bonus_ref, g_ref, lg_ref, lb_ref, bd_ref, o_ref):
    y = y_ref[...]
    mu = _seg_sum(y, bd_ref) * (1.0 / R_HEAD)
    d = y - mu
    var = _seg_sum(d * d, bd_ref) * (1.0 / R_HEAD)
    o = d * lax.rsqrt(var + LNX_EPS) * lg_ref[...] + lb_ref[...] + bonus_ref[...]
    o_ref[...] = (o * _silu(g_ref[...])).astype(o_ref.dtype)


def _rwkv_post(y, bonus, p, lw, tm):
    m = y.shape[0]
    blk = pl.BlockSpec((tm, BRANCH), lambda i: (i, 0))
    vec = pl.BlockSpec((1, BRANCH), lambda i: (0, 0))
    return pl.pallas_call(
        _rwkv_post_kernel,
        grid=(m // tm,),
        in_specs=[blk, blk, pl.BlockSpec((tm, BRANCH), lambda i: (i, C_RG)), vec, vec,
                  pl.BlockSpec((BRANCH, BRANCH), lambda i: (0, 0))],
        out_specs=blk,
        out_shape=jax.ShapeDtypeStruct((m, BRANCH), BF16),
        compiler_params=_cparams(("parallel",)),
        name="rwkv_post",
    )(y, bonus, p, lw["lnx_g"], lw["lnx_b"], lw["bd"])


def _softplus(x):
    return jnp.maximum(x, 0.0) + jnp.log1p(jnp.exp(-jnp.abs(x)))


def _neg_exp_alog(alog_ref):
    lanes = lax.broadcasted_iota(jnp.int32, alog_ref.shape, 1)
    return jnp.where(lanes < H_MAMBA, -jnp.exp(alog_ref[...]), 0.0)


def _gated_norm(ym, z, g):
    u = ym * _silu(z)
    ms = jnp.mean(u * u, axis=-1, keepdims=True)
    return u * lax.rsqrt(ms + RMS_EPS) * g


def _mamba_prompt_kernel(mx_ref, mbc_ref, mdt_ref, mz_ref, cwx_ref, cwb_ref, cbx_ref, cbb_ref, dtb_ref, alog_ref,
                         dskip_ref, gn_ref, o_ref, s_ref, extx_ref, extb_ref, ym_ref):
    L = SSD_CHUNK
    pad = 8

    @pl.when(pl.program_id(1) == 0)
    def _():
        extx_ref[0:pad, :] = jnp.zeros((pad, BRANCH), F32)
        extb_ref[0:pad, :] = jnp.zeros((pad, BRANCH), F32)
        s_ref[...] = jnp.zeros(s_ref.shape, F32)

    def conv(ext_ref, cur_ref, cw_ref, cb_ref):
        ext_ref[pad:pad + L, :] = cur_ref[...]
        acc = cb_ref[...]
        for j in range(M_CONV):
            lo = pad - (M_CONV - 1) + j
            acc = acc + ext_ref[lo:lo + L, :] * cw_ref[j:j + 1, :]
        ext_ref[0:pad, :] = ext_ref[L:L + pad, :]
        return _silu(acc)

    x = conv(extx_ref, mx_ref, cwx_ref, cbx_ref)
    bc = conv(extb_ref, mbc_ref, cwb_ref, cbb_ref)
    dt = _softplus(mdt_ref[...] + dtb_ref[...])
    adt = dt * _neg_exp_alog(alog_ref)
    rowi = lax.broadcasted_iota(jnp.int32, (L, L), 0)
    coli = lax.broadcasted_iota(jnp.int32, (L, L), 1)
    lower = rowi >= coli
    acs = jnp.dot(lower.astype(F32), adt, preferred_element_type=F32, precision=HIGHEST)
    acs_t = acs.T
    last = acs[L - 1:L, :]
    dstate = jnp.exp(last - acs)
    eacs = jnp.exp(acs)
    elast = jnp.exp(last)
    bcb = bc.astype(BF16)
    grp = []
    for g in range(M_GROUPS):
        bg = bcb[:, g * M_STATE:(g + 1) * M_STATE]
        cg = bcb[:, (M_GROUPS + g) * M_STATE:(M_GROUPS + g + 1) * M_STATE]
        grp.append((bg, cg, _dot_t(cg, bg)))
    for h in range(H_MAMBA):
        bg, cg, cb = grp[h // (H_MAMBA // M_GROUPS)]
        sl = slice(h * M_HEADDIM, (h + 1) * M_HEADDIM)
        decay = jnp.where(lower, jnp.exp(acs[:, h:h + 1] - acs_t[h:h + 1, :]), 0.0)
        xh = x[:, sl]
        xdt = xh * dt[:, h:h + 1]
        yd = jnp.dot((cb * decay).astype(BF16), xdt.astype(BF16), preferred_element_type=F32)
        st = s_ref[h]
        yo = _dot_t(cg, st.astype(BF16)) * eacs[:, h:h + 1]
        upd = lax.dot_general((xdt * dstate[:, h:h + 1]).astype(BF16), bg, (((0,), (0,)), ((), ())),
                              preferred_element_type=F32)
        s_ref[h] = st * elast[:, h:h + 1] + upd
        ym_ref[:, sl] = yd + yo + dskip_ref[:, sl] * xh
    o_ref[...] = _gated_norm(ym_ref[...], mz_ref[...], gn_ref[...]).astype(o_ref.dtype)


def _mamba_prompt(p, lw, nb, t):
    L = SSD_CHUNK
    nt = t // L
    rmap = lambda c: (lambda b, i: (b * nt + i, c))
    cmap = lambda b, i: (0, 0)
    wide = lambda c: pl.BlockSpec((L, BRANCH), rmap(c))
    vec = lambda n: pl.BlockSpec((1, n), cmap)
    cw = pl.BlockSpec((M_CONV, BRANCH), cmap)
    return pl.pallas_call(
        _mamba_prompt_kernel,
        grid=(nb, nt),
        in_specs=[wide(C_MX), wide(C_MBC), pl.BlockSpec((L, 128), rmap(C_DT128)), wide(C_MZ),
                  cw, cw, vec(BRANCH), vec(BRANCH), vec(128), vec(128), vec(BRANCH), vec(BRANCH)],
        out_specs=[pl.BlockSpec((L, BRANCH), rmap(0)),
                   pl.BlockSpec((None, H_MAMBA, M_HEADDIM, M_STATE), lambda b, i: (b, 0, 0, 0))],
        out_shape=[jax.ShapeDtypeStruct((nb * t, BRANCH), BF16),
                   jax.ShapeDtypeStruct((nb, H_MAMBA, M_HEADDIM, M_STATE), F32)],
        scratch_shapes=[pltpu.VMEM((L + 8, BRANCH), F32), pltpu.VMEM((L + 8, BRANCH), F32),
                        pltpu.VMEM((L, BRANCH), F32)],
        compiler_params=_cparams(("parallel", "arbitrary")),
        name="mamba_prompt",
    )(p, p, p, p, lw["cw_x"], lw["cw_bc"], lw["cb_x"], lw["cb_bc"], lw["dt_bias"], lw["a_log"],
      lw["d_skip"], lw["gnorm_g"])


def _mamba_dec_pre_kernel(mx_ref, mbc_ref, mdt_ref, sx_ref, sb_ref, cwx_ref, cwb_ref, cbx_ref, cbb_ref,
                          dtb_ref, alog_ref, x_ref, bc_ref, xdt_ref, da_ref):
    def conv(st_ref, cur_ref, cw_ref, cb_ref):
        acc = cb_ref[...]
        for j in range(M_CONV - 1):
            acc = acc + st_ref[j] * cw_ref[j:j + 1, :]
        acc = acc + cur_ref[...] * cw_ref[M_CONV - 1:M_CONV, :]
        return _silu(acc)

    x = conv(sx_ref, mx_ref, cwx_ref, cbx_ref)
    x_ref[...] = x
    bc_ref[...] = conv(sb_ref, mbc_ref, cwb_ref, cbb_ref)
    dt = _softplus(mdt_ref[...] + dtb_ref[...])
    da = jnp.exp(dt * _neg_exp_alog(alog_ref))
    n = x.shape[0]
    for h in range(H_MAMBA):
        sl = slice(h * M_HEADDIM, (h + 1) * M_HEADDIM)
        xdt_ref[:, sl] = x[:, sl] * dt[:, h:h + 1]
        da_ref[:, sl] = jnp.broadcast_to(da[:, h:h + 1], (n, M_HEADDIM))


def _mamba_dec_pre(p, conv_x, conv_bc, lw):
    n = p.shape[0]
    cmap = lambda i: (0, 0)
    wide = lambda c: pl.BlockSpec((n, BRANCH), lambda i: (0, c))
    vec = lambda m: pl.BlockSpec((1, m), cmap)
    cw = pl.BlockSpec((M_CONV, BRANCH), cmap)
    st = pl.BlockSpec((M_CONV - 1, n, BRANCH), lambda i: (0, 0, 0))
    out = pl.BlockSpec((n, BRANCH), cmap)
    return pl.pallas_call(
        _mamba_dec_pre_kernel,
        grid=(1,),
        in_specs=[wide(C_MX), wide(C_MBC), pl.BlockSpec((n, 128), lambda i: (0, C_DT128)), st, st,
                  cw, cw, vec(BRANCH), vec(BRANCH), vec(128), vec(128)],
        out_specs=[out] * 4,
        out_shape=[jax.ShapeDtypeStruct((n, BRANCH), F32)] * 4,
        compiler_params=_cparams(("arbitrary",)),
        name="mamba_dec_pre",
    )(p, p, p, conv_x, conv_bc, lw["cw_x"], lw["cw_bc"], lw["cb_x"], lw["cb_bc"], lw["dt_bias"], lw["a_log"])


def _mamba_step_kernel(bc_ref, xdtt_ref, dat_ref, s_ref, yt_ref, so_ref, *, bb):
    units = [(j, h // (H_MAMBA // M_GROUPS), slice(h * M_HEADDIM, (h + 1) * M_HEADDIM))
             for j in range(bb) for h in range(H_MAMBA)]
    st = []
    for j, g, sl in units:
        bg = bc_ref[j:j + 1, g * M_STATE:(g + 1) * M_STATE]
        z = s_ref[j, sl, :] * dat_ref[sl, j:j + 1] + xdtt_ref[sl, j:j + 1] * bg
        so_ref[j, sl, :] = z
        st.append(z)
    ys = [jnp.sum(z * bc_ref[j:j + 1, (M_GROUPS + g) * M_STATE:(M_GROUPS + g + 1) * M_STATE], axis=-1, keepdims=True)
          for z, (j, g, sl) in zip(st, units)]
    for y, (j, g, sl) in zip(ys, units):
        yt_ref[sl, j:j + 1] = y


def _mamba_step(bc, xdtt, dat, state, layer, bb=8):
    n = bc.shape[0]
    rowb = pl.BlockSpec((bb, BRANCH), lambda i: (i, 0))
    colb = pl.BlockSpec((None, BRANCH, bb), lambda i: (i, 0, 0))
    stb = pl.BlockSpec((bb, BRANCH, M_STATE), lambda i: (i, 0, 0))
    return pl.pallas_call(
        functools.partial(_mamba_step_kernel, bb=bb),
        grid=(n // bb,),
        in_specs=[rowb, colb, colb, pl.BlockSpec((None, bb, BRANCH, M_STATE), lambda i: (layer, i, 0, 0))],
        out_specs=[colb, stb],
        out_shape=[jax.ShapeDtypeStruct((n // bb, BRANCH, bb), F32),
                   jax.ShapeDtypeStruct((n, BRANCH, M_STATE), F32)],
        compiler_params=_cparams(("parallel",)),
        name="mamba_step",
    )(bc, xdtt, dat, state)


def _mamba_dec_post_kernel(y_ref, x_ref, z_ref, dskip_ref, gn_ref, o_ref):
    ym = y_ref[...] + dskip_ref[...] * x_ref[...]
    o_ref[...] = _gated_norm(ym, z_ref[...], gn_ref[...]).astype(o_ref.dtype)


def _mamba_dec_post(y, x, p, lw):
    n = y.shape[0]
    blk = pl.BlockSpec((n, BRANCH), lambda i: (0, 0))
    vec = pl.BlockSpec((1, BRANCH), lambda i: (0, 0))
    return pl.pallas_call(
        _mamba_dec_post_kernel,
        grid=(1,),
        in_specs=[blk, blk, pl.BlockSpec((n, BRANCH), lambda i: (0, C_MZ)), vec, vec],
        out_specs=blk,
        out_shape=jax.ShapeDtypeStruct((n, BRANCH), BF16),
        compiler_params=_cparams(("arbitrary",)),
        name="mamba_dec_post",
    )(y, x, p, lw["d_skip"], lw["gnorm_g"])


def _xattn_prompt_kernel(q_ref, g_ref, mk_ref, mv_ref, o_ref):
    q = q_ref[...].astype(BF16)
    mk = mk_ref[...].astype(BF16)
    mv = mv_ref[...].astype(BF16)
    for h in range(H_X):
        sl = slice(h * D_X, (h + 1) * D_X)
        s = _dot_t(q[:, sl], mk[:, sl]) * (D_X ** -0.5)
        e = jnp.exp(s - jnp.max(s, axis=-1, keepdims=True))
        o = jnp.dot(e.astype(BF16), mv[:, sl], preferred_element_type=F32) / jnp.sum(e, axis=-1, keepdims=True)
        o_ref[:, sl] = (o * _silu(g_ref[:, sl])).astype(o_ref.dtype)


def _xattn_prompt(p, mkv, nb, t):
    tq = min(512, t)
    nq = t // tq
    return pl.pallas_call(
        _xattn_prompt_kernel,
        grid=(nb, nq),
        in_specs=[pl.BlockSpec((tq, BRANCH), lambda b, i: (b * nq + i, C_XQ)),
                  pl.BlockSpec((tq, BRANCH), lambda b, i: (b * nq + i, C_XG)),
                  pl.BlockSpec((N_MEM, BRANCH), lambda b, i: (b, 0)),
                  pl.BlockSpec((N_MEM, BRANCH), lambda b, i: (b, 1))],
        out_specs=pl.BlockSpec((tq, BRANCH), lambda b, i: (b * nq + i, 0)),
        out_shape=jax.ShapeDtypeStruct((nb * t, BRANCH), BF16),
        compiler_params=_cparams(("parallel", "parallel")),
        name="xattn_prompt",
    )(p, p, mkv, mkv)


def _xattn_decode_kernel(q_ref, g_ref, mk_ref, mv_ref, o_ref, *, bb):
    nrow = 8
    units = [(j, h) for j in range(bb) for h in range(H_X)]
    q = [q_ref[j].astype(BF16) for j in range(bb)]
    s = [_dot_t(jnp.broadcast_to(q[j][:, h * D_X:(h + 1) * D_X], (nrow, D_X)),
                mk_ref[j, pl.ds(h, N_MEM, stride=H_X), :].astype(BF16)) * (D_X ** -0.5) for j, h in units]
    e = [jnp.exp(z - jnp.max(z, axis=-1, keepdims=True)) for z in s]
    r = [jnp.dot(z.astype(BF16), mv_ref[j, pl.ds(h, N_MEM, stride=H_X), :].astype(BF16),
                 preferred_element_type=F32) for z, (j, h) in zip(e, units)]
    outs = [(x / jnp.sum(z, axis=-1, keepdims=True))[0:1, :] for x, z in zip(r, e)]
    for j in range(bb):
        o = jnp.concatenate(outs[j * H_X:(j + 1) * H_X], axis=-1)
        o_ref[j] = (o * _silu(g_ref[j])).astype(o_ref.dtype)


def _xattn_decode(p3, layer, mem_k, mem_v, bb=4):
    n = p3.shape[0]
    mem = pl.BlockSpec((None, bb, N_MEM * H_X, D_X), lambda b: (layer, b, 0, 0))
    return pl.pallas_call(
        functools.partial(_xattn_decode_kernel, bb=bb),
        grid=(n // bb,),
        in_specs=[pl.BlockSpec((bb, 1, BRANCH), lambda b: (b, 0, C_XQ)),
                  pl.BlockSpec((bb, 1, BRANCH), lambda b: (b, 0, C_XG)), mem, mem],
        out_specs=pl.BlockSpec((bb, 1, BRANCH), lambda b: (b, 0, 0)),
        out_shape=jax.ShapeDtypeStruct((n, 1, BRANCH), BF16),
        compiler_params=_cparams(("parallel",)),
        name="xattn_decode",
    )(p3, p3, mem_k, mem_v)


N_IN = 8 * BRANCH + N_SHIFT + 2 * BRANCH + H_MAMBA


def _reorder_w_in_kernel(w_ref, o_ref):
    o_rs, o_rg, o_mz, o_mxbc, o_dt, o_xq = 2048, 3712, 4224, 4736, 5760, 5768
    o_lora = o_rs + 3 * BRANCH

    def put(dst, lo, hi):
        o_ref[dst:dst + hi - lo, :] = w_ref[lo:hi, :].astype(o_ref.dtype)

    put(0, 0, o_lora)
    put(C_RG * BRANCH, o_rg, o_mxbc)
    put(C_XQ * BRANCH, o_xq, N_IN)
    put(C_MX * BRANCH, o_mxbc, o_dt)
    put(C_LORA128 * 128, o_lora, o_rg)
    pad = jnp.zeros((128 - H_MAMBA, w_ref.shape[1]), F32)
    o_ref[C_DT128 * 128:N_PROJ, :] = jnp.concatenate([w_ref[o_dt:o_xq, :], pad], axis=0).astype(o_ref.dtype)


def _reorder_w_in(w_in_t, tk=512):
    depth = w_in_t.shape[0]
    return pl.pallas_call(
        _reorder_w_in_kernel,
        grid=(depth, D_MODEL // tk),
        in_specs=[pl.BlockSpec((None, N_IN, tk), lambda l, i: (l, 0, i))],
        out_specs=pl.BlockSpec((None, N_PROJ, tk), lambda l, i: (l, 0, i)),
        out_shape=jax.ShapeDtypeStruct((depth, N_PROJ, D_MODEL), BF16),
        compiler_params=_cparams(("parallel", "parallel")),
        name="reorder_w_in",
    )(w_in_t)


def _layer_weights(l, prm):
    row = lambda v: v.reshape(1, -1).astype(F32)
    mu = prm["shift_mu"][l].astype(F32)
    seg = jnp.arange(BRANCH) // R_HEAD
    zl = jnp.zeros((N_LORA // 2, BRANCH), F32)
    cw = prm["conv_w"][l].astype(F32)
    cb = prm["conv_b"][l].astype(F32)
    pad8 = lambda v: jnp.pad(v.astype(F32), (0, 128 - H_MAMBA)).reshape(1, 128)
    lw = {
        "w_in": prm["w_in_reordered"], "norm_g": prm["norm_g"][l], "w_out": prm["w_out"][l].astype(BF16),
        "lam": [row(prm[n][l]) for n in ("lam_q1", "lam_k1", "lam_q2", "lam_k2")],
        "subln_g": prm["subln_g"][l].astype(F32),
        "mu": [row(mu[:BRANCH]), row(mu[BRANCH:2 * BRANCH]), row(mu[2 * BRANCH:3 * BRANCH]), row(mu[3 * BRANCH:])],
        "w0": row(prm["w0"][l]), "a0": row(prm["a0"][l]),
        "w_up": jnp.concatenate([prm["w_up"][l].astype(F32), zl], axis=0).astype(BF16),
        "a_up": jnp.concatenate([zl, prm["a_up"][l].astype(F32)], axis=0).astype(BF16),
        "k_k": row(prm["k_k"][l]), "k_a": row(prm["k_a"][l]), "r_k": row(prm["r_k"][l]),
        "lnx_g": row(prm["lnx_g"][l]), "lnx_b": row(prm["lnx_b"][l]),
        "bd": (seg[:, None] == seg[None, :]).astype(BF16),
        "cw_x": cw[:, :BRANCH], "cw_bc": cw[:, BRANCH:], "cb_x": row(cb[:BRANCH]), "cb_bc": row(cb[BRANCH:]),
        "dt_bias": pad8(prm["dt_bias"][l]), "a_log": pad8(prm["a_log"][l]),
        "d_skip": row(jnp.repeat(prm["d_skip"][l].astype(F32), M_HEADDIM)),
        "gnorm_g": row(prm["gnorm_g"][l]),
        "w_mkv": jnp.concatenate([prm["w_mk"][l], prm["w_mv"][l]], axis=1).astype(BF16),
        "mem_norm_g": prm["mem_norm_g"][l],
    }
    if l > 0:
        lw["vres_0"] = row(prm["vres_0"][l - 1])
        lw["vres_a"] = prm["vres_a"][l - 1].astype(BF16)
        lw["vres_b"] = prm["vres_b"][l - 1].astype(BF16)
    return lw


def _lam_init(l):
    return 0.8 - 0.6 * math.exp(-0.3 * l)


def _prompt_layer(x, l, lw, mem, vfirst, nb, t, fg, final, tm_proj, tm_row):
    p = _rms_matmul(x, lw["norm_g"], lw["w_in"], tm_proj, 1152, layer=l)
    mkv = _rms_matmul(mem, lw["mem_norm_g"], lw["w_mkv"], mem.shape[0], 2 * BRANCH)
    out_a = _diff_prompt(p, lw["lam"], lw["subln_g"], nb, t, _lam_init(l))
    r, w, k, v, na, b, bonus = _rwkv_pre(p, None, lw, vfirst, nb, t, tm_row)
    seqs = [z.reshape(nb, t, BRANCH) for z in (r, w, k, v, na, b)]
    y, rw_state = _rwkv_scan(seqs, nb, t)
    out_b = _rwkv_post(y.reshape(nb * t, BRANCH), bonus, p, lw, tm_row)
    out_m, ssm_state = _mamba_prompt(p, lw, nb, t)
    out_x = _xattn_prompt(p, mkv, nb, t)
    x = _outproj((out_a, out_b, out_m, out_x), lw["w_out"], x, fg, tm_row, final)
    return x, p, mkv, rw_state, ssm_state, (v if vfirst is None else vfirst)


def _decode_layer(x, l, lw, caches, vfirst, fg, final):
    cache_k, cache_v, mem_k, mem_v, page_table, rw_prev, rw_state, conv_state, ssm_state = caches
    n = x.shape[0]
    bb = 8
    p = _rms_matmul(x, lw["norm_g"], lw["w_in"], n, 1152, layer=l)
    p3 = p.reshape(n, 1, N_PROJ)
    out_a = _diff_decode(p3, l, cache_k, cache_v, page_table, lw["lam"], lw["subln_g"], _lam_init(l)).reshape(n, BRANCH)

    rt, wt, kt, vt, nat, bt, bonus, v = _rwkv_pre(p, rw_prev, lw, vfirst, 1, n, n)
    to_cols = lambda z: z.reshape(n // bb, bb, BRANCH).transpose(0, 2, 1)
    from_cols = lambda z: z.transpose(0, 2, 1).reshape(n, BRANCH)
    yt, rw_new = _rwkv_step((rt, wt, kt, nat, bt, vt), rw_state, l)
    out_b = _rwkv_post(yt.T, bonus, p, lw, n)

    cs = conv_state.transpose(1, 0, 2)
    xm, bc, xdt, da = _mamba_dec_pre(p, cs[:, :, :BRANCH], cs[:, :, BRANCH:], lw)
    ymt, ssm_new = _mamba_step(bc, to_cols(xdt), to_cols(da), ssm_state.reshape(-1, n, BRANCH, M_STATE), l, bb)
    out_m = _mamba_dec_post(from_cols(ymt), xm, p, lw)

    out_x = _xattn_decode(p3, l, mem_k, mem_v).reshape(n, BRANCH)
    x = _outproj((out_a, out_b, out_m, out_x), lw["w_out"], x, fg, n, final)
    return x, p, rw_new, ssm_new, (v if vfirst is None else vfirst)


def _cols(p, c, width=BRANCH):
    return p[..., c * width:(c + 1) * width]


def kernel(x_prompt, x_sample, cache_diff_k, cache_diff_v, cache_mem_k, cache_mem_v, state_rwkv_shift, state_rwkv, state_conv, state_ssm, page_table, mem_prompt, norm_g, w_in, w_out, lam_q1, lam_k1, lam_q2, lam_k2, subln_g, shift_mu, w0, w_up, a0, a_up, k_k, k_a, r_k, lnx_g, lnx_b, vres_0, vres_a, vres_b, conv_w, conv_b, dt_bias, a_log, d_skip, gnorm_g, mem_norm_g, w_mk, w_mv, final_g):
    prm = dict(norm_g=norm_g, w_in=w_in, w_out=w_out, lam_q1=lam_q1, lam_k1=lam_k1, lam_q2=lam_q2, lam_k2=lam_k2,
               subln_g=subln_g, shift_mu=shift_mu, w0=w0, w_up=w_up, a0=a0, a_up=a_up, k_k=k_k, k_a=k_a, r_k=r_k,
               lnx_g=lnx_g, lnx_b=lnx_b, vres_0=vres_0, vres_a=vres_a, vres_b=vres_b, conv_w=conv_w, conv_b=conv_b,
               dt_bias=dt_bias, a_log=a_log, d_skip=d_skip, gnorm_g=gnorm_g, mem_norm_g=mem_norm_g,
               w_mk=w_mk, w_mv=w_mv,
               w_in_reordered=_reorder_w_in(jnp.transpose(w_in, (0, 2, 1))))
    depth = w_in.shape[0]
    nb, t, _ = x_prompt.shape
    n = x_sample.shape[0]
    n_pool = cache_diff_k.shape[1]
    ck = jnp.transpose(cache_diff_k, (0, 1, 3, 4, 5, 2)).reshape(depth, n_pool, BRANCH, PAGE_SIZE)
    cv = cache_diff_v.reshape(depth, n_pool, PAGE_SIZE * H_DIFF, DV)
    cmk = cache_mem_k.reshape(depth, n, N_MEM * H_X, D_X)
    cmv = cache_mem_v.reshape(depth, n, N_MEM * H_X, D_X)
    rws = jnp.transpose(state_rwkv, (0, 2, 3, 4, 1)).reshape(depth, BRANCH, R_HEAD, n)
    mem = mem_prompt.reshape(nb * N_MEM, D_MODEL)

    xp = x_prompt.reshape(nb * t, D_MODEL)
    xs = x_sample.reshape(n, D_MODEL)
    vf_p = vf_s = None
    outs = [[] for _ in range(14)]
    for l in range(depth):
        lw = _layer_weights(l, prm)
        final = l == depth - 1
        xp, p, mkv, rw_p, ssm_p, vf_p = _prompt_layer(xp, l, lw, mem, vf_p, nb, t, final_g, final,
                                                      min(1024, nb * t), min(512, t))
        p3 = p.reshape(nb, t, N_PROJ)
        last = p3[:, t - 1]
        outs[0].append(_cols(p3, C_DK).reshape(nb, t, H_DIFF, 2, DQK))
        outs[1].append(_cols(p3, C_DV).reshape(nb, t, H_DIFF, DV))
        outs[2].append(mkv[:, :BRANCH].reshape(nb, N_MEM, H_X, D_X))
        outs[3].append(mkv[:, BRANCH:].reshape(nb, N_MEM, H_X, D_X))
        outs[4].append(jnp.concatenate([_cols(last, C_RR), _cols(last, C_RK), _cols(last, C_RV),
                                        _cols(last, C_LORA128, 128)], axis=-1))
        outs[5].append(rw_p.reshape(nb, H_RWKV, R_HEAD, R_HEAD))
        tail = p3[:, t - (M_CONV - 1):]
        outs[6].append(jnp.concatenate([_cols(tail, C_MX), _cols(tail, C_MBC)], axis=-1))
        outs[7].append(ssm_p)

        caches = (ck, cv, cmk, cmv, page_table, state_rwkv_shift[l], rws, state_conv[l], state_ssm)
        xs, ps, rw_s, ssm_s, vf_s = _decode_layer(xs, l, lw, caches, vf_s, final_g, final)
        outs[8].append(_cols(ps, C_DK).reshape(n, 1, H_DIFF, 2, DQK))
        outs[9].append(_cols(ps, C_DV).reshape(n, 1, H_DIFF, DV))
        outs[10].append(jnp.concatenate([_cols(ps, C_RR), _cols(ps, C_RK), _cols(ps, C_RV),
                                         _cols(ps, C_LORA128, 128)], axis=-1))
        outs[11].append(rw_s.reshape(H_RWKV, R_HEAD, R_HEAD, n).transpose(3, 0, 1, 2))
        outs[12].append(jnp.concatenate(
            [state_conv[l][:, 1:], jnp.concatenate([_cols(ps, C_MX), _cols(ps, C_MBC)], axis=-1)[:, None]], axis=1))
        outs[13].append(ssm_s.reshape(n, H_MAMBA, M_HEADDIM, M_STATE))

    return (xp.reshape(nb, t, D_MODEL), xs.reshape(n, 1, D_MODEL)) + tuple(jnp.stack(o) for o in outs)
```
